```python
import math
import jax, jax.numpy as jnp
from jax import lax
import numpy as np

D_MODEL = 1024
BATCH = 2
SEQ = 8192
DEPTH = 2

GRID_W = 64
CTX_LEN = 256
NORM_EPS = 1e-6
A_HEADS = 4
A_DK = 128
A_DV = 128
A_CONV = 5
A_CHUNK = 64
B_HEADS = 8
B_Q_RANK = 384
B_KV_RANK = 256
B_NOPE = 64
B_ROPE = 32
B_DV = 64
B_QBLOCK = 128
ROPE_THETA = 10000.0
C_HEADS = 16
C_DH = D_MODEL // C_HEADS
WIN_R = 8
WIN_C = 16
D_FF = 2816
N_EXPERTS = 8
TOP_K = 2
D_FF_EXPERT = 3584

N_EVEN = (DEPTH + 1) // 2
N_ODD = DEPTH // 2
IN_SPLITS = (A_HEADS * A_DK, A_HEADS * A_DK, A_HEADS * A_DV, A_HEADS * A_DV, 2 * A_HEADS, 2 * A_HEADS, B_Q_RANK, B_KV_RANK, B_ROPE)
IN_DIM = sum(IN_SPLITS)
MIX_OUT = A_HEADS * A_DV + B_HEADS * B_DV

kernel_name = 'hybrid_gdn_mla_nat_moe_dit'

F32 = jnp.float32


def rmsnorm(x, g):
    xf = x.astype(F32)
    y = xf * lax.rsqrt(jnp.mean(xf * xf, axis=-1, keepdims=True) + NORM_EPS)
    return (y * g.astype(F32)).astype(x.dtype)


def l2norm(x):
    xf = x.astype(F32)
    return (xf * lax.rsqrt(jnp.sum(xf * xf, axis=-1, keepdims=True) + 1e-6)).astype(x.dtype)


def split_cols(t, sizes):
    offs = [int(o) for o in np.cumsum(sizes)[:-1]]
    return jnp.split(t, offs, axis=-1)


def dwconv_centred(x, w):
    k = w.shape[0]
    return lax.conv_general_dilated(x, w[:, None, :].astype(x.dtype), window_strides=(1,), padding=[((k - 1) // 2, (k - 1) // 2)], dimension_numbers=('NWC', 'WIO', 'NWC'), feature_group_count=x.shape[-1])


def axial_rope_tables(L):
    t = jnp.arange(L)
    row = (t // GRID_W).astype(F32)
    col = (t % GRID_W).astype(F32)
    half = B_ROPE // 2
    inv = ROPE_THETA ** (-jnp.arange(0, half, 2, dtype=F32) / half)
    ar = row[:, None] * inv[None, :]
    ac = col[:, None] * inv[None, :]
    ang = jnp.concatenate([ar, ar, ac, ac], axis=-1)
    return jnp.cos(ang), jnp.sin(ang)


def apply_rope(x, cos, sin):
    x1, x2, x3, x4 = jnp.split(x.astype(F32), 4, axis=-1)
    rot = jnp.concatenate([-x2, x1, -x4, x3], axis=-1)
    shp = (cos.shape[0],) + (1,) * (x.ndim - 3) + (cos.shape[1],)
    return (x.astype(F32) * cos.reshape(shp) + rot * sin.reshape(shp)).astype(x.dtype)


def gated_delta_chunked(q, k, v, g, beta, s0):
    b, h, L, dk = q.shape
    dv = v.shape[-1]
    n = L // A_CHUNK

    def blocks(t):
        return t.astype(F32).reshape(b, h, n, A_CHUNK, *t.shape[3:])

    q = blocks(q) * (dk ** -0.5)
    k, v, g, beta = blocks(k), blocks(v), blocks(g), blocks(beta)
    gc = jnp.cumsum(g, axis=-1)
    lower = jnp.tril(jnp.ones((A_CHUNK, A_CHUNK), bool))
    strict = jnp.tril(jnp.ones((A_CHUNK, A_CHUNK), bool), -1)
    diff = gc[..., :, None] - gc[..., None, :]
    decay = jnp.where(lower, jnp.exp(jnp.where(lower, diff, 0.0)), 0.0)
    kb = k * beta[..., None]
    a_mat = jnp.where(strict, jnp.einsum('bhnid,bhnjd->bhnij', kb, k) * decay, 0.0)
    tmat = a_mat + jnp.eye(A_CHUNK, dtype=F32)
    u = lax.linalg.triangular_solve(tmat, v * beta[..., None], left_side=True, lower=True, unit_diagonal=True)
    w = lax.linalg.triangular_solve(tmat, kb * jnp.exp(gc)[..., None], left_side=True, lower=True, unit_diagonal=True)
    qk = jnp.einsum('bhnid,bhnjd->bhnij', q, k) * decay
    qg = q * jnp.exp(gc)[..., None]
    kg = k * jnp.exp(gc[..., -1:] - gc)[..., None]
    g_last = jnp.exp(gc[..., -1])

    def step(s, xs):
        qg_i, kg_i, u_i, w_i, qk_i, gl_i = xs
        v_new = u_i - jnp.einsum('bhid,bhde->bhie', w_i, s)
        o = jnp.einsum('bhid,bhde->bhie', qg_i, s) + jnp.einsum('bhij,bhje->bhie', qk_i, v_new)
        s = s * gl_i[..., None, None] + jnp.einsum('bhid,bhie->bhde', kg_i, v_new)
        return s, o

    xs = (jnp.moveaxis(qg, 2, 0), jnp.moveaxis(kg, 2, 0), jnp.moveaxis(u, 2, 0), jnp.moveaxis(w, 2, 0), jnp.moveaxis(qk, 2, 0), jnp.moveaxis(g_last, 2, 0))
    s_final, o = lax.scan(step, s0, xs)
    return jnp.moveaxis(o, 0, 2).reshape(b, h, L, dv), s_final


def gdn_prepare(q, k, v, a, bg, conv_w, a_log, dt_bias):
    bsz, L, _ = q.shape
    qkv = jax.nn.silu(dwconv_centred(jnp.concatenate([q, k, v], axis=-1), conv_w))
    q, k, v = split_cols(qkv, (A_HEADS * A_DK, A_HEADS * A_DK, A_HEADS * A_DV))
    q = l2norm(q.reshape(bsz, L, A_HEADS, A_DK))
    k = l2norm(k.reshape(bsz, L, A_HEADS, A_DK))
    v = v.reshape(bsz, L, A_HEADS, A_DV)
    a = a.reshape(bsz, L, 2, A_HEADS).astype(F32)
    bg = bg.reshape(bsz, L, 2, A_HEADS).astype(F32)
    g = -jnp.exp(a_log.astype(F32)) * jax.nn.softplus(a + dt_bias.astype(F32))
    beta = jax.nn.sigmoid(bg)
    return (jnp.moveaxis(q, 1, 2), jnp.moveaxis(k, 1, 2), jnp.moveaxis(v, 1, 2), jnp.moveaxis(g, 1, -1), jnp.moveaxis(beta, 1, -1))


def gdn_bidirectional(ctx_in, lat_in):
    qc, kc, vc, gcx, bcx = ctx_in
    ql, kl, vl, glt, blt = lat_in
    s0 = jnp.zeros((qc.shape[0], A_HEADS, A_DK, A_DV), F32)
    fl = lambda t: jnp.flip(t, axis=2)
    oc_f, sc_f = gated_delta_chunked(qc, kc, vc, gcx[:, 0], bcx[:, 0], s0)
    ol_f, _ = gated_delta_chunked(ql, kl, vl, glt[:, 0], blt[:, 0], sc_f)
    oc_b, sc_b = gated_delta_chunked(fl(qc), fl(kc), fl(vc), fl(gcx[:, 1]), fl(bcx[:, 1]), s0)
    ol_b, _ = gated_delta_chunked(fl(ql), fl(kl), fl(vl), fl(glt[:, 1]), fl(blt[:, 1]), sc_b)
    return oc_f + fl(oc_b), ol_f + fl(ol_b)


def gdn_output(o, z, g):
    bsz, H, L, dv = o.shape
    o = rmsnorm(jnp.moveaxis(o, 1, 2), g) * jax.nn.silu(z.reshape(bsz, L, H, dv).astype(F32))
    return o.reshape(bsz, L, H * dv).astype(z.dtype)


def mla_project(cq, ckv, kr, q_g, w_uq, kv_g, w_ukv):
    bsz, L, _ = cq.shape
    q = (rmsnorm(cq, q_g) @ w_uq).reshape(bsz, L, B_HEADS, B_NOPE + B_ROPE)
    kv = (rmsnorm(ckv, kv_g) @ w_ukv).reshape(bsz, L, B_HEADS, B_NOPE + B_DV)
    return q[..., :B_NOPE], q[..., B_NOPE:], kv[..., :B_NOPE], kr, kv[..., B_NOPE:]


def mla_attend(qn, qr, kn, kr, v):
    s = jnp.einsum('bqhd,bkhd->bhqk', qn, kn).astype(F32) + jnp.einsum('bqhr,bkr->bhqk', qr, kr).astype(F32)
    p = jax.nn.softmax(s * ((B_NOPE + B_ROPE) ** -0.5), axis=-1).astype(v.dtype)
    return jnp.einsum('bhqk,bkhd->bqhd', p, v)


def mla_latent_attention(qn, qr, kn, kr, v):
    bsz, L, H, _ = qn.shape
    nb = L // B_QBLOCK
    blk = lambda t: jnp.moveaxis(t.reshape(bsz, nb, B_QBLOCK, *t.shape[2:]), 1, 0)
    o = lax.map(lambda qs: mla_attend(qs[0], qs[1], kn, kr, v), (blk(qn), blk(qr)))
    return jnp.moveaxis(o, 0, 1).reshape(bsz, L, H * B_DV)


def dense_attention(q, k, v):
    s = jnp.einsum('bqhd,bkhd->bhqk', q, k).astype(F32) * (q.shape[-1] ** -0.5)
    p = jax.nn.softmax(s, axis=-1).astype(v.dtype)
    return jnp.einsum('bhqk,bkhd->bqhd', p, v)


def neighbourhood_attention(q, k, v, k_ctx, v_ctx, rpb):
    bsz, L, H, dh = q.shape
    rows = L // GRID_W
    wr = min(WIN_R, rows)
    n_nb = wr * WIN_C
    scale = dh ** -0.5
    qg = q.reshape(bsz, rows, GRID_W, H, dh)
    kg = k.reshape(bsz, rows, GRID_W, H, dh)
    vg = v.reshape(bsz, rows, GRID_W, H, dh)
    col = jnp.arange(GRID_W)
    col_idx = jnp.clip(col - WIN_C // 2, 0, GRID_W - WIN_C)[:, None] + jnp.arange(WIN_C)[None, :]
    dc_idx = col_idx - col[:, None] + (WIN_C - 1)

    def row_block(r):
        r0 = jnp.clip(r - wr // 2, 0, rows - wr)
        k_win = lax.dynamic_slice_in_dim(kg, r0, wr, axis=1)[:, :, col_idx]
        v_win = lax.dynamic_slice_in_dim(vg, r0, wr, axis=1)[:, :, col_idx]
        q_r = lax.dynamic_index_in_dim(qg, r, axis=1, keepdims=False)
        dr_idx = r0 + jnp.arange(wr) - r + (WIN_R - 1)
        bias = rpb[:, dr_idx[None, :, None], dc_idx[:, None, :]].astype(F32)
        s_nb = jnp.einsum('bqhd,brqwhd->bhqrw', q_r, k_win).astype(F32) * scale + bias
        s_ctx = jnp.einsum('bqhd,bkhd->bhqk', q_r, k_ctx).astype(F32) * scale
        s = jnp.concatenate([s_nb.reshape(bsz, H, GRID_W, n_nb), s_ctx], axis=-1)
        p = jax.nn.softmax(s, axis=-1).astype(v.dtype)
        p_nb = p[..., :n_nb].reshape(bsz, H, GRID_W, wr, WIN_C)
        return jnp.einsum('bhqrw,brqwhd->bqhd', p_nb, v_win) + jnp.einsum('bhqk,bkhd->bqhd', p[..., n_nb:], v_ctx)

    o = lax.map(row_block, jnp.arange(rows))
    return jnp.moveaxis(o, 0, 1).reshape(bsz, L, H, dh)


def even_mixer(h_c, h_l, cos, sin, need_ctx, w_in, conv_w, a_log, dt_bias, gdn_g, q_g, w_uq, kv_g, w_ukv, w_out):
    pc = split_cols(h_c @ w_in, IN_SPLITS)
    pl = split_cols(h_l @ w_in, IN_SPLITS)
    a_ctx = gdn_prepare(pc[0], pc[1], pc[2], pc[4], pc[5], conv_w, a_log, dt_bias)
    a_lat = gdn_prepare(pl[0], pl[1], pl[2], pl[4], pl[5], conv_w, a_log, dt_bias)
    oa_c, oa_l = gdn_bidirectional(a_ctx, a_lat)
    ya_l = gdn_output(oa_l, pl[3], gdn_g)
    qn_c, qr_c, kn_c, kr_c, v_c = mla_project(pc[6], pc[7], pc[8], q_g, w_uq, kv_g, w_ukv)
    qn_l, qr_l, kn_l, kr_l, v_l = mla_project(pl[6], pl[7], pl[8], q_g, w_uq, kv_g, w_ukv)
    qr_l = apply_rope(qr_l, cos, sin)
    kr_l = apply_rope(kr_l, cos, sin)
    yb_l = mla_latent_attention(qn_l, qr_l, jnp.concatenate([kn_c, kn_l], axis=1), jnp.concatenate([kr_c, kr_l], axis=1), jnp.concatenate([v_c, v_l], axis=1))
    out_l = jnp.concatenate([ya_l, yb_l], axis=-1) @ w_out
    out_c = None
    if need_ctx:
        bsz, Lc, _ = h_c.shape
        ya_c = gdn_output(oa_c, pc[3], gdn_g)
        yb_c = mla_attend(qn_c, qr_c, kn_c, kr_c, v_c).reshape(bsz, Lc, B_HEADS * B_DV)
        out_c = jnp.concatenate([ya_c, yb_c], axis=-1) @ w_out
    return out_c, out_l


def odd_mixer(h_c, h_l, need_ctx, w_qkv, rpb, w_out):
    bsz, L, D = h_l.shape
    Lc = h_c.shape[1]
    q_l, k_l, v_l = [t.reshape(bsz, L, C_HEADS, C_DH) for t in jnp.split(h_l @ w_qkv, 3, axis=-1)]
    q_c, k_c, v_c = [t.reshape(bsz, Lc, C_HEADS, C_DH) for t in jnp.split(h_c @ w_qkv, 3, axis=-1)]
    out_l = neighbourhood_attention(q_l, k_l, v_l, k_c, v_c, rpb).reshape(bsz, L, D) @ w_out
    out_c = None
    if need_ctx:
        out_c = dense_attention(q_c, k_c, v_c).reshape(bsz, Lc, D) @ w_out
    return out_c, out_l


def swiglu(h, wg, wu, wd):
    return (jax.nn.silu(h @ wg) * (h @ wu)) @ wd


def moe_swiglu(h, w_router, wg, wu, wd):
    shp = h.shape
    t = h.reshape(-1, shp[-1])
    logits = (t @ w_router).astype(F32)
    top_v, top_i = lax.top_k(logits, TOP_K)
    gates = jax.nn.softmax(top_v, axis=-1)
    out = jnp.zeros_like(t)
    for e in range(N_EXPERTS):
        w_e = jnp.sum(jnp.where(top_i == e, gates, 0.0), axis=-1).astype(t.dtype)
        out = out + w_e[:, None] * swiglu(t, wg[e], wu[e], wd[e])
    return out.reshape(shp)


def setup_inputs(seed: int = 0) -> dict:
    key = jax.random.key(seed)
    ks = iter(list(jax.random.split(key, 40)))
    D = D_MODEL
    nrm = lambda shape, scale: jax.random.normal(next(ks), shape, F32) * scale
    gain = lambda shape: 1.0 + 0.02 * jax.random.normal(next(ks), shape, F32)
    dt = jnp.exp(jax.random.uniform(next(ks), (N_EVEN, 2, A_HEADS), F32, math.log(1e-3), math.log(1e-1)))
    return {
        'x': nrm((BATCH, SEQ, D), 1.0),
        'c': nrm((BATCH, D), 1.0),
        'ctx': nrm((BATCH, CTX_LEN, D), 1.0),
        'c_ctx': nrm((D,), 1.0),
        'ada_w': nrm((DEPTH, D, 6 * D), 0.5 * D ** -0.5),
        'ada_b': nrm((DEPTH, 6 * D), 0.01),
        'norm_g': gain((DEPTH, 2, D)),
        'ev_w_in': nrm((N_EVEN, D, IN_DIM), D ** -0.5),
        'ev_conv_w': nrm((N_EVEN, A_CONV, 2 * A_HEADS * A_DK + A_HEADS * A_DV), A_CONV ** -0.5),
        'ev_a_log': jnp.log(jax.random.uniform(next(ks), (N_EVEN, 2, A_HEADS), F32, 1.0, 16.0)),
        'ev_dt_bias': dt + jnp.log(-jnp.expm1(-dt)),
        'ev_gdn_norm_g': gain((N_EVEN, A_DV)),
        'ev_q_norm_g': gain((N_EVEN, B_Q_RANK)),
        'ev_w_uq': nrm((N_EVEN, B_Q_RANK, B_HEADS * (B_NOPE + B_ROPE)), B_Q_RANK ** -0.5),
        'ev_kv_norm_g': gain((N_EVEN, B_KV_RANK)),
        'ev_w_ukv': nrm((N_EVEN, B_KV_RANK, B_HEADS * (B_NOPE + B_DV)), B_KV_RANK ** -0.5),
        'ev_w_out': nrm((N_EVEN, MIX_OUT, D), MIX_OUT ** -0.5),
        'ev_ffn_wg': nrm((N_EVEN, D, D_FF), D ** -0.5),
        'ev_ffn_wu': nrm((N_EVEN, D, D_FF), D ** -0.5),
        'ev_ffn_wd': nrm((N_EVEN, D_FF, D), D_FF ** -0.5),
        'od_w_qkv': nrm((N_ODD, D, 3 * D), D ** -0.5),
        'od_rpb': nrm((N_ODD, C_HEADS, 2 * WIN_R - 1, 2 * WIN_C - 1), 0.1),
        'od_w_out': nrm((N_ODD, D, D), D ** -0.5),
        'od_router': nrm((N_ODD, D, N_EXPERTS), D ** -0.5),
        'od_exp_wg': nrm((N_ODD, N_EXPERTS, D, D_FF_EXPERT), D ** -0.5),
        'od_exp_wu': nrm((N_ODD, N_EXPERTS, D, D_FF_EXPERT), D ** -0.5),
        'od_exp_wd': nrm((N_ODD, N_EXPERTS, D_FF_EXPERT, D), D_FF_EXPERT ** -0.5),
        'final_norm_g': gain((D,)),
    }


def reference(x, c, ctx, c_ctx, ada_w, ada_b, norm_g, ev_w_in, ev_conv_w, ev_a_log, ev_dt_bias, ev_gdn_norm_g, ev_q_norm_g, ev_w_uq, ev_kv_norm_g, ev_w_ukv, ev_w_out, ev_ffn_wg, ev_ffn_wu, ev_ffn_wd, od_w_qkv, od_rpb, od_w_out, od_router, od_exp_wg, od_exp_wu, od_exp_wd, final_norm_g):
    L = x.shape[1]
    cos, sin = axial_rope_tables(L)
    s_c = jax.nn.silu(c)
    s_cc = jax.nn.silu(c_ctx)
    x_l, x_c = x, ctx
    for i in range(DEPTH):
        need_ctx = i < DEPTH - 1
        j = i // 2
        mod_l = (s_c @ ada_w[i] + ada_b[i])[:, None, :]
        mod_c = (s_cc @ ada_w[i] + ada_b[i])[None, None, :]
        sh1_l, sc1_l, g1_l, sh2_l, sc2_l, g2_l = jnp.split(mod_l, 6, axis=-1)
        sh1_c, sc1_c, g1_c, sh2_c, sc2_c, g2_c = jnp.split(mod_c, 6, axis=-1)
        h_l = rmsnorm(x_l, norm_g[i, 0]) * (1.0 + sc1_l) + sh1_l
        h_c = rmsnorm(x_c, norm_g[i, 0]) * (1.0 + sc1_c) + sh1_c
        if i % 2 == 0:
            out_c, out_l = even_mixer(h_c, h_l, cos, sin, need_ctx, ev_w_in[j], ev_conv_w[j], ev_a_log[j], ev_dt_bias[j], ev_gdn_norm_g[j], ev_q_norm_g[j], ev_w_uq[j], ev_kv_norm_g[j], ev_w_ukv[j], ev_w_out[j])
        else:
            out_c, out_l = odd_mixer(h_c, h_l, need_ctx, od_w_qkv[j], od_rpb[j], od_w_out[j])
        x_l = x_l + g1_l * out_l
        f_l = rmsnorm(x_l, norm_g[i, 1]) * (1.0 + sc2_l) + sh2_l
        if i % 2 == 0:
            x_l = x_l + g2_l * swiglu(f_l, ev_ffn_wg[j], ev_ffn_wu[j], ev_ffn_wd[j])
        else:
            x_l = x_l + g2_l * moe_swiglu(f_l, od_router[j], od_exp_wg[j], od_exp_wu[j], od_exp_wd[j])
        if need_ctx:
            x_c = x_c + g1_c * out_c
            f_c = rmsnorm(x_c, norm_g[i, 1]) * (1.0 + sc2_c) + sh2_c
            if i % 2 == 0:
                x_c = x_c + g2_c * swiglu(f_c, ev_ffn_wg[j], ev_ffn_wu[j], ev_ffn_wd[j])
            else:
                x_c = x_c + g2_c * moe_swiglu(f_c, od_router[j], od_exp_wg[j], od_exp_wu[j], od_exp_wd[j])
    return rmsnorm(x_l, final_norm_g)
```

```python
import functools

import numpy as np
import jax
import jax.numpy as jnp
from jax import lax
from jax.experimental import pallas as pl
from jax.experimental.pallas import tpu as pltpu

F32 = jnp.float32
BF16 = jnp.bfloat16
HIGHEST = lax.Precision.HIGHEST

GRID_W = 64
NORM_EPS = 1e-6
A_HEADS, A_DK, A_DV, A_CHUNK = 4, 128, 128, 64
B_HEADS, B_Q_RANK, B_KV_RANK, B_NOPE, B_ROPE, B_DV = 8, 384, 256, 64, 32, 64
ROPE_THETA = 10000.0
C_HEADS, C_DH = 16, 64
WIN_R, WIN_C = 8, 16
N_EXPERTS = 8
LANES = 128
MOD_ROWS = 8
NEG = -1e30
VMEM_LIMIT = 56 * 1024 * 1024

TM = 256


def _params(sem):
    return pltpu.CompilerParams(dimension_semantics=sem, vmem_limit_bytes=VMEM_LIMIT)


def _const_spec(shape):
    nd = len(shape)
    return pl.BlockSpec(shape, lambda *_: (0,) * nd, pipeline_mode=pl.Buffered(1))


def _rms(x):
    return x * lax.rsqrt(jnp.mean(x * x, axis=-1, keepdims=True) + NORM_EPS)


def _silu(x):
    return x * jax.nn.sigmoid(x)


def _dot(a, b, **kw):
    return jnp.dot(a, b, preferred_element_type=F32, **kw)


def _dot_nt(a, b, **kw):
    return lax.dot_general(a, b, (((1,), (1,)), ((), ())), preferred_element_type=F32, **kw)


def _ada_kernel(s_ref, w_ref, b_ref, o_ref):
    s = _silu(s_ref[...])
    o_ref[0] = _dot(s, w_ref[0], precision=HIGHEST) + b_ref[0]


def _ada_mod(cs, ada_w, ada_b):
    depth, d, d6 = ada_w.shape
    tn = d6 // 4
    return pl.pallas_call(
        _ada_kernel,
        out_shape=jax.ShapeDtypeStruct((depth, MOD_ROWS, d6), F32),
        grid=(depth, d6 // tn),
        in_specs=[
            pl.BlockSpec((MOD_ROWS, d), lambda i, j: (0, 0)),
            pl.BlockSpec((1, d, tn), lambda i, j: (i, 0, j)),
            pl.BlockSpec((1, 1, tn), lambda i, j: (i, 0, j)),
        ],
        out_specs=pl.BlockSpec((1, MOD_ROWS, tn), lambda i, j: (i, 0, j)),
        compiler_params=_params(("parallel", "parallel")),
        name="ada_mod",
    )(cs, ada_w, ada_b.reshape(depth, 1, d6))


N_QKVZ = 4 * A_HEADS * A_DK
IN_OFF_AB = N_QKVZ
IN_OFF_CQ = IN_OFF_AB + LANES
IN_OFF_CKV = IN_OFF_CQ + B_Q_RANK
IN_OFF_KR = IN_OFF_CKV + B_KV_RANK
IN_COLS = IN_OFF_KR + LANES
HW = B_HEADS * LANES


def _proj0_kernel(x_ref, mod_ref, ng_ref, win_ref, qg_ref, wq_ref, wqr_ref, kvg_ref, wk_ref, wv_ref,
                  e_ref, er_ref, cq_ref, sq_ref, ck_ref, sk_ref,
                  qkv_ref, z_ref, ab_ref, q_ref, k_ref, v_ref):
    x = x_ref[0]
    m = mod_ref[0]
    h = _rms(x) * ng_ref[...] * (1.0 + m[1:2]) + m[0:1]
    p = _dot(h.astype(BF16), win_ref[...])
    qkv_ref[0] = p[:, :3 * A_HEADS * A_DK]
    z_ref[0] = p[:, 3 * A_HEADS * A_DK:N_QKVZ]
    ab_ref[0] = p[:, IN_OFF_AB:IN_OFF_CQ]
    nq = (_rms(p[:, IN_OFF_CQ:IN_OFF_CKV]) * qg_ref[...]).astype(BF16)
    nkv = (_rms(p[:, IN_OFF_CKV:IN_OFF_KR]) * kvg_ref[...]).astype(BF16)
    krp = p[:, IN_OFF_KR:IN_COLS]
    qa = _dot(nq, wq_ref[...])
    qb = _dot(nq, wqr_ref[...])
    kn = _dot(nkv, wk_ref[...])
    vv = _dot(nkv, wv_ref[...])
    ka = _dot(krp, e_ref[...], precision=HIGHEST)
    kb = _dot(krp, er_ref[...], precision=HIGHEST)
    cq, sq, ck, sk = cq_ref[...], sq_ref[...], ck_ref[...], sk_ref[...]
    for hh in range(B_HEADS):
        sl = slice(hh * LANES, (hh + 1) * LANES)
        q_ref[0, hh] = (qa[:, sl] * cq + qb[:, sl] * sq).astype(BF16)
        k_ref[0, hh] = (kn[:, sl] + ka[:, sl] * ck + kb[:, sl] * sk).astype(BF16)
        v_ref[0, hh] = vv[:, sl].astype(BF16)


def _rot_cols(w):
    q = B_ROPE // 4
    return jnp.concatenate([-w[:, q:2 * q], w[:, :q], -w[:, 3 * q:], w[:, 2 * q:3 * q]], axis=1)


def _proj0_weights(w_in, w_uq, w_ukv):
    d = w_in.shape[0]
    offs = np.cumsum([0, 512, 512, 512, 512, 8, 8, B_Q_RANK, B_KV_RANK, B_ROPE])
    zeros = lambda n: jnp.zeros((d, n), F32)
    kr = w_in[:, offs[8]:offs[9]]
    win = jnp.concatenate([
        w_in[:, :offs[4]],
        w_in[:, offs[4]:offs[6]], zeros(LANES - 16),
        w_in[:, offs[6]:offs[7]],
        w_in[:, offs[7]:offs[8]],
        kr, zeros(LANES - B_ROPE),
    ], axis=1).astype(BF16)
    dq = B_NOPE + B_ROPE
    wq3 = w_uq.reshape(B_Q_RANK, B_HEADS, dq)
    zq = jnp.zeros((B_Q_RANK, B_HEADS, LANES - dq), F32)
    wq = jnp.concatenate([wq3, zq], axis=2).reshape(B_Q_RANK, HW).astype(BF16)
    rot = jnp.stack([_rot_cols(wq3[:, hh, B_NOPE:]) for hh in range(B_HEADS)], axis=1)
    wqr = jnp.concatenate([jnp.zeros((B_Q_RANK, B_HEADS, B_NOPE), F32), rot, zq], axis=2)
    wqr = wqr.reshape(B_Q_RANK, HW).astype(BF16)
    wkv3 = w_ukv.reshape(B_KV_RANK, B_HEADS, B_NOPE + B_DV)
    zk = jnp.zeros((B_KV_RANK, B_HEADS, LANES - B_NOPE), F32)
    wk = jnp.concatenate([wkv3[:, :, :B_NOPE], zk], axis=2).reshape(B_KV_RANK, HW).astype(BF16)
    vpart = wkv3[:, :, B_NOPE:]
    zv = jnp.zeros_like(vpart)
    even = (jnp.arange(B_HEADS) % 2 == 0)[None, :, None]
    wv = jnp.concatenate([jnp.where(even, vpart, zv), jnp.where(even, zv, vpart)], axis=2)
    wv = wv.reshape(B_KV_RANK, HW).astype(BF16)
    return win, wq, wqr, wk, wv


def _rope_select():
    e = np.zeros((LANES, HW), np.float32)
    er = np.zeros((LANES, HW), np.float32)
    q = B_ROPE // 4
    rmat = np.zeros((B_ROPE, B_ROPE), np.float32)
    for i in range(q):
        rmat[q + i, i] = -1.0
        rmat[i, q + i] = 1.0
        rmat[3 * q + i, 2 * q + i] = -1.0
        rmat[2 * q + i, 3 * q + i] = 1.0
    for hh in range(B_HEADS):
        base = hh * LANES + B_NOPE
        e[:B_ROPE, base:base + B_ROPE] = np.eye(B_ROPE, dtype=np.float32)
        er[:B_ROPE, base:base + B_ROPE] = rmat
    return jnp.asarray(e), jnp.asarray(er)


def _rope_tables(lc, l):
    t = jnp.arange(l)
    row = (t // GRID_W).astype(F32)
    col = (t % GRID_W).astype(F32)
    half = B_ROPE // 2
    inv = ROPE_THETA ** (-jnp.arange(0, half, 2, dtype=F32) / half)
    ar = row[:, None] * inv[None, :]
    ac = col[:, None] * inv[None, :]
    ang = jnp.concatenate([ar, ar, ac, ac], axis=-1)
    cos = jnp.concatenate([jnp.ones((lc, B_ROPE), F32), jnp.cos(ang)], axis=0)
    sin = jnp.concatenate([jnp.zeros((lc, B_ROPE), F32), jnp.sin(ang)], axis=0)
    tt = lc + l
    scale = (B_NOPE + B_ROPE) ** -0.5
    pad = jnp.zeros((tt, LANES - B_NOPE - B_ROPE), F32)
    z64 = jnp.zeros((tt, B_NOPE), F32)
    cq = jnp.concatenate([jnp.full((tt, B_NOPE), scale, F32), scale * cos, pad], axis=1)
    sq = jnp.concatenate([z64, scale * sin, pad], axis=1)
    ck = jnp.concatenate([z64, cos, pad], axis=1)
    sk = jnp.concatenate([z64, sin, pad], axis=1)
    return cq, sq, ck, sk


def _mod_row(b, t, n_ctx_tiles, bsz):
    return jnp.where(t < n_ctx_tiles, bsz, b)


def _proj0(xa, mod, ng, weights, tables, lc):
    bsz, tt, d = xa.shape
    win, qg, wq, wqr, kvg, wk, wv = weights
    e, er = _rope_select()
    nct = lc // TM
    tok = lambda w: pl.BlockSpec((1, TM, w), lambda b, t: (b, t, 0))
    head = pl.BlockSpec((1, B_HEADS, TM, LANES), lambda b, t: (b, 0, t, 0))
    tab = pl.BlockSpec((TM, LANES), lambda b, t: (t, 0))
    hshape = jax.ShapeDtypeStruct((bsz, B_HEADS, tt, LANES), BF16)
    return pl.pallas_call(
        _proj0_kernel,
        out_shape=(
            jax.ShapeDtypeStruct((bsz, tt, 3 * A_HEADS * A_DK), F32),
            jax.ShapeDtypeStruct((bsz, tt, A_HEADS * A_DV), F32),
            jax.ShapeDtypeStruct((bsz, tt, LANES), F32),
            hshape, hshape, hshape,
        ),
        grid=(bsz, tt // TM),
        in_specs=[
            tok(d),
            pl.BlockSpec((1, 6, d), lambda b, t: (_mod_row(b, t, nct, bsz), 0, 0)),
            _const_spec((1, d)), _const_spec(win.shape),
            _const_spec((1, B_Q_RANK)), _const_spec(wq.shape), _const_spec(wqr.shape),
            _const_spec((1, B_KV_RANK)), _const_spec(wk.shape), _const_spec(wv.shape),
            _const_spec(e.shape), _const_spec(er.shape),
            tab, tab, tab, tab,
        ],
        out_specs=(tok(3 * A_HEADS * A_DK), tok(A_HEADS * A_DV), tok(LANES), head, head, head),
        compiler_params=_params(("parallel", "parallel")),
        name="proj0",
    )(xa, mod, ng.reshape(1, d), win, qg.reshape(1, -1), wq, wqr, kvg.reshape(1, -1), wk, wv, e, er, *tables)


def _gdn_prepare(qkv_pre, ab, conv_w, a_log, dt_bias, lc):
    bsz, tt, _ = qkv_pre.shape
    kk = conv_w.shape[0]

    def conv(x):
        return lax.conv_general_dilated(x, conv_w[:, None, :], window_strides=(1,),
                                        padding=[((kk - 1) // 2, (kk - 1) // 2)],
                                        dimension_numbers=('NWC', 'WIO', 'NWC'),
                                        feature_group_count=x.shape[-1])

    y = jnp.concatenate([conv(qkv_pre[:, :lc]), conv(qkv_pre[:, lc:])], axis=1)
    y = _silu(y)
    nq = A_HEADS * A_DK

    def l2(t):
        t = t.reshape(bsz, tt, A_HEADS, A_DK)
        t = t * lax.rsqrt(jnp.sum(t * t, axis=-1, keepdims=True) + 1e-6)
        return t.reshape(bsz, tt, nq)

    q = l2(y[..., :nq]) * (A_DK ** -0.5)
    k = l2(y[..., nq:2 * nq])
    v = y[..., 2 * nq:]
    a = ab[..., :2 * A_HEADS].reshape(bsz, tt, 2, A_HEADS)
    bg = ab[..., 2 * A_HEADS:4 * A_HEADS].reshape(bsz, tt, 2, A_HEADS)
    g = -jnp.exp(a_log) * jax.nn.softplus(a + dt_bias)
    beta = jax.nn.sigmoid(bg)
    gb = jnp.moveaxis(jnp.concatenate([g, beta], axis=-1), 2, 1)
    nc = tt // A_CHUNK
    gt = jnp.swapaxes(gb.reshape(bsz, 2, nc, A_CHUNK, 2 * A_HEADS), 3, 4)
    return q, k, v, gb, gt


def _gdn_kernel(q_ref, k_ref, v_ref, gb_ref, gt_ref, o_ref, s_ref):
    d = pl.program_id(1)
    step = pl.program_id(2)

    @pl.when(step == 0)
    def _():
        s_ref[...] = jnp.zeros_like(s_ref)

    c = A_CHUNK
    row = lax.broadcasted_iota(jnp.int32, (c, c), 0)
    col = lax.broadcasted_iota(jnp.int32, (c, c), 1)
    dif = (row - col) * (1 - 2 * d)
    incl = dif >= 0
    strict = dif > 0
    m_incl = incl.astype(F32)
    eye = (row == col).astype(F32)
    gb = gb_ref[0, 0]
    gt = gt_ref[0, 0, 0]
    gc_col = _dot(m_incl, gb, precision=HIGHEST)
    gc_row = _dot_nt(gt, m_incl, precision=HIGHEST)
    g_tot = jnp.sum(gb, axis=0, keepdims=True)
    for h in range(A_HEADS):
        sl = slice(h * A_DK, (h + 1) * A_DK)
        qh, kh, vh = q_ref[0, :, sl], k_ref[0, :, sl], v_ref[0, :, sl]
        beta = gb[:, A_HEADS + h:A_HEADS + h + 1]
        gcc = gc_col[:, h:h + 1]
        gcr = gc_row[h:h + 1, :]
        gtot = g_tot[:, h:h + 1]
        decay = jnp.where(incl, jnp.exp(jnp.where(incl, gcc - gcr, 0.0)), 0.0)
        kb = kh * beta
        khb = kh.astype(BF16)
        n = -jnp.where(strict, _dot_nt(kb.astype(BF16), khb) * decay, 0.0)
        tinv = eye + n
        pw = n
        for _ in range(5):
            pw = _dot(pw, pw, precision=HIGHEST)
            tinv = tinv + _dot(tinv, pw, precision=HIGHEST)
        eg = jnp.exp(gcc)
        rhs = jnp.concatenate([vh * beta, kb * eg], axis=1)
        uw = _dot(tinv, rhs, precision=HIGHEST)
        u, w = uw[:, :A_DV], uw[:, A_DV:]
        qk = jnp.where(incl, _dot_nt(qh.astype(BF16), khb) * decay, 0.0)
        qg = qh * eg
        kg = kh * jnp.exp(gtot - gcc)
        s = s_ref[h]
        sb = s.astype(BF16)
        v_new = u - _dot(w.astype(BF16), sb)
        vb = v_new.astype(BF16)
        o_ref[0, 0, :, sl] = _dot(qg.astype(BF16), sb) + _dot(qk.astype(BF16), vb)
        s_ref[h] = s * jnp.exp(gtot) + _dot(kg.T.astype(BF16), vb)


def _gdn_scan(q, k, v, gb, gt, lc):
    bsz, tt, _ = q.shape
    nc = tt // A_CHUNK
    ncc = lc // A_CHUNK

    def chunk(d, s):
        bwd = jnp.where(s < ncc, ncc - 1 - s, nc - 1 - (s - ncc))
        return jnp.where(d == 0, s, bwd)

    tok = pl.BlockSpec((1, A_CHUNK, A_HEADS * A_DK), lambda b, d, s: (b, chunk(d, s), 0))
    return pl.pallas_call(
        _gdn_kernel,
        out_shape=jax.ShapeDtypeStruct((bsz, 2, tt, A_HEADS * A_DV), F32),
        grid=(bsz, 2, nc),
        in_specs=[
            tok, tok, tok,
            pl.BlockSpec((1, 1, A_CHUNK, 2 * A_HEADS), lambda b, d, s: (b, d, chunk(d, s), 0)),
            pl.BlockSpec((1, 1, 1, 2 * A_HEADS, A_CHUNK), lambda b, d, s: (b, d, chunk(d, s), 0, 0)),
        ],
        out_specs=pl.BlockSpec((1, 1, A_CHUNK, A_HEADS * A_DV), lambda b, d, s: (b, d, chunk(d, s), 0)),
        scratch_shapes=[pltpu.VMEM((A_HEADS, A_DK, A_DV), F32)],
        compiler_params=_params(("parallel", "parallel", "arbitrary")),
        name="gdn_scan",
    )(q, k, v, gb, gt)


def _pick_chunk(n, options):
    for o in options:
        if n % o == 0:
            return o
    raise ValueError(f"no chunk size in {options} divides {n}")


def _mla_kernel(q_ref, k_ref, v_ref, o_ref, *, lc, tq, tk, nk):
    t = pl.program_id(2)

    def ctx_head(h):
        q = q_ref[0, h]
        s = _dot_nt(q, k_ref[0, h, :lc, :])
        p = jnp.exp(s - jnp.max(s, axis=-1, keepdims=True))
        l = jnp.sum(p, axis=-1, keepdims=True)
        return _dot(p.astype(BF16), v_ref[0, h, :lc, :]) / l

    def lat_head(h):
        q = q_ref[0, h]

        def body(j, carry):
            m, l, acc = carry
            off = pl.multiple_of(j * tk, tk)
            s = _dot_nt(q, k_ref[0, h, pl.ds(off, tk), :])
            m_new = jnp.maximum(m, jnp.max(s, axis=-1, keepdims=True))
            alpha = jnp.exp(m - m_new)
            p = jnp.exp(s - m_new)
            l = alpha * l + jnp.sum(p, axis=-1, keepdims=True)
            acc = alpha * acc + _dot(p.astype(BF16), v_ref[0, h, pl.ds(off, tk), :])
            return m_new, l, acc

        init = (jnp.full((tq, 1), -jnp.inf, F32), jnp.zeros((tq, 1), F32), jnp.zeros((tq, LANES), F32))
        _, l, acc = lax.fori_loop(0, nk, body, init)
        return acc / l

    @pl.when(t < lc // tq)
    def _():
        o_ref[0] = ctx_head(0) + ctx_head(1)

    @pl.when(t >= lc // tq)
    def _():
        o_ref[0] = lat_head(0) + lat_head(1)


def _mla_attention(q, k, v, lc):
    bsz, nh, tt, _ = q.shape
    tq = TM
    tk = _pick_chunk(tt, (768, 512, 256))
    kern = functools.partial(_mla_kernel, lc=lc, tq=tq, tk=tk, nk=tt // tk)
    kv = pl.BlockSpec((1, 2, tt, LANES), lambda b, hp, t: (b, hp, 0, 0))
    return pl.pallas_call(
        kern,
        out_shape=jax.ShapeDtypeStruct((bsz, tt, nh // 2 * LANES), F32),
        grid=(bsz, nh // 2, tt // tq),
        in_specs=[pl.BlockSpec((1, 2, tq, LANES), lambda b, hp, t: (b, hp, t, 0)), kv, kv],
        out_specs=pl.BlockSpec((1, tq, LANES), lambda b, hp, t: (b, t, hp)),
        compiler_params=_params(("parallel", "parallel", "parallel")),
        name="mla_attention",
    )(q, k, v)


def _mix0_kernel(of_ref, ob_ref, z_ref, yb_ref, x_ref, mod_ref, gg_ref, woa_ref, wob_ref, ng_ref,
                 wg_ref, wu_ref, wd_ref, o_ref):
    o = of_ref[0, 0] + ob_ref[0, 0]
    z = z_ref[0]
    gg = gg_ref[...]
    parts = []
    for h in range(A_HEADS):
        sl = slice(h * A_DV, (h + 1) * A_DV)
        parts.append((_rms(o[:, sl]) * gg * _silu(z[:, sl])).astype(BF16))
    ya = jnp.concatenate(parts, axis=1)
    y = _dot(ya, woa_ref[...]) + _dot(yb_ref[0].astype(BF16), wob_ref[...])
    m = mod_ref[0]
    x1 = x_ref[0] + m[2:3] * y
    f = (_rms(x1) * ng_ref[...] * (1.0 + m[4:5]) + m[3:4]).astype(BF16)
    hid = _silu(_dot(f, wg_ref[...])) * _dot(f, wu_ref[...])
    o_ref[0] = x1 + m[5:6] * _dot(hid.astype(BF16), wd_ref[...])


def _mix0(o2, z, yb, xa, mod, gdn_g, w_out, ng, wg, wu, wd, lc):
    bsz, tt, d = xa.shape
    nct = lc // TM
    na = A_HEADS * A_DV
    woa, wob = w_out[:na].astype(BF16), w_out[na:].astype(BF16)
    wg, wu, wd = wg.astype(BF16), wu.astype(BF16), wd.astype(BF16)
    tok = lambda w: pl.BlockSpec((1, TM, w), lambda b, t: (b, t, 0))
    return pl.pallas_call(
        _mix0_kernel,
        out_shape=jax.ShapeDtypeStruct((bsz, tt, d), F32),
        grid=(bsz, tt // TM),
        in_specs=[
            pl.BlockSpec((1, 1, TM, na), lambda b, t: (b, 0, t, 0)),
            pl.BlockSpec((1, 1, TM, na), lambda b, t: (b, 1, t, 0)),
            tok(na), tok(B_HEADS * B_DV), tok(d),
            pl.BlockSpec((1, 6, d), lambda b, t: (_mod_row(b, t, nct, bsz), 0, 0)),
            _const_spec((1, A_DV)), _const_spec(woa.shape), _const_spec(wob.shape), _const_spec((1, d)),
            _const_spec(wg.shape), _const_spec(wu.shape), _const_spec(wd.shape),
        ],
        out_specs=tok(d),
        compiler_params=_params(("parallel", "parallel")),
        name="mix0_ffn",
    )(o2, o2, z, yb, xa, mod, gdn_g.reshape(1, -1), woa, wob, ng.reshape(1, d), wg, wu, wd)


def _proj1_kernel(x_ref, mod_ref, ng_ref, w_ref, q_ref, k_ref, v_ref):
    m = mod_ref[0]
    h = (_rms(x_ref[0]) * ng_ref[...] * (1.0 + m[1:2]) + m[0:1]).astype(BF16)
    p = _dot(h, w_ref[...])
    d = q_ref.shape[-1]
    q_ref[0] = (p[:, :d] * (C_DH ** -0.5)).astype(BF16)
    k_ref[0] = p[:, d:2 * d].astype(BF16)
    v_ref[0] = p[:, 2 * d:].astype(BF16)


def _proj1(xa, mod, ng, w_qkv, lc):
    bsz, tt, d = xa.shape
    nct = lc // TM
    w = w_qkv.astype(BF16)
    tok = pl.BlockSpec((1, TM, d), lambda b, t: (b, t, 0))
    shp = jax.ShapeDtypeStruct((bsz, tt, d), BF16)
    return pl.pallas_call(
        _proj1_kernel,
        out_shape=(shp, shp, shp),
        grid=(bsz, tt // TM),
        in_specs=[tok, pl.BlockSpec((1, 6, d), lambda b, t: (_mod_row(b, t, nct, bsz), 0, 0)),
                  _const_spec((1, d)), _const_spec(w.shape)],
        out_specs=(tok, tok, tok),
        compiler_params=_params(("parallel", "parallel")),
        name="proj1",
    )(xa, mod, ng.reshape(1, d), w)


NAT_HG = 4
NAT_KEYS = WIN_R * GRID_W


def _nat_bias(rpb):
    v = np.arange(WIN_R)[:, None]
    rr = np.arange(WIN_R)[None, :]
    dr = rr - v + (WIN_R - 1)
    qc = np.arange(GRID_W)[:, None]
    kc = np.arange(GRID_W)[None, :]
    c0 = np.clip(qc - WIN_C // 2, 0, GRID_W - WIN_C)
    inside = (kc >= c0) & (kc < c0 + WIN_C)
    dc = np.clip(kc - qc + (WIN_C - 1), 0, 2 * WIN_C - 2)
    tab = rpb[:, dr[:, :, None, None], dc[None, None, :, :]]
    tab = jnp.where(jnp.asarray(inside)[None, None, None], tab, NEG)
    tab = jnp.transpose(tab, (1, 0, 3, 2, 4))
    return tab.reshape(WIN_R, rpb.shape[0], GRID_W, NAT_KEYS)


def _nat_kernel(q_ref, k_ref, v_ref, bias_ref, o_ref, *, lc, rows):
    r = pl.program_id(2)
    r0 = jnp.clip(r - WIN_R // 2, 0, rows - WIN_R)
    start = pl.multiple_of(lc + r0 * GRID_W, GRID_W)
    kw = k_ref[0, pl.ds(start, NAT_KEYS), :]
    vw = v_ref[0, pl.ds(start, NAT_KEYS), :]
    kc = k_ref[0, :lc, :]
    vc = v_ref[0, :lc, :]
    q = q_ref[0]
    lane = lax.broadcasted_iota(jnp.int32, q.shape, 1)
    acc = jnp.zeros(q.shape, F32)
    for h in range(NAT_HG):
        hm = (lane >= h * C_DH) & (lane < (h + 1) * C_DH)
        qm = jnp.where(hm, q, jnp.zeros_like(q))
        s1 = _dot_nt(qm, kw) + bias_ref[0, h]
        s2 = _dot_nt(qm, kc)
        m = jnp.maximum(jnp.max(s1, axis=-1, keepdims=True), jnp.max(s2, axis=-1, keepdims=True))
        p1 = jnp.exp(s1 - m)
        p2 = jnp.exp(s2 - m)
        l = jnp.sum(p1, axis=-1, keepdims=True) + jnp.sum(p2, axis=-1, keepdims=True)
        o = _dot(p1.astype(BF16), vw) + _dot(p2.astype(BF16), vc)
        acc = jnp.where(hm, o / l, acc)
    o_ref[0] = acc.astype(BF16)


def _nat(q, k, v, rpb, lc):
    bsz, tt, d = q.shape
    l = tt - lc
    rows = l // GRID_W
    assert rows >= WIN_R
    bias = _nat_bias(rpb)
    hw = NAT_HG * C_DH
    kern = functools.partial(_nat_kernel, lc=lc, rows=rows)

    def variant(r):
        return r - jnp.clip(r - WIN_R // 2, 0, rows - WIN_R)

    kv = pl.BlockSpec((1, tt, hw), lambda b, g, r: (b, 0, g))
    return pl.pallas_call(
        kern,
        out_shape=jax.ShapeDtypeStruct((bsz, l, d), BF16),
        grid=(bsz, d // hw, rows),
        in_specs=[
            pl.BlockSpec((1, GRID_W, hw), lambda b, g, r: (b, lc // GRID_W + r, g)),
            kv, kv,
            pl.BlockSpec((1, NAT_HG, GRID_W, NAT_KEYS), lambda b, g, r: (variant(r), g, 0, 0)),
        ],
        out_specs=pl.BlockSpec((1, GRID_W, hw), lambda b, g, r: (b, r, g)),
        compiler_params=_params(("parallel", "parallel", "arbitrary")),
        name="nat",
    )(q, k, v, bias)


def _mix1_kernel(o_ref, x_ref, mod_ref, wo_ref, ng_ref, wr_ref, x_out, f_out, w_out):
    m = mod_ref[0]
    x1 = x_ref[0] + m[2:3] * _dot(o_ref[0], wo_ref[...])
    x_out[0] = x1
    f = _rms(x1) * ng_ref[...] * (1.0 + m[4:5]) + m[3:4]
    f_out[0] = f.astype(BF16)
    logits = _dot(f, wr_ref[...], precision=HIGHEST)
    lane = lax.broadcasted_iota(jnp.int32, logits.shape, 1)
    logits = jnp.where(lane < N_EXPERTS, logits, -jnp.inf)
    m1 = jnp.max(logits, axis=-1, keepdims=True)
    i1 = jnp.min(jnp.where(logits == m1, lane, LANES), axis=-1, keepdims=True)
    rest = jnp.where(lane == i1, -jnp.inf, logits)
    m2 = jnp.max(rest, axis=-1, keepdims=True)
    i2 = jnp.min(jnp.where(rest == m2, lane, LANES), axis=-1, keepdims=True)
    e2 = jnp.exp(m2 - m1)
    g1 = 1.0 / (1.0 + e2)
    w_out[0] = jnp.where(lane == i1, g1, 0.0) + jnp.where(lane == i2, e2 * g1, 0.0)


def _mix1(o, xa, mod, w_out, ng, w_router, lc):
    bsz, tt, d = xa.shape
    l = tt - lc
    nct = lc // TM
    wr = jnp.concatenate([w_router, jnp.zeros((d, LANES - N_EXPERTS), F32)], axis=1)
    tok = lambda w: pl.BlockSpec((1, TM, w), lambda b, t: (b, t, 0))
    return pl.pallas_call(
        _mix1_kernel,
        out_shape=(jax.ShapeDtypeStruct((bsz, l, d), F32), jax.ShapeDtypeStruct((bsz, l, d), BF16),
                   jax.ShapeDtypeStruct((bsz, l, LANES), F32)),
        grid=(bsz, l // TM),
        in_specs=[tok(d), pl.BlockSpec((1, TM, d), lambda b, t: (b, t + nct, 0)),
                  pl.BlockSpec((1, 6, d), lambda b, t: (b, 0, 0)),
                  _const_spec((d, d)), _const_spec((1, d)), _const_spec(wr.shape)],
        out_specs=(tok(d), tok(d), tok(LANES)),
        compiler_params=_params(("parallel", "parallel")),
        name="mix1_router",
    )(o, xa, mod, w_out.astype(BF16), ng.reshape(1, d), wr)


def _moe_kernel(f_ref, w_ref, x_ref, mod_ref, wg_ref, wu_ref, wd_ref, fg_ref, o_ref, acc_ref):
    e = pl.program_id(2)
    c = pl.program_id(3)

    @pl.when((e == 0) & (c == 0))
    def _():
        acc_ref[...] = jnp.zeros_like(acc_ref)

    f = f_ref[0]
    hid = _silu(_dot(f, wg_ref[0].astype(BF16))) * _dot(f, wu_ref[0].astype(BF16))
    w = w_ref[0]
    lane = lax.broadcasted_iota(jnp.int32, w.shape, 1)
    we = jnp.sum(jnp.where(lane == e, w, 0.0), axis=-1, keepdims=True)
    acc_ref[...] += _dot((hid * we).astype(BF16), wd_ref[0].astype(BF16))

    @pl.when((e == pl.num_programs(2) - 1) & (c == pl.num_programs(3) - 1))
    def _():
        m = mod_ref[0]
        x2 = x_ref[0] + m[5:6] * acc_ref[...]
        o_ref[0] = _rms(x2) * fg_ref[...]


def _moe(f, w, x, mod, wg, wu, wd, fg):
    bsz, l, d = x.shape
    ne, _, dff = wg.shape
    tmx = _pick_chunk(l, (1024, 512, 256))
    tf = 512
    tok = lambda wd_: pl.BlockSpec((1, tmx, wd_), lambda b, t, e, c: (b, t, 0))
    return pl.pallas_call(
        _moe_kernel,
        out_shape=jax.ShapeDtypeStruct((bsz, l, d), F32),
        grid=(bsz, l // tmx, ne, dff // tf),
        in_specs=[
            tok(d), tok(LANES), tok(d),
            pl.BlockSpec((1, 6, d), lambda b, t, e, c: (b, 0, 0)),
            pl.BlockSpec((1, d, tf), lambda b, t, e, c: (e, 0, c)),
            pl.BlockSpec((1, d, tf), lambda b, t, e, c: (e, 0, c)),
            pl.BlockSpec((1, tf, d), lambda b, t, e, c: (e, c, 0)),
            pl.BlockSpec((1, d), lambda b, t, e, c: (0, 0)),
        ],
        out_specs=tok(d),
        scratch_shapes=[pltpu.VMEM((tmx, d), F32)],
        compiler_params=_params(("parallel", "parallel", "arbitrary", "arbitrary")),
        name="moe_ffn",
    )(f, w, x, mod, wg, wu, wd, fg.reshape(1, d))


def kernel(x, c, ctx, c_ctx, ada_w, ada_b, norm_g, ev_w_in, ev_conv_w, ev_a_log, ev_dt_bias, ev_gdn_norm_g, ev_q_norm_g, ev_w_uq, ev_kv_norm_g, ev_w_ukv, ev_w_out, ev_ffn_wg, ev_ffn_wu, ev_ffn_wd, od_w_qkv, od_rpb, od_w_out, od_router, od_exp_wg, od_exp_wu, od_exp_wd, final_norm_g):
    bsz, l, d = x.shape
    lc = ctx.shape[1]
    assert ada_w.shape[0] == 2 and bsz < MOD_ROWS and lc % TM == 0 and l % TM == 0
    xa = jnp.concatenate([ctx, x], axis=1)

    cs = jnp.concatenate([c, c_ctx[None], jnp.zeros((MOD_ROWS - bsz - 1, d), F32)], axis=0)
    mods = _ada_mod(cs, ada_w, ada_b).reshape(2, MOD_ROWS, 6, d)

    win, wq, wqr, wk, wv = _proj0_weights(ev_w_in[0], ev_w_uq[0], ev_w_ukv[0])
    qkv_pre, z, ab, q, k, v = _proj0(xa, mods[0], norm_g[0, 0],
                                     (win, ev_q_norm_g[0], wq, wqr, ev_kv_norm_g[0], wk, wv),
                                     _rope_tables(lc, l), lc)
    gq, gk, gv, gb, gt = _gdn_prepare(qkv_pre, ab, ev_conv_w[0], ev_a_log[0], ev_dt_bias[0], lc)
    o2 = _gdn_scan(gq, gk, gv, gb, gt, lc)
    yb = _mla_attention(q, k, v, lc)
    xa = _mix0(o2, z, yb, xa, mods[0], ev_gdn_norm_g[0], ev_w_out[0], norm_g[0, 1],
               ev_ffn_wg[0], ev_ffn_wu[0], ev_ffn_wd[0], lc)

    q1, k1, v1 = _proj1(xa, mods[1], norm_g[1, 0], od_w_qkv[0], lc)
    o1 = _nat(q1, k1, v1, od_rpb[0], lc)
    x1, f1, gates = _mix1(o1, xa, mods[1], od_w_out[0], norm_g[1, 1], od_router[0], lc)
    return _moe(f1, gates, x1, mods[1], od_exp_wg[0], od_exp_wu[0], od_exp_wd[0], final_norm_g)
```

```python
import functools

import numpy as np
import jax
import jax.numpy as jnp
from jax import lax
from jax.experimental import pallas as pl
from jax.experimental.pallas import tpu as pltpu

F32 = jnp.float32
BF16 = jnp.bfloat16
HIGHEST = lax.Precision.HIGHEST

GRID_W = 64
NORM_EPS = 1e-6
A_HEADS, A_DK, A_DV = 4, 128, 128
GDN_CHUNK_LOG2 = 7
GDN_CHUNK = 1 << GDN_CHUNK_LOG2
GDN_BASE_LOG2 = 3
B_HEADS, B_Q_RANK, B_KV_RANK, B_NOPE, B_ROPE, B_DV = 8, 384, 256, 64, 32, 64
ROPE_THETA = 10000.0
C_HEADS, C_DH = 16, 64
WIN_R, WIN_C = 8, 16
N_EXPERTS = 8
LANES = 128
MOD_ROWS = 8
NEG = -1e30
LOG2E = 1.4426950408889634
VMEM_LIMIT = 56 * 1024 * 1024

TM = 256


def _params(sem):
    return pltpu.CompilerParams(dimension_semantics=sem, vmem_limit_bytes=VMEM_LIMIT)


def _const_spec(shape):
    nd = len(shape)
    return pl.BlockSpec(shape, lambda *_: (0,) * nd, pipeline_mode=pl.Buffered(1))


def _rms(x):
    return x * lax.rsqrt(jnp.mean(x * x, axis=-1, keepdims=True) + NORM_EPS)


def _silu(x):
    return x * jax.nn.sigmoid(x)


def _dot(a, b, **kw):
    return jnp.dot(a, b, preferred_element_type=F32, **kw)


def _dot_nt(a, b, **kw):
    return lax.dot_general(a, b, (((1,), (1,)), ((), ())), preferred_element_type=F32, **kw)


def _ada_kernel(s_ref, w_ref, b_ref, o_ref):
    s = _silu(s_ref[...])
    o_ref[0] = _dot(s, w_ref[0], precision=HIGHEST) + b_ref[0]


def _ada_mod(cs, ada_w, ada_b):
    depth, d, d6 = ada_w.shape
    tn = d6 // 4
    return pl.pallas_call(
        _ada_kernel,
        out_shape=jax.ShapeDtypeStruct((depth, MOD_ROWS, d6), F32),
        grid=(depth, d6 // tn),
        in_specs=[
            pl.BlockSpec((MOD_ROWS, d), lambda i, j: (0, 0)),
            pl.BlockSpec((1, d, tn), lambda i, j: (i, 0, j)),
            pl.BlockSpec((1, 1, tn), lambda i, j: (i, 0, j)),
        ],
        out_specs=pl.BlockSpec((1, MOD_ROWS, tn), lambda i, j: (i, 0, j)),
        compiler_params=_params(("parallel", "parallel")),
        name="ada_mod",
    )(cs, ada_w, ada_b.reshape(depth, 1, d6))


N_QKVZ = 4 * A_HEADS * A_DK
IN_OFF_AB = N_QKVZ
IN_OFF_CQ = IN_OFF_AB + LANES
IN_OFF_CKV = IN_OFF_CQ + B_Q_RANK
IN_OFF_KR = IN_OFF_CKV + B_KV_RANK
IN_COLS = IN_OFF_KR + LANES
HW = B_HEADS * LANES


def _proj0_kernel(x_ref, mod_ref, ng_ref, win_ref, qg_ref, wq_ref, wqr_ref, kvg_ref, wk_ref, wv_ref,
                  e_ref, er_ref, vone_ref, cq_ref, sq_ref, ck_ref, sk_ref,
                  qkv_ref, z_ref, ab_ref, q_ref, k_ref, vt_ref):
    x = x_ref[0]
    m = mod_ref[0]
    h = _rms(x) * ng_ref[...] * (1.0 + m[1:2]) + m[0:1]
    p = _dot(h.astype(BF16), win_ref[...])
    qkv_ref[0] = p[:, :3 * A_HEADS * A_DK]
    z_ref[0] = p[:, 3 * A_HEADS * A_DK:N_QKVZ]
    ab_ref[0] = p[:, IN_OFF_AB:IN_OFF_CQ]
    nq = (_rms(p[:, IN_OFF_CQ:IN_OFF_CKV]) * qg_ref[...]).astype(BF16)
    nkv = (_rms(p[:, IN_OFF_CKV:IN_OFF_KR]) * kvg_ref[...]).astype(BF16)
    krp = p[:, IN_OFF_KR:IN_COLS]
    qa = _dot(nq, wq_ref[...])
    qb = _dot(nq, wqr_ref[...])
    kn = _dot(nkv, wk_ref[...])
    vv = _dot(nkv, wv_ref[...]) + vone_ref[...]
    ka = _dot(krp, e_ref[...], precision=HIGHEST)
    kb = _dot(krp, er_ref[...], precision=HIGHEST)
    cq, sq, ck, sk = cq_ref[...], sq_ref[...], ck_ref[...], sk_ref[...]
    for hh in range(B_HEADS):
        sl = slice(hh * LANES, (hh + 1) * LANES)
        q_ref[0, hh] = (qa[:, sl] * cq + qb[:, sl] * sq).astype(BF16)
        k_ref[0, hh] = (kn[:, sl] + ka[:, sl] * ck + kb[:, sl] * sk).astype(BF16)
        vt_ref[0, hh] = vv[:, sl].T.astype(BF16)


def _rot_cols(w):
    q = B_ROPE // 4
    return jnp.concatenate([-w[:, q:2 * q], w[:, :q], -w[:, 3 * q:], w[:, 2 * q:3 * q]], axis=1)


def _proj0_weights(w_in, w_uq, w_ukv):
    d = w_in.shape[0]
    offs = np.cumsum([0, 512, 512, 512, 512, 8, 8, B_Q_RANK, B_KV_RANK, B_ROPE])
    zeros = lambda n: jnp.zeros((d, n), F32)
    kr = w_in[:, offs[8]:offs[9]]
    win = jnp.concatenate([
        w_in[:, :offs[4]],
        w_in[:, offs[4]:offs[6]], zeros(LANES - 16),
        w_in[:, offs[6]:offs[7]],
        w_in[:, offs[7]:offs[8]],
        kr, zeros(LANES - B_ROPE),
    ], axis=1).astype(BF16)
    dq = B_NOPE + B_ROPE
    wq3 = w_uq.reshape(B_Q_RANK, B_HEADS, dq)
    zq = jnp.zeros((B_Q_RANK, B_HEADS, LANES - dq), F32)
    wq = jnp.concatenate([wq3, zq], axis=2).reshape(B_Q_RANK, HW).astype(BF16)
    rot = jnp.stack([_rot_cols(wq3[:, hh, B_NOPE:]) for hh in range(B_HEADS)], axis=1)
    wqr = jnp.concatenate([jnp.zeros((B_Q_RANK, B_HEADS, B_NOPE), F32), rot, zq], axis=2)
    wqr = wqr.reshape(B_Q_RANK, HW).astype(BF16)
    wkv3 = w_ukv.reshape(B_KV_RANK, B_HEADS, B_NOPE + B_DV)
    zk = jnp.zeros((B_KV_RANK, B_HEADS, LANES - B_NOPE), F32)
    wk = jnp.concatenate([wkv3[:, :, :B_NOPE], zk], axis=2).reshape(B_KV_RANK, HW).astype(BF16)
    vpart = wkv3[:, :, B_NOPE:]
    zv = jnp.zeros_like(vpart)
    even = (jnp.arange(B_HEADS) % 2 == 0)[None, :, None]
    wv = jnp.concatenate([jnp.where(even, vpart, zv), jnp.where(even, zv, vpart)], axis=2)
    wv = wv.reshape(B_KV_RANK, HW).astype(BF16)
    return win, wq, wqr, wk, wv


def _rope_select():
    e = np.zeros((LANES, HW), np.float32)
    er = np.zeros((LANES, HW), np.float32)
    q = B_ROPE // 4
    rmat = np.zeros((B_ROPE, B_ROPE), np.float32)
    for i in range(q):
        rmat[q + i, i] = -1.0
        rmat[i, q + i] = 1.0
        rmat[3 * q + i, 2 * q + i] = -1.0
        rmat[2 * q + i, 3 * q + i] = 1.0
    for hh in range(B_HEADS):
        base = hh * LANES + B_NOPE
        e[:B_ROPE, base:base + B_ROPE] = np.eye(B_ROPE, dtype=np.float32)
        er[:B_ROPE, base:base + B_ROPE] = rmat
    vone = np.zeros((1, HW), np.float32)
    for hh in range(B_HEADS):
        lo = hh * LANES + (B_DV if hh % 2 == 0 else 0)
        vone[0, lo:lo + B_DV] = 1.0
    return jnp.asarray(e), jnp.asarray(er), jnp.asarray(vone)


def _rope_tables(lc, l):
    t = jnp.arange(l)
    row = (t // GRID_W).astype(F32)
    col = (t % GRID_W).astype(F32)
    half = B_ROPE // 2
    inv = ROPE_THETA ** (-jnp.arange(0, half, 2, dtype=F32) / half)
    ar = row[:, None] * inv[None, :]
    ac = col[:, None] * inv[None, :]
    ang = jnp.concatenate([ar, ar, ac, ac], axis=-1)
    cos = jnp.concatenate([jnp.ones((lc, B_ROPE), F32), jnp.cos(ang)], axis=0)
    sin = jnp.concatenate([jnp.zeros((lc, B_ROPE), F32), jnp.sin(ang)], axis=0)
    tt = lc + l
    scale = (B_NOPE + B_ROPE) ** -0.5 * LOG2E
    pad =jnp.zeros((tt, LANES - B_NOPE - B_ROPE), F32)
    z64 = jnp.zeros((tt, B_NOPE), F32)
    cq = jnp.concatenate([jnp.full((tt, B_NOPE), scale, F32), scale * cos, pad], axis=1)
    sq = jnp.concatenate([z64, scale * sin, pad], axis=1)
    ck = jnp.concatenate([z64, cos, pad], axis=1)
    sk = jnp.concatenate([z64, sin, pad], axis=1)
    return cq, sq, ck, sk


def _mod_row(b, t, n_ctx_tiles, bsz):
    return jnp.where(t < n_ctx_tiles, bsz, b)


def _proj0(xa, mod, ng, weights, tables, lc):
    bsz, tt, d = xa.shape
    win, qg, wq, wqr, kvg, wk, wv = weights
    e, er, vone = _rope_select()
    nct = lc // TM
    tok = lambda w: pl.BlockSpec((1, TM, w), lambda b, t: (b, t, 0))
    head = pl.BlockSpec((1, B_HEADS, TM, LANES), lambda b, t: (b, 0, t, 0))
    head_t = pl.BlockSpec((1, B_HEADS, LANES, TM), lambda b, t: (b, 0, 0, t))
    tab = pl.BlockSpec((TM, LANES), lambda b, t: (t, 0))
    hshape = jax.ShapeDtypeStruct((bsz, B_HEADS, tt, LANES), BF16)
    return pl.pallas_call(
        _proj0_kernel,
        out_shape=(
            jax.ShapeDtypeStruct((bsz, tt, 3 * A_HEADS * A_DK), F32),
            jax.ShapeDtypeStruct((bsz, tt, A_HEADS * A_DV), F32),
            jax.ShapeDtypeStruct((bsz, tt, LANES), F32),
            hshape, hshape, jax.ShapeDtypeStruct((bsz, B_HEADS, LANES, tt), BF16),
        ),
        grid=(bsz, tt // TM),
        in_specs=[
            tok(d),
            pl.BlockSpec((1, 6, d), lambda b, t: (_mod_row(b, t, nct, bsz), 0, 0)),
            _const_spec((1, d)), _const_spec(win.shape),
            _const_spec((1, B_Q_RANK)), _const_spec(wq.shape), _const_spec(wqr.shape),
            _const_spec((1, B_KV_RANK)), _const_spec(wk.shape), _const_spec(wv.shape),
            _const_spec(e.shape), _const_spec(er.shape), _const_spec(vone.shape),
            tab, tab, tab, tab,
        ],
        out_specs=(tok(3 * A_HEADS * A_DK), tok(A_HEADS * A_DV), tok(LANES), head, head, head_t),
        compiler_params=_params(("parallel", "parallel")),
        name="proj0",
    )(xa, mod, ng.reshape(1, d), win, qg.reshape(1, -1), wq, wqr, kvg.reshape(1, -1), wk, wv, e, er, vone,
      *tables)


def _gdn_prepare(qkv_pre, ab, conv_w, a_log, dt_bias, lc):
    bsz, tt, _ = qkv_pre.shape
    kk = conv_w.shape[0]

    def conv(x):
        return lax.conv_general_dilated(x, conv_w[:, None, :], window_strides=(1,),
                                        padding=[((kk - 1) // 2, (kk - 1) // 2)],
                                        dimension_numbers=('NWC', 'WIO', 'NWC'),
                                        feature_group_count=x.shape[-1])

    y = jnp.concatenate([conv(qkv_pre[:, :lc]), conv(qkv_pre[:, lc:])], axis=1)
    y = _silu(y)
    nq = A_HEADS * A_DK

    def l2(t):
        t = t.reshape(bsz, tt, A_HEADS, A_DK)
        t = t * lax.rsqrt(jnp.sum(t * t, axis=-1, keepdims=True) + 1e-6)
        return t.reshape(bsz, tt, nq)

    q = l2(y[..., :nq]) * (A_DK ** -0.5)
    k = l2(y[..., nq:2 * nq])
    v = y[..., 2 * nq:]
    a = ab[..., :2 * A_HEADS].reshape(bsz, tt, 2, A_HEADS)
    bg = ab[..., 2 * A_HEADS:4 * A_HEADS].reshape(bsz, tt, 2, A_HEADS)
    g = -jnp.exp(a_log) * jax.nn.softplus(a + dt_bias)
    beta = jax.nn.sigmoid(bg)
    gb = jnp.moveaxis(jnp.concatenate([g, beta], axis=-1), 2, 1)
    nc = tt // GDN_CHUNK
    gt = jnp.swapaxes(gb.reshape(bsz, 2, nc, GDN_CHUNK, 2 * A_HEADS), 3, 4)
    return q, k, v, gb, gt


def _split_bf16(a):
    hi = a.astype(BF16)
    return hi, (a - hi.astype(F32)).astype(BF16)


def _dot3(a, b):
    ah, al = _split_bf16(a)
    bh, bl = _split_bf16(b)
    return _dot(ah, bh) + (_dot(ah, bl) + _dot(al, bh))


def _bdot3(a, b):
    bd = lambda u, v: lax.dot_general(u, v, (((2,), (1,)), ((0,), (0,))), preferred_element_type=F32)
    ah, al = _split_bf16(a)
    bh, bl = _split_bf16(b)
    return bd(ah, bh) + (bd(ah, bl) + bd(al, bh))


def _gdn_pre_kernel(q_ref, k_ref, v_ref, gb_ref, gt_ref, u_ref, wq_ref, kgt_ref, qk_ref, gl_ref):
    c = GDN_CHUNK
    row = lax.broadcasted_iota(jnp.int32, (c, c), 0)
    col = lax.broadcasted_iota(jnp.int32, (c, c), 1)
    eye = (row == col).astype(F32)
    ns, rhs, qgs = [], [], []
    for d in range(2):
        incl = (row >= col) if d == 0 else (row <= col)
        strict = (row > col) if d == 0 else (row < col)
        m_incl = incl.astype(F32)
        gb = gb_ref[0, d]
        gc_col = _dot(m_incl, gb, precision=HIGHEST)
        gc_row = _dot_nt(gt_ref[0, d, 0], m_incl, precision=HIGHEST)
        g_tot = jnp.sum(gb, axis=0, keepdims=True)
        gl_rows = []
        for h in range(A_HEADS):
            sl = slice(h * A_DK, (h + 1) * A_DK)
            qh, kh, vh = q_ref[0, :, sl], k_ref[0, :, sl], v_ref[0, :, sl]
            khb = kh.astype(BF16)
            beta = gb[:, A_HEADS + h:A_HEADS + h + 1]
            gcc = gc_col[:, h:h + 1]
            gcr = gc_row[h:h + 1, :]
            gtot = g_tot[:, h:h + 1]
            decay = jnp.where(incl, jnp.exp(jnp.where(incl, gcc - gcr, 0.0)), 0.0)
            n = -jnp.where(strict, beta * _dot_nt(khb, khb) * decay, 0.0)
            eg = jnp.exp(gcc)
            ns.append(n)
            rhs.append(jnp.concatenate([vh * beta, kh * (beta * eg)], axis=1))
            qgs.append(qh * eg)
            kgt_ref[0, d, 0, h] = (kh * jnp.exp(gtot - gcc)).T.astype(BF16)
            qk_ref[0, d, 0, h] = jnp.where(incl, _dot_nt(qh.astype(BF16), khb) * decay, 0.0).astype(BF16)
            gl_rows.append(jnp.broadcast_to(jnp.exp(gtot), (1, LANES)))
        gl_ref[0, d, 0] = jnp.concatenate(gl_rows + gl_rows, axis=0)
    n = jnp.stack(ns, axis=0)
    same = lambda s: lax.shift_right_logical(row, s) == lax.shift_right_logical(col, s)
    nd = jnp.where(same(GDN_BASE_LOG2), n, 0.0)
    x = eye + nd
    p = _bdot3(nd, nd)
    z = _bdot3(p, jnp.concatenate([p, x], axis=2))
    x = x + z[:, :, c:]
    x = x + _bdot3(z[:, :, :c], x)
    for s in range(GDN_BASE_LOG2, GDN_CHUNK_LOG2):
        nl = jnp.where(same(s + 1) & jnp.logical_not(same(s)), n, 0.0)
        x = x + _bdot3(_bdot3(x, nl), x)
    y = _bdot3(x, jnp.stack(rhs, axis=0))
    for d in range(2):
        for h in range(A_HEADS):
            i = d * A_HEADS + h
            u_ref[0, d, 0, h] = y[i, :, :A_DV]
            wq_ref[0, d, 0, h] = jnp.concatenate([y[i, :, A_DV:], qgs[i]], axis=0).astype(BF16)


def _gdn_scan_kernel(*refs, bsz):
    ins, (of_ref, ob_ref, s_ref) = refs[:10], refs[10:]

    @pl.when(pl.program_id(0) == 0)
    def _():
        s_ref[...] = jnp.zeros_like(s_ref)

    c = GDN_CHUNK
    for d, o_ref in ((0, of_ref), (1, ob_ref)):
        u_ref, wq_ref, kgt_ref, qk_ref, gl_ref = ins[5 * d:5 * d + 5]
        for b in range(bsz):
            for h in range(A_HEADS):
                s = s_ref[b, d, h]
                ws = _dot(wq_ref[b, 0, 0, h], s.astype(BF16))
                v_new = (u_ref[b, 0, 0, h] - ws[:c]).astype(BF16)
                o_ref[b, :, h * A_DV:(h + 1) * A_DV] = ws[c:] + _dot(qk_ref[b, 0, 0, h], v_new)
                s_ref[b, d, h] = s * gl_ref[b, 0, 0, h:h + 1, :] + _dot(kgt_ref[b, 0, 0, h], v_new)


def _gdn(q, k, v, gb, gt, lc):
    bsz, tt, _ = q.shape
    c = GDN_CHUNK
    nc, ncc = tt // c, lc // c
    tok = pl.BlockSpec((1, c, A_HEADS * A_DK), lambda b, i: (b, i, 0))
    per = lambda r, w: pl.BlockSpec((1, 2, 1, A_HEADS, r, w), lambda b, i: (b, 0, i, 0, 0, 0))
    shp = lambda r, w, dt: jax.ShapeDtypeStruct((bsz, 2, nc, A_HEADS, r, w), dt)
    pre = pl.pallas_call(
        _gdn_pre_kernel,
        out_shape=(shp(c, A_DV, F32), shp(2 * c, A_DK, BF16), shp(A_DK, c, BF16), shp(c, c, BF16),
                   jax.ShapeDtypeStruct((bsz, 2, nc, 2 * A_HEADS, LANES), F32)),
        grid=(bsz, nc),
        in_specs=[tok, tok, tok,
                  pl.BlockSpec((1, 2, c, 2 * A_HEADS), lambda b, i: (b, 0, i, 0)),
                  pl.BlockSpec((1, 2, 1, 2 * A_HEADS, c), lambda b, i: (b, 0, i, 0, 0))],
        out_specs=(per(c, A_DV), per(2 * c, A_DK), per(A_DK, c), per(c, c),
                   pl.BlockSpec((1, 2, 1, 2 * A_HEADS, LANES), lambda b, i: (b, 0, i, 0, 0))),
        compiler_params=_params(("parallel", "parallel")),
        name="gdn_pre",
    )(q, k, v, gb, gt)

    def chunk(d, s):
        return s if d == 0 else jnp.where(s < ncc, ncc - 1 - s, nc - 1 - (s - ncc))

    in_specs = []
    for d in range(2):
        for arr in pre[:4]:
            r, w = arr.shape[-2:]
            in_specs.append(pl.BlockSpec((bsz, 1, 1, A_HEADS, r, w),
                                         lambda s, d=d: (0, d, chunk(d, s), 0, 0, 0)))
        in_specs.append(pl.BlockSpec((bsz, 1, 1, 2 * A_HEADS, LANES), lambda s, d=d: (0, d, chunk(d, s), 0, 0)))
    o_shape = jax.ShapeDtypeStruct((bsz, tt, A_HEADS * A_DV), F32)
    return pl.pallas_call(
        functools.partial(_gdn_scan_kernel, bsz=bsz),
        out_shape=(o_shape, o_shape),
        grid=(nc,),
        in_specs=in_specs,
        out_specs=tuple(pl.BlockSpec((bsz, c, A_HEADS * A_DV), lambda s, d=d: (0, chunk(d, s), 0))
                        for d in range(2)),
        scratch_shapes=[pltpu.VMEM((bsz, 2, A_HEADS, A_DK, A_DV), F32)],
        compiler_params=_params(("arbitrary",)),
        name="gdn_scan",
    )(*pre, *pre)


def _pick_chunk(n, options):
    for o in options:
        if n % o == 0:
            return o
    raise ValueError(f"no chunk size in {options} divides {n}")


def _mla_kernel(q_ref, k_ref, vt_ref, o_ref, *, lc, tq, tk, nk):
    t = pl.program_id(2)
    half = LANES // 2
    g = q_ref.shape[1]
    bnt = lambda a, b: lax.dot_general(a, b, (((2,), (2,)), ((0,), (0,))), preferred_element_type=F32)
    bnn = lambda a, b: lax.dot_general(a, b, (((2,), (1,)), ((0,), (0,))), preferred_element_type=F32)

    def update(q, m, acc, k_chunk, vt_chunk):
        s = bnt(k_chunk, q)
        m_new = jnp.maximum(m, jnp.max(s, axis=1, keepdims=True))
        p = jnp.exp2(s - m_new).astype(BF16)
        return m_new, jnp.exp2(m - m_new) * acc + bnn(vt_chunk, p)

    def finish(acc):
        parts = [acc[h, :half] / acc[h, half:] if h % 2 == 0 else acc[h, half:] / acc[h, :half]
                 for h in range(g)]
        o_ref[0] = jnp.concatenate(parts, axis=0).T

    init = (jnp.full((g, 1, tq), -jnp.inf, F32), jnp.zeros((g, LANES, tq), F32))

    @pl.when(t < lc // tq)
    def _():
        finish(update(q_ref[0], *init, k_ref[0, :, :lc, :], vt_ref[0, :, :, :lc])[1])

    @pl.when(t >= lc // tq)
    def _():
        q = q_ref[0]

        def body(j, carry):
            off = pl.multiple_of(j * tk, tk)
            return update(q, *carry, k_ref[0, :, pl.ds(off, tk), :], vt_ref[0, :, :, pl.ds(off, tk)])

        finish(lax.fori_loop(0, nk, body, init)[1])


MLA_G = 8


def _mla_attention(q, k, vt, lc):
    bsz, nh, tt, _ = q.shape
    tq = TM
    tk = _pick_chunk(tt, (768, 512, 256))
    g = MLA_G
    kern = functools.partial(_mla_kernel, lc=lc, tq=tq, tk=tk, nk=tt // tk)
    return pl.pallas_call(
        kern,
        out_shape=jax.ShapeDtypeStruct((bsz, tt, nh // 2 * LANES), F32),
        grid=(bsz, nh // g, tt // tq),
        in_specs=[pl.BlockSpec((1, g, tq, LANES), lambda b, hp, t: (b, hp, t, 0)),
                  pl.BlockSpec((1, g, tt, LANES), lambda b, hp, t: (b, hp, 0, 0), pipeline_mode=pl.Buffered(1)),
                  pl.BlockSpec((1, g, LANES, tt), lambda b, hp, t: (b, hp, 0, 0), pipeline_mode=pl.Buffered(1))],
        out_specs=pl.BlockSpec((1, tq, g // 2 * LANES), lambda b, hp, t: (b, t, hp)),
        compiler_params=_params(("parallel", "parallel", "parallel")),
        name="mla_attention",
    )(q, k, vt)


def _mix0_kernel(of_ref, ob_ref, z_ref, yb_ref, x_ref, mod_ref, gg_ref, woa_ref, wob_ref, ng_ref,
                 wg_ref, wu_ref, wd_ref, o_ref):
    o = of_ref[0] + ob_ref[0]
    z = z_ref[0]
    gg = gg_ref[...]
    parts = []
    for h in range(A_HEADS):
        sl = slice(h * A_DV, (h + 1) * A_DV)
        parts.append((_rms(o[:, sl]) * gg * _silu(z[:, sl])).astype(BF16))
    ya = jnp.concatenate(parts, axis=1)
    y = _dot(ya, woa_ref[...]) + _dot(yb_ref[0].astype(BF16), wob_ref[...])
    m = mod_ref[0]
    x1 = x_ref[0] + m[2:3] * y
    f = (_rms(x1) * ng_ref[...] * (1.0 + m[4:5]) + m[3:4]).astype(BF16)
    hid = _silu(_dot(f, wg_ref[...])) * _dot(f, wu_ref[...])
    o_ref[0] = x1 + m[5:6] * _dot(hid.astype(BF16), wd_ref[...])


def _mix0(o_f, o_b, z, yb, xa, mod, gdn_g, w_out, ng, wg, wu, wd, lc):
    bsz, tt, d = xa.shape
    nct = lc // TM
    na = A_HEADS * A_DV
    woa, wob = w_out[:na].astype(BF16), w_out[na:].astype(BF16)
    wg, wu, wd = wg.astype(BF16), wu.astype(BF16), wd.astype(BF16)
    tok = lambda w: pl.BlockSpec((1, TM, w), lambda b, t: (b, t, 0))
    return pl.pallas_call(
        _mix0_kernel,
        out_shape=jax.ShapeDtypeStruct((bsz, tt, d), F32),
        grid=(bsz, tt // TM),
        in_specs=[
            tok(na), tok(na), tok(na), tok(B_HEADS * B_DV), tok(d),
            pl.BlockSpec((1, 6, d), lambda b, t: (_mod_row(b, t, nct, bsz), 0, 0)),
            _const_spec((1, A_DV)), _const_spec(woa.shape), _const_spec(wob.shape), _const_spec((1, d)),
            _const_spec(wg.shape), _const_spec(wu.shape), _const_spec(wd.shape),
        ],
        out_specs=tok(d),
        compiler_params=_params(("parallel", "parallel")),
        name="mix0_ffn",
    )(o_f, o_b, z, yb, xa, mod, gdn_g.reshape(1, -1), woa, wob, ng.reshape(1, d), wg, wu, wd)


def _proj1_kernel(x_ref, mod_ref, ng_ref, w_ref, q_ref, k_ref, v_ref):
    m = mod_ref[0]
    h = (_rms(x_ref[0]) * ng_ref[...] * (1.0 + m[1:2]) + m[0:1]).astype(BF16)
    p = _dot(h, w_ref[...])
    d = q_ref.shape[-1]
    q_ref[0] = (p[:, :d] * (C_DH ** -0.5)).astype(BF16)
    k_ref[0] = p[:, d:2 * d].astype(BF16)
    v_ref[0] = p[:, 2 * d:].astype(BF16)


def _proj1(xa, mod, ng, w_qkv, lc):
    bsz, tt, d = xa.shape
    nct = lc // TM
    w = w_qkv.astype(BF16)
    tok = pl.BlockSpec((1, TM, d), lambda b, t: (b, t, 0))
    shp = jax.ShapeDtypeStruct((bsz, tt, d), BF16)
    return pl.pallas_call(
        _proj1_kernel,
        out_shape=(shp, shp, shp),
        grid=(bsz, tt // TM),
        in_specs=[tok, pl.BlockSpec((1, 6, d), lambda b, t: (_mod_row(b, t, nct, bsz), 0, 0)),
                  _const_spec((1, d)), _const_spec(w.shape)],
        out_specs=(tok, tok, tok),
        compiler_params=_params(("parallel", "parallel")),
        name="proj1",
    )(xa, mod, ng.reshape(1, d), w)


NAT_HG = 4
NAT_KEYS = WIN_R * GRID_W


def _nat_bias(rpb):
    qc = np.arange(GRID_W)[:, None]
    kc = np.arange(GRID_W)[None, :]
    c0 = np.clip(qc - WIN_C // 2, 0, GRID_W - WIN_C)
    inside = (kc >= c0) & (kc < c0 + WIN_C)
    dc = kc - qc + (WIN_C - 1)
    onehot = ((np.arange(2 * WIN_C - 1)[:, None, None] == dc[None]) & inside[None]).astype(np.float32)
    tab = jnp.einsum('hrd,dqk->hqrk', rpb, jnp.asarray(onehot), precision=HIGHEST)
    tab = tab + jnp.asarray(np.where(inside, 0.0, NEG).astype(np.float32))[None, :, None, :]
    tab = jnp.stack([tab[:, :, WIN_R - 1 - v:2 * WIN_R - 1 - v, :] for v in range(WIN_R)], axis=0)
    return tab.reshape(WIN_R, rpb.shape[0], GRID_W, NAT_KEYS)


def _nat_kernel(q_ref, k_ref, v_ref, bias_ref, o_ref, *, lc, rows):
    r = pl.program_id(2)
    r0 = jnp.clip(r - WIN_R // 2, 0, rows - WIN_R)
    start = pl.multiple_of(lc + r0 * GRID_W, GRID_W)
    kw = k_ref[0, pl.ds(start, NAT_KEYS), :]
    vw = v_ref[0, pl.ds(start, NAT_KEYS), :]
    kc = k_ref[0, :lc, :]
    vc = v_ref[0, :lc, :]
    q = q_ref[0]
    lane = lax.broadcasted_iota(jnp.int32, q.shape, 1)
    acc = jnp.zeros(q.shape, F32)
    for h in range(NAT_HG):
        hm = (lane >= h * C_DH) & (lane < (h + 1) * C_DH)
        qm = jnp.where(hm, q, jnp.zeros_like(q))
        s1 = _dot_nt(qm, kw) + bias_ref[0, h]
        s2 = _dot_nt(qm, kc)
        m = jnp.maximum(jnp.max(s1, axis=-1, keepdims=True), jnp.max(s2, axis=-1, keepdims=True))
        p1 = jnp.exp(s1 - m)
        p2 = jnp.exp(s2 - m)
        l = jnp.sum(p1, axis=-1, keepdims=True) + jnp.sum(p2, axis=-1, keepdims=True)
        o = _dot(p1.astype(BF16), vw) + _dot(p2.astype(BF16), vc)
        acc = jnp.where(hm, o / l, acc)
    o_ref[0] = acc.astype(BF16)


def _nat(q, k, v, rpb, lc):
    bsz, tt, d = q.shape
    l = tt - lc
    rows = l // GRID_W
    assert rows >= WIN_R
    bias = _nat_bias(rpb)
    hw = NAT_HG * C_DH
    kern = functools.partial(_nat_kernel, lc=lc, rows=rows)

    def variant(r):
        return r - jnp.clip(r - WIN_R // 2, 0, rows - WIN_R)

    kv = pl.BlockSpec((1, tt, hw), lambda b, g, r: (b, 0, g))
    return pl.pallas_call(
        kern,
        out_shape=jax.ShapeDtypeStruct((bsz, l, d), BF16),
        grid=(bsz, d // hw, rows),
        in_specs=[
            pl.BlockSpec((1, GRID_W, hw), lambda b, g, r: (b, lc // GRID_W + r, g)),
            kv, kv,
            pl.BlockSpec((1, NAT_HG, GRID_W, NAT_KEYS), lambda b, g, r: (variant(r), g, 0, 0)),
        ],
        out_specs=pl.BlockSpec((1, GRID_W, hw), lambda b, g, r: (b, r, g)),
        compiler_params=_params(("parallel", "parallel", "arbitrary")),
        name="nat",
    )(q, k, v, bias)


def _mix1_kernel(o_ref, x_ref, mod_ref, wo_ref, ng_ref, wr_ref, x_out, f_out, w_out):
    m = mod_ref[0]
    x1 = x_ref[0] + m[2:3] * _dot(o_ref[0], wo_ref[...])
    x_out[0] = x1
    f = _rms(x1) * ng_ref[...] * (1.0 + m[4:5]) + m[3:4]
    f_out[0] = f.astype(BF16)
    logits = _dot(f, wr_ref[...], precision=HIGHEST)
    lane = lax.broadcasted_iota(jnp.int32, logits.shape, 1)
    logits = jnp.where(lane < N_EXPERTS, logits, -jnp.inf)
    m1 = jnp.max(logits, axis=-1, keepdims=True)
    i1 = jnp.min(jnp.where(logits == m1, lane, LANES), axis=-1, keepdims=True)
    rest = jnp.where(lane == i1, -jnp.inf, logits)
    m2 = jnp.max(rest, axis=-1, keepdims=True)
    i2 = jnp.min(jnp.where(rest == m2, lane, LANES), axis=-1, keepdims=True)
    e2 = jnp.exp(m2 - m1)
    g1 = 1.0 / (1.0 + e2)
    w_out[0] = jnp.where(lane == i1, g1, 0.0) + jnp.where(lane == i2, e2 * g1, 0.0)


def _mix1(o, xa, mod, w_out, ng, w_router, lc):
    bsz, tt, d = xa.shape
    l = tt - lc
    nct = lc // TM
    wr = jnp.concatenate([w_router, jnp.zeros((d, LANES - N_EXPERTS), F32)], axis=1)
    tok = lambda w: pl.BlockSpec((1, TM, w), lambda b, t: (b, t, 0))
    return pl.pallas_call(
        _mix1_kernel,
        out_shape=(jax.ShapeDtypeStruct((bsz, l, d), F32), jax.ShapeDtypeStruct((bsz, l, d), BF16),
                   jax.ShapeDtypeStruct((bsz, l, LANES), F32)),
        grid=(bsz, l // TM),
        in_specs=[tok(d), pl.BlockSpec((1, TM, d), lambda b, t: (b, t + nct, 0)),
                  pl.BlockSpec((1, 6, d), lambda b, t: (b, 0, 0)),
                  _const_spec((d, d)), _const_spec((1, d)), _const_spec(wr.shape)],
        out_specs=(tok(d), tok(d), tok(LANES)),
        compiler_params=_params(("parallel", "parallel")),
        name="mix1_router",
    )(o, xa, mod, w_out.astype(BF16), ng.reshape(1, d), wr)


def _moe_kernel(f_ref, w_ref, x_ref, mod_ref, wg_ref, wu_ref, wd_ref, fg_ref, o_ref, acc_ref):
    e = pl.program_id(2)
    c = pl.program_id(3)

    @pl.when((e == 0) & (c == 0))
    def _():
        acc_ref[...] = jnp.zeros_like(acc_ref)

    f = f_ref[0]
    hid = _silu(_dot(f, wg_ref[0].astype(BF16))) * _dot(f, wu_ref[0].astype(BF16))
    w = w_ref[0]
    lane = lax.broadcasted_iota(jnp.int32, w.shape, 1)
    we = jnp.sum(jnp.where(lane == e, w, 0.0), axis=-1, keepdims=True)
    acc_ref[...] += _dot((hid * we).astype(BF16), wd_ref[0].astype(BF16))

    @pl.when((e == pl.num_programs(2) - 1) & (c == pl.num_programs(3) - 1))
    def _():
        m = mod_ref[0]
        x2 = x_ref[0] + m[5:6] * acc_ref[...]
        o_ref[0] = _rms(x2) * fg_ref[...]


def _moe(f, w, x, mod, wg, wu, wd, fg):
    bsz, l, d = x.shape
    ne, _, dff = wg.shape
    tmx = _pick_chunk(l, (1024, 512, 256))
    tf = 512
    tok = lambda wd_: pl.BlockSpec((1, tmx, wd_), lambda b, t, e, c: (b, t, 0))
    return pl.pallas_call(
        _moe_kernel,
        out_shape=jax.ShapeDtypeStruct((bsz, l, d), F32),
        grid=(bsz, l // tmx, ne, dff // tf),
        in_specs=[
            tok(d), tok(LANES), tok(d),
            pl.BlockSpec((1, 6, d), lambda b, t, e, c: (b, 0, 0)),
            pl.BlockSpec((1, d, tf), lambda b, t, e, c: (e, 0, c)),
            pl.BlockSpec((1, d, tf), lambda b, t, e, c: (e, 0, c)),
            pl.BlockSpec((1, tf, d), lambda b, t, e, c: (e, c, 0)),
            pl.BlockSpec((1, d), lambda b, t, e, c: (0, 0)),
        ],
        out_specs=tok(d),
        scratch_shapes=[pltpu.VMEM((tmx, d), F32)],
        compiler_params=_params(("parallel", "parallel", "arbitrary", "arbitrary")),
        name="moe_ffn",
    )(f, w, x, mod, wg, wu, wd, fg.reshape(1, d))


def kernel(x, c, ctx, c_ctx, ada_w, ada_b, norm_g, ev_w_in, ev_conv_w, ev_a_log, ev_dt_bias, ev_gdn_norm_g, ev_q_norm_g, ev_w_uq, ev_kv_norm_g, ev_w_ukv, ev_w_out, ev_ffn_wg, ev_ffn_wu, ev_ffn_wd, od_w_qkv, od_rpb, od_w_out, od_router, od_exp_wg, od_exp_wu, od_exp_wd, final_norm_g):
    bsz, l, d = x.shape
    lc = ctx.shape[1]
    assert ada_w.shape[0] == 2 and bsz < MOD_ROWS and lc % TM == 0 and l % TM == 0
    xa = jnp.concatenate([ctx, x], axis=1)

    cs = jnp.concatenate([c, c_ctx[None], jnp.zeros((MOD_ROWS - bsz - 1, d), F32)], axis=0)
    mods = _ada_mod(cs, ada_w, ada_b).reshape(2, MOD_ROWS, 6, d)

    win, wq, wqr, wk, wv = _proj0_weights(ev_w_in[0], ev_w_uq[0], ev_w_ukv[0])
    qkv_pre, z, ab, q, k, v = _proj0(xa, mods[0], norm_g[0, 0],
                                     (win, ev_q_norm_g[0], wq, wqr, ev_kv_norm_g[0], wk, wv),
                                     _rope_tables(lc, l), lc)
    gq, gk, gv, gb, gt = _gdn_prepare(qkv_pre, ab, ev_conv_w[0], ev_a_log[0], ev_dt_bias[0], lc)
    o_f, o_b = _gdn(gq, gk, gv, gb, gt, lc)
    yb = _mla_attention(q, k, v, lc)
    xa = _mix0(o_f, o_b, z, yb, xa, mods[0], ev_gdn_norm_g[0], ev_w_out[0], norm_g[0, 1],
               ev_ffn_wg[0], ev_ffn_wu[0], ev_ffn_wd[0], lc)

    q1, k1, v1 = _proj1(xa, mods[1], norm_g[1, 0], od_w_qkv[0], lc)
    o1 = _nat(q1, k1, v1, od_rpb[0], lc)
    x1, f1, gates = _mix1(o1, xa, mods[1], od_w_out[0], norm_g[1, 1], od_router[0], lc)
    return _moe(f1, gates, x1, mods[1], od_exp_wg[0], od_exp_wu[0], od_exp_wd[0], final_norm_g)
```

```python
import functools

import numpy as np
import jax
import jax.numpy as jnp
from jax import lax
from jax.experimental import pallas as pl
from jax.experimental.pallas import tpu as pltpu

F32 = jnp.float32
BF16 = jnp.bfloat16
HIGHEST = lax.Precision.HIGHEST

GRID_W = 64
NORM_EPS = 1e-6
A_HEADS, A_DK, A_DV = 4, 128, 128
GDN_CHUNK_LOG2 = 7
GDN_CHUNK = 1 << GDN_CHUNK_LOG2
GDN_BASE_LOG2 = 3
B_HEADS, B_Q_RANK, B_KV_RANK, B_NOPE, B_ROPE, B_DV = 8, 384, 256, 64, 32, 64
ROPE_THETA = 10000.0
C_HEADS, C_DH = 16, 64
WIN_R, WIN_C = 8, 16
N_EXPERTS = 8
LANES = 128
MOD_ROWS = 8
NEG = -1e30
LOG2E = 1.4426950408889634
VMEM_LIMIT = 56 * 1024 * 1024

TM = 256


def _params(sem):
    return pltpu.CompilerParams(dimension_semantics=sem, vmem_limit_bytes=VMEM_LIMIT)


def _const_spec(shape):
    nd = len(shape)
    return pl.BlockSpec(shape, lambda *_: (0,) * nd, pipeline_mode=pl.Buffered(1))


def _rms(x):
    return x * lax.rsqrt(jnp.mean(x * x, axis=-1, keepdims=True) + NORM_EPS)


def _silu(x):
    return x * jax.nn.sigmoid(x)


def _dot(a, b, **kw):
    return jnp.dot(a, b, preferred_element_type=F32, **kw)


def _dot_nt(a, b, **kw):
    return lax.dot_general(a, b, (((1,), (1,)), ((), ())), preferred_element_type=F32, **kw)


def _ada_kernel(s_ref, w_ref, b_ref, o_ref):
    s = _silu(s_ref[...])
    o_ref[0] = _dot(s, w_ref[0], precision=HIGHEST) + b_ref[0]


def _ada_mod(cs, ada_w, ada_b):
    depth, d, d6 = ada_w.shape
    tn = d6 // 4
    return pl.pallas_call(
        _ada_kernel,
        out_shape=jax.ShapeDtypeStruct((depth, MOD_ROWS, d6), F32),
        grid=(depth, d6 // tn),
        in_specs=[
            pl.BlockSpec((MOD_ROWS, d), lambda i, j: (0, 0)),
            pl.BlockSpec((1, d, tn), lambda i, j: (i, 0, j)),
            pl.BlockSpec((1, 1, tn), lambda i, j: (i, 0, j)),
        ],
        out_specs=pl.BlockSpec((1, MOD_ROWS, tn), lambda i, j: (i, 0, j)),
        compiler_params=_params(("parallel", "parallel")),
        name="ada_mod",
    )(cs, ada_w, ada_b.reshape(depth, 1, d6))


N_QKVZ = 4 * A_HEADS * A_DK
IN_OFF_AB = N_QKVZ
IN_OFF_CQ = IN_OFF_AB + LANES
IN_OFF_CKV = IN_OFF_CQ + B_Q_RANK
IN_OFF_KR = IN_OFF_CKV + B_KV_RANK
IN_COLS = IN_OFF_KR + LANES
HW = B_HEADS * LANES


def _proj0_kernel(x_ref, mod_ref, ng_ref, win_ref, qg_ref, wq_ref, wqr_ref, kvg_ref, wk_ref, wv_ref,
                  e_ref, er_ref, vone_ref, cq_ref, sq_ref, ck_ref, sk_ref,
                  qkv_ref, z_ref, ab_ref, q_ref, k_ref, vt_ref):
    x = x_ref[0]
    m = mod_ref[0]
    h = _rms(x) * ng_ref[...] * (1.0 + m[1:2]) + m[0:1]
    p = _dot(h.astype(BF16), win_ref[...])
    qkv_ref[0] = p[:, :3 * A_HEADS * A_DK]
    z_ref[0] = p[:, 3 * A_HEADS * A_DK:N_QKVZ]
    ab_ref[0] = p[:, IN_OFF_AB:IN_OFF_CQ]
    nq = (_rms(p[:, IN_OFF_CQ:IN_OFF_CKV]) * qg_ref[...]).astype(BF16)
    nkv = (_rms(p[:, IN_OFF_CKV:IN_OFF_KR]) * kvg_ref[...]).astype(BF16)
    krp = p[:, IN_OFF_KR:IN_COLS]
    qa = _dot(nq, wq_ref[...])
    qb = _dot(nq, wqr_ref[...])
    kn = _dot(nkv, wk_ref[...])
    vv = _dot(nkv, wv_ref[...]) + vone_ref[...]
    ka = _dot(krp, e_ref[...], precision=HIGHEST)
    kb = _dot(krp, er_ref[...], precision=HIGHEST)
    cq, sq, ck, sk = cq_ref[...], sq_ref[...], ck_ref[...], sk_ref[...]
    for hh in range(B_HEADS):
        sl = slice(hh * LANES, (hh + 1) * LANES)
        q_ref[0, hh] = (qa[:, sl] * cq + qb[:, sl] * sq).astype(BF16)
        k_ref[0, hh] = (kn[:, sl] + ka[:, sl] * ck + kb[:, sl] * sk).astype(BF16)
        vt_ref[0, hh] = vv[:, sl].T.astype(BF16)


def _rot_cols(w):
    q = B_ROPE // 4
    return jnp.concatenate([-w[:, q:2 * q], w[:, :q], -w[:, 3 * q:], w[:, 2 * q:3 * q]], axis=1)


def _proj0_weights(w_in, w_uq, w_ukv):
    d = w_in.shape[0]
    offs = np.cumsum([0, 512, 512, 512, 512, 8, 8, B_Q_RANK, B_KV_RANK, B_ROPE])
    zeros = lambda n: jnp.zeros((d, n), F32)
    kr = w_in[:, offs[8]:offs[9]]
    win = jnp.concatenate([
        w_in[:, :offs[4]],
        w_in[:, offs[4]:offs[6]], zeros(LANES - 16),
        w_in[:, offs[6]:offs[7]],
        w_in[:, offs[7]:offs[8]],
        kr, zeros(LANES - B_ROPE),
    ], axis=1).astype(BF16)
    dq = B_NOPE + B_ROPE
    wq3 = w_uq.reshape(B_Q_RANK, B_HEADS, dq)
    zq = jnp.zeros((B_Q_RANK, B_HEADS, LANES - dq), F32)
    wq = jnp.concatenate([wq3, zq], axis=2).reshape(B_Q_RANK, HW).astype(BF16)
    rot = jnp.stack([_rot_cols(wq3[:, hh, B_NOPE:]) for hh in range(B_HEADS)], axis=1)
    wqr = jnp.concatenate([jnp.zeros((B_Q_RANK, B_HEADS, B_NOPE), F32), rot, zq], axis=2)
    wqr = wqr.reshape(B_Q_RANK, HW).astype(BF16)
    wkv3 = w_ukv.reshape(B_KV_RANK, B_HEADS, B_NOPE + B_DV)
    zk = jnp.zeros((B_KV_RANK, B_HEADS, LANES - B_NOPE), F32)
    wk = jnp.concatenate([wkv3[:, :, :B_NOPE], zk], axis=2).reshape(B_KV_RANK, HW).astype(BF16)
    vpart = wkv3[:, :, B_NOPE:]
    zv = jnp.zeros_like(vpart)
    even = (jnp.arange(B_HEADS) % 2 == 0)[None, :, None]
    wv = jnp.concatenate([jnp.where(even, vpart, zv), jnp.where(even, zv, vpart)], axis=2)
    wv = wv.reshape(B_KV_RANK, HW).astype(BF16)
    return win, wq, wqr, wk, wv


def _rope_select():
    e = np.zeros((LANES, HW), np.float32)
    er = np.zeros((LANES, HW), np.float32)
    q = B_ROPE // 4
    rmat = np.zeros((B_ROPE, B_ROPE), np.float32)
    for i in range(q):
        rmat[q + i, i] = -1.0
        rmat[i, q + i] = 1.0
        rmat[3 * q + i, 2 * q + i] = -1.0
        rmat[2 * q + i, 3 * q + i] = 1.0
    for hh in range(B_HEADS):
        base = hh * LANES + B_NOPE
        e[:B_ROPE, base:base + B_ROPE] = np.eye(B_ROPE, dtype=np.float32)
        er[:B_ROPE, base:base + B_ROPE] = rmat
    vone = np.zeros((1, HW), np.float32)
    for hh in range(B_HEADS):
        lo = hh * LANES + (B_DV if hh % 2 == 0 else 0)
        vone[0, lo:lo + B_DV] = 1.0
    return jnp.asarray(e), jnp.asarray(er), jnp.asarray(vone)


def _rope_tables(lc, l):
    t = jnp.arange(l)
    row = (t // GRID_W).astype(F32)
    col = (t % GRID_W).astype(F32)
    half = B_ROPE // 2
    inv = ROPE_THETA ** (-jnp.arange(0, half, 2, dtype=F32) / half)
    ar = row[:, None] * inv[None, :]
    ac = col[:, None] * inv[None, :]
    ang = jnp.concatenate([ar, ar, ac, ac], axis=-1)
    cos = jnp.concatenate([jnp.ones((lc, B_ROPE), F32), jnp.cos(ang)], axis=0)
    sin = jnp.concatenate([jnp.zeros((lc, B_ROPE), F32), jnp.sin(ang)], axis=0)
    tt = lc + l
    scale = (B_NOPE + B_ROPE) ** -0.5 * LOG2E
    pad =jnp.zeros((tt, LANES - B_NOPE - B_ROPE), F32)
    z64 = jnp.zeros((tt, B_NOPE), F32)
    cq = jnp.concatenate([jnp.full((tt, B_NOPE), scale, F32), scale * cos, pad], axis=1)
    sq = jnp.concatenate([z64, scale * sin, pad], axis=1)
    ck = jnp.concatenate([z64, cos, pad], axis=1)
    sk = jnp.concatenate([z64, sin, pad], axis=1)
    return cq, sq, ck, sk


def _mod_row(b, t, n_ctx_tiles, bsz):
    return jnp.where(t < n_ctx_tiles, bsz, b)


def _proj0(xa, mod, ng, weights, tables, lc):
    bsz, tt, d = xa.shape
    win, qg, wq, wqr, kvg, wk, wv = weights
    e, er, vone = _rope_select()
    nct = lc // TM
    tok = lambda w: pl.BlockSpec((1, TM, w), lambda b, t: (b, t, 0))
    head = pl.BlockSpec((1, B_HEADS, TM, LANES), lambda b, t: (b, 0, t, 0))
    head_t = pl.BlockSpec((1, B_HEADS, LANES, TM), lambda b, t: (b, 0, 0, t))
    tab = pl.BlockSpec((TM, LANES), lambda b, t: (t, 0))
    hshape = jax.ShapeDtypeStruct((bsz, B_HEADS, tt, LANES), BF16)
    return pl.pallas_call(
        _proj0_kernel,
        out_shape=(
            jax.ShapeDtypeStruct((bsz, tt, 3 * A_HEADS * A_DK), F32),
            jax.ShapeDtypeStruct((bsz, tt, A_HEADS * A_DV), F32),
            jax.ShapeDtypeStruct((bsz, tt, LANES), F32),
            hshape, hshape, jax.ShapeDtypeStruct((bsz, B_HEADS, LANES, tt), BF16),
        ),
        grid=(bsz, tt // TM),
        in_specs=[
            tok(d),
            pl.BlockSpec((1, 6, d), lambda b, t: (_mod_row(b, t, nct, bsz), 0, 0)),
            _const_spec((1, d)), _const_spec(win.shape),
            _const_spec((1, B_Q_RANK)), _const_spec(wq.shape), _const_spec(wqr.shape),
            _const_spec((1, B_KV_RANK)), _const_spec(wk.shape), _const_spec(wv.shape),
            _const_spec(e.shape), _const_spec(er.shape), _const_spec(vone.shape),
            tab, tab, tab, tab,
        ],
        out_specs=(tok(3 * A_HEADS * A_DK), tok(A_HEADS * A_DV), tok(LANES), head, head, head_t),
        compiler_params=_params(("parallel", "parallel")),
        name="proj0",
    )(xa, mod, ng.reshape(1, d), win, qg.reshape(1, -1), wq, wqr, kvg.reshape(1, -1), wk, wv, e, er, vone,
      *tables)


A_CONV = 5
CONV_HALO = 8


def _split_bf16(a):
    hi = a.astype(BF16)
    return hi, (a - hi.astype(F32)).astype(BF16)


def _dot3(a, b):
    ah, al = _split_bf16(a)
    bh, bl = _split_bf16(b)
    return _dot(ah, bh) + (_dot(ah, bl) + _dot(al, bh))


def _bdot3(a, b):
    bd = lambda u, v: lax.dot_general(u, v, (((2,), (1,)), ((0,), (0,))), preferred_element_type=F32)
    ah, al = _split_bf16(a)
    bh, bl = _split_bf16(b)
    return bd(ah, bh) + (bd(ah, bl) + bd(al, bh))


def _gdn_pre_kernel(x_ref, prev_ref, next_ref, ab_ref, cw_ref, alog_ref, dtb_ref,
                    u_ref, wq_ref, kgt_ref, qk_ref, gl_ref, *, ncc, nc):
    c = GDN_CHUNK
    i = pl.program_id(1)
    halo = CONV_HALO
    prev_ok = ((i != 0) & (i != ncc)).astype(F32)
    next_ok = ((i != ncc - 1) & (i != nc - 1)).astype(F32)
    xe = jnp.concatenate([prev_ref[0] * prev_ok, x_ref[0], next_ref[0] * next_ok], axis=0)
    cw = cw_ref[...]
    taps = A_CONV
    y = sum(cw[j:j + 1] * xe[halo - taps // 2 + j:halo - taps // 2 + j + c] for j in range(taps))
    y = _silu(y)
    nq = A_HEADS * A_DK

    def l2n(t):
        return t * lax.rsqrt(jnp.sum(t * t, axis=-1, keepdims=True) + 1e-6)

    ab = ab_ref[0]
    g_all = -jnp.exp(alog_ref[...]) * (jnp.maximum(ab + dtb_ref[...], 0.0)
                                       + jnp.log1p(jnp.exp(-jnp.abs(ab + dtb_ref[...]))))
    beta_all = jax.nn.sigmoid(ab)
    g_all_t = g_all.T
    g_tot = jnp.sum(g_all, axis=0, keepdims=True)
    row = lax.broadcasted_iota(jnp.int32, (c, c), 0)
    col = lax.broadcasted_iota(jnp.int32, (c, c), 1)
    eye = (row == col).astype(F32)
    ns, rhs, qgs = [], [], []
    heads = []
    for h in range(A_HEADS):
        sl = slice(h * A_DK, (h + 1) * A_DK)
        kh = l2n(y[:, nq + h * A_DK:nq + (h + 1) * A_DK])
        heads.append((l2n(y[:, sl]) * (A_DK ** -0.5), kh, y[:, 2 * nq + h * A_DV:2 * nq + (h + 1) * A_DV],
                      kh.astype(BF16)))
    for d in range(2):
        incl = (row >= col) if d == 0 else (row <= col)
        strict = (row > col) if d == 0 else (row < col)
        m_incl = incl.astype(F32)
        gc_col = _dot(m_incl, g_all, precision=HIGHEST)
        gc_row = _dot_nt(g_all_t, m_incl, precision=HIGHEST)
        gl_rows = []
        for h in range(A_HEADS):
            qh, kh, vh, khb = heads[h]
            ln = d * A_HEADS + h
            beta = beta_all[:, 2 * A_HEADS + ln:2 * A_HEADS + ln + 1]
            gcc = gc_col[:, ln:ln + 1]
            gcr = gc_row[ln:ln + 1, :]
            gtot = g_tot[:, ln:ln + 1]
            decay = jnp.where(incl, jnp.exp(jnp.where(incl, gcc - gcr, 0.0)), 0.0)
            n = -jnp.where(strict, beta * _dot_nt(khb, khb) * decay, 0.0)
            eg = jnp.exp(gcc)
            ns.append(n)
            rhs.append(jnp.concatenate([vh * beta, kh * (beta * eg)], axis=1))
            qgs.append(qh * eg)
            kgt_ref[0, d, 0, h] = (kh * jnp.exp(gtot - gcc)).T.astype(BF16)
            qk_ref[0, d, 0, h] = jnp.where(incl, _dot_nt(qh.astype(BF16), khb) * decay, 0.0).astype(BF16)
            gl_rows.append(jnp.broadcast_to(jnp.exp(gtot), (1, LANES)))
        gl_ref[0, d, 0] = jnp.concatenate(gl_rows + gl_rows, axis=0)
    n = jnp.stack(ns, axis=0)
    same = lambda s: lax.shift_right_logical(row, s) == lax.shift_right_logical(col, s)
    nd = jnp.where(same(GDN_BASE_LOG2), n, 0.0)
    x = eye + nd
    p = _bdot3(nd, nd)
    z = _bdot3(p, jnp.concatenate([p, x], axis=2))
    x = x + z[:, :, c:]
    x = x + _bdot3(z[:, :, :c], x)
    for s in range(GDN_BASE_LOG2, GDN_CHUNK_LOG2):
        nl = jnp.where(same(s + 1) & jnp.logical_not(same(s)), n, 0.0)
        x = x + _bdot3(_bdot3(x, nl), x)
    y = _bdot3(x, jnp.stack(rhs, axis=0))
    for d in range(2):
        for h in range(A_HEADS):
            i = d * A_HEADS + h
            u_ref[0, d, 0, h] = y[i, :, :A_DV]
            wq_ref[0, d, 0, h] = jnp.concatenate([y[i, :, A_DV:], qgs[i]], axis=0).astype(BF16)


def _gdn_scan_kernel(*refs, bsz):
    ins, (of_ref, ob_ref, s_ref) = refs[:10], refs[10:]

    @pl.when(pl.program_id(0) == 0)
    def _():
        s_ref[...] = jnp.zeros_like(s_ref)

    c = GDN_CHUNK
    for d, o_ref in ((0, of_ref), (1, ob_ref)):
        u_ref, wq_ref, kgt_ref, qk_ref, gl_ref = ins[5 * d:5 * d + 5]
        for b in range(bsz):
            for h in range(A_HEADS):
                s = s_ref[b, d, h]
                ws = _dot(wq_ref[b, 0, 0, h], s.astype(BF16))
                v_new = (u_ref[b, 0, 0, h] - ws[:c]).astype(BF16)
                o_ref[b, :, h * A_DV:(h + 1) * A_DV] = ws[c:] + _dot(qk_ref[b, 0, 0, h], v_new)
                s_ref[b, d, h] = s * gl_ref[b, 0, 0, h:h + 1, :] + _dot(kgt_ref[b, 0, 0, h], v_new)


def _gdn(qkv_pre, ab, conv_w, a_log, dt_bias, lc):
    bsz, tt, wqkv = qkv_pre.shape
    c = GDN_CHUNK
    nc, ncc = tt // c, lc // c
    hb = c // CONV_HALO
    per = lambda r, w: pl.BlockSpec((1, 2, 1, A_HEADS, r, w), lambda b, i: (b, 0, i, 0, 0, 0))
    shp = lambda r, w, dt: jax.ShapeDtypeStruct((bsz, 2, nc, A_HEADS, r, w), dt)
    lane_row = lambda t: jnp.concatenate([t.reshape(1, -1), jnp.zeros((1, LANES - t.size), F32)], axis=1)
    cw = jnp.concatenate([conv_w, jnp.zeros((CONV_HALO - A_CONV, wqkv), F32)], axis=0)
    pre = pl.pallas_call(
        functools.partial(_gdn_pre_kernel, ncc=ncc, nc=nc),
        out_shape=(shp(c, A_DV, F32), shp(2 * c, A_DK, BF16), shp(A_DK, c, BF16), shp(c, c, BF16),
                   jax.ShapeDtypeStruct((bsz, 2, nc, 2 * A_HEADS, LANES), F32)),
        grid=(bsz, nc),
        in_specs=[pl.BlockSpec((1, c, wqkv), lambda b, i: (b, i, 0)),
                  pl.BlockSpec((1, CONV_HALO, wqkv), lambda b, i: (b, jnp.maximum(i * hb - 1, 0), 0)),
                  pl.BlockSpec((1, CONV_HALO, wqkv), lambda b, i: (b, jnp.minimum((i + 1) * hb, nc * hb - 1), 0)),
                  pl.BlockSpec((1, c, LANES), lambda b, i: (b, i, 0)),
                  _const_spec(cw.shape), _const_spec((1, LANES)), _const_spec((1, LANES))],
        out_specs=(per(c, A_DV), per(2 * c, A_DK), per(A_DK, c), per(c, c),
                   pl.BlockSpec((1, 2, 1, 2 * A_HEADS, LANES), lambda b, i: (b, 0, i, 0, 0))),
        compiler_params=_params(("parallel", "parallel")),
        name="gdn_pre",
    )(qkv_pre, qkv_pre, qkv_pre, ab, cw, lane_row(a_log), lane_row(dt_bias))

    def chunk(d, s):
        return s if d == 0 else jnp.where(s < ncc, ncc - 1 - s, nc - 1 - (s - ncc))

    in_specs = []
    for d in range(2):
        for arr in pre[:4]:
            r, w = arr.shape[-2:]
            in_specs.append(pl.BlockSpec((bsz, 1, 1, A_HEADS, r, w),
                                         lambda s, d=d: (0, d, chunk(d, s), 0, 0, 0)))
        in_specs.append(pl.BlockSpec((bsz, 1, 1, 2 * A_HEADS, LANES), lambda s, d=d: (0, d, chunk(d, s), 0, 0)))
    o_shape = jax.ShapeDtypeStruct((bsz, tt, A_HEADS * A_DV), F32)
    return pl.pallas_call(
        functools.partial(_gdn_scan_kernel, bsz=bsz),
        out_shape=(o_shape, o_shape),
        grid=(nc,),
        in_specs=in_specs,
        out_specs=tuple(pl.BlockSpec((bsz, c, A_HEADS * A_DV), lambda s, d=d: (0, chunk(d, s), 0))
                        for d in range(2)),
        scratch_shapes=[pltpu.VMEM((bsz, 2, A_HEADS, A_DK, A_DV), F32)],
        compiler_params=_params(("arbitrary",)),
        name="gdn_scan",
    )(*pre, *pre)


def _pick_chunk(n, options):
    for o in options:
        if n % o == 0:
            return o
    raise ValueError(f"no chunk size in {options} divides {n}")


def _mla_kernel(q_ref, k_ref, vt_ref, o_ref, *, lc, tq, tk, nk):
    t = pl.program_id(2)
    half = LANES // 2
    g = q_ref.shape[1]
    bnt = lambda a, b: lax.dot_general(a, b, (((2,), (2,)), ((0,), (0,))), preferred_element_type=F32)
    bnn = lambda a, b: lax.dot_general(a, b, (((2,), (1,)), ((0,), (0,))), preferred_element_type=F32)

    def update(q, m, acc, k_chunk, vt_chunk):
        s = bnt(k_chunk, q)
        m_new = jnp.maximum(m, jnp.max(s, axis=1, keepdims=True))
        p = jnp.exp2(s - m_new).astype(BF16)
        return m_new, jnp.exp2(m - m_new) * acc + bnn(vt_chunk, p)

    def finish(acc):
        parts = [acc[h, :half] / acc[h, half:] if h % 2 == 0 else acc[h, half:] / acc[h, :half]
                 for h in range(g)]
        o_ref[0] = jnp.concatenate(parts, axis=0).T

    init = (jnp.full((g, 1, tq), -jnp.inf, F32), jnp.zeros((g, LANES, tq), F32))

    @pl.when(t < lc // tq)
    def _():
        finish(update(q_ref[0], *init, k_ref[0, :, :lc, :], vt_ref[0, :, :, :lc])[1])

    @pl.when(t >= lc // tq)
    def _():
        q = q_ref[0]

        def body(j, carry):
            off = pl.multiple_of(j * tk, tk)
            return update(q, *carry, k_ref[0, :, pl.ds(off, tk), :], vt_ref[0, :, :, pl.ds(off, tk)])

        finish(lax.fori_loop(0, nk, body, init)[1])


MLA_G = 8


def _mla_attention(q, k, vt, lc):
    bsz, nh, tt, _ = q.shape
    tq = TM
    tk = _pick_chunk(tt, (768, 512, 256))
    g = MLA_G
    kern = functools.partial(_mla_kernel, lc=lc, tq=tq, tk=tk, nk=tt // tk)
    return pl.pallas_call(
        kern,
        out_shape=jax.ShapeDtypeStruct((bsz, tt, nh // 2 * LANES), F32),
        grid=(bsz, nh // g, tt // tq),
        in_specs=[pl.BlockSpec((1, g, tq, LANES), lambda b, hp, t: (b, hp, t, 0)),
                  pl.BlockSpec((1, g, tt, LANES), lambda b, hp, t: (b, hp, 0, 0), pipeline_mode=pl.Buffered(1)),
                  pl.BlockSpec((1, g, LANES, tt), lambda b, hp, t: (b, hp, 0, 0), pipeline_mode=pl.Buffered(1))],
        out_specs=pl.BlockSpec((1, tq, g // 2 * LANES), lambda b, hp, t: (b, t, hp)),
        compiler_params=_params(("parallel", "parallel", "parallel")),
        name="mla_attention",
    )(q, k, vt)


def _mix0_kernel(of_ref, ob_ref, z_ref, yb_ref, x_ref, mod_ref, gg_ref, woa_ref, wob_ref, ng_ref,
                 wg_ref, wu_ref, wd_ref, o_ref):
    o = of_ref[0] + ob_ref[0]
    z = z_ref[0]
    gg = gg_ref[...]
    parts = []
    for h in range(A_HEADS):
        sl = slice(h * A_DV, (h + 1) * A_DV)
        parts.append((_rms(o[:, sl]) * gg * _silu(z[:, sl])).astype(BF16))
    ya = jnp.concatenate(parts, axis=1)
    y = _dot(ya, woa_ref[...]) + _dot(yb_ref[0].astype(BF16), wob_ref[...])
    m = mod_ref[0]
    x1 = x_ref[0] + m[2:3] * y
    f = (_rms(x1) * ng_ref[...] * (1.0 + m[4:5]) + m[3:4]).astype(BF16)
    hid = _silu(_dot(f, wg_ref[...])) * _dot(f, wu_ref[...])
    o_ref[0] = x1 + m[5:6] * _dot(hid.astype(BF16), wd_ref[...])


def _mix0(o_f, o_b, z, yb, xa, mod, gdn_g, w_out, ng, wg, wu, wd, lc):
    bsz, tt, d = xa.shape
    nct = lc // TM
    na = A_HEADS * A_DV
    woa, wob = w_out[:na].astype(BF16), w_out[na:].astype(BF16)
    wg, wu, wd = wg.astype(BF16), wu.astype(BF16), wd.astype(BF16)
    tok = lambda w: pl.BlockSpec((1, TM, w), lambda b, t: (b, t, 0))
    return pl.pallas_call(
        _mix0_kernel,
        out_shape=jax.ShapeDtypeStruct((bsz, tt, d), F32),
        grid=(bsz, tt // TM),
        in_specs=[
            tok(na), tok(na), tok(na), tok(B_HEADS * B_DV), tok(d),
            pl.BlockSpec((1, 6, d), lambda b, t: (_mod_row(b, t, nct, bsz), 0, 0)),
            _const_spec((1, A_DV)), _const_spec(woa.shape), _const_spec(wob.shape), _const_spec((1, d)),
            _const_spec(wg.shape), _const_spec(wu.shape), _const_spec(wd.shape),
        ],
        out_specs=tok(d),
        compiler_params=_params(("parallel", "parallel")),
        name="mix0_ffn",
    )(o_f, o_b, z, yb, xa, mod, gdn_g.reshape(1, -1), woa, wob, ng.reshape(1, d), wg, wu, wd)


def _proj1_kernel(x_ref, mod_ref, ng_ref, w_ref, q_ref, k_ref, v_ref):
    m = mod_ref[0]
    h = (_rms(x_ref[0]) * ng_ref[...] * (1.0 + m[1:2]) + m[0:1]).astype(BF16)
    p = _dot(h, w_ref[...])
    d = q_ref.shape[-1]
    q_ref[0] = (p[:, :d] * (C_DH ** -0.5 * LOG2E)).astype(BF16)
    k_ref[0] = p[:, d:2 * d].astype(BF16)
    v_ref[0] = p[:, 2 * d:].astype(BF16)


def _proj1(xa, mod, ng, w_qkv, lc):
    bsz, tt, d = xa.shape
    nct = lc // TM
    w = w_qkv.astype(BF16)
    tok = pl.BlockSpec((1, TM, d), lambda b, t: (b, t, 0))
    shp = jax.ShapeDtypeStruct((bsz, tt, d), BF16)
    return pl.pallas_call(
        _proj1_kernel,
        out_shape=(shp, shp, shp),
        grid=(bsz, tt // TM),
        in_specs=[tok, pl.BlockSpec((1, 6, d), lambda b, t: (_mod_row(b, t, nct, bsz), 0, 0)),
                  _const_spec((1, d)), _const_spec(w.shape)],
        out_specs=(tok, tok, tok),
        compiler_params=_params(("parallel", "parallel")),
        name="proj1",
    )(xa, mod, ng.reshape(1, d), w)


NAT_HG = 4
NAT_KEYS = WIN_R * GRID_W


def _nat_bias(rpb):
    qc = np.arange(GRID_W)[:, None]
    kc = np.arange(GRID_W)[None, :]
    c0 = np.clip(qc - WIN_C // 2, 0, GRID_W - WIN_C)
    inside = (kc >= c0) & (kc < c0 + WIN_C)
    dc = kc - qc + (WIN_C - 1)
    onehot = ((np.arange(2 * WIN_C - 1)[:, None, None] == dc[None]) & inside[None]).astype(np.float32)
    tab = jnp.einsum('hrd,dqk->hqrk', rpb, jnp.asarray(onehot), precision=HIGHEST)
    tab = tab + jnp.asarray(np.where(inside, 0.0, NEG).astype(np.float32))[None, :, None, :]
    tab = jnp.stack([tab[:, :, WIN_R - 1 - v:2 * WIN_R - 1 - v, :] for v in range(WIN_R)], axis=0)
    return tab.reshape(WIN_R, rpb.shape[0], GRID_W, NAT_KEYS)


NAT_RB = 4


def _nat_kernel(q_ref, k_ref, v_ref, bias_ref, o_ref, *, lc, rows):
    i = pl.program_id(2)
    kc = k_ref[0, :lc, :]
    vc = v_ref[0, :lc, :]
    lane = lax.broadcasted_iota(jnp.int32, (GRID_W, NAT_HG * C_DH), 1)
    masks = [(lane >= h * C_DH) & (lane < (h + 1) * C_DH) for h in range(NAT_HG)]
    for j in range(NAT_RB):
        r = i * NAT_RB + j
        r0 = jnp.clip(r - WIN_R // 2, 0, rows - WIN_R)
        start = pl.multiple_of(lc + r0 * GRID_W, GRID_W)
        kw = k_ref[0, pl.ds(start, NAT_KEYS), :]
        vw = v_ref[0, pl.ds(start, NAT_KEYS), :]
        q = q_ref[0, j * GRID_W:(j + 1) * GRID_W, :]
        qs = jnp.concatenate([jnp.where(hm, q, jnp.zeros_like(q)) for hm in masks], axis=0)
        s1 = _dot_nt(qs, kw) + bias_ref[r - r0, 0].reshape(NAT_HG * GRID_W, NAT_KEYS)
        s2 = _dot_nt(qs, kc)
        m = jnp.maximum(jnp.max(s1, axis=-1, keepdims=True), jnp.max(s2, axis=-1, keepdims=True))
        p1 = jnp.exp2(s1 - m)
        p2 = jnp.exp2(s2 - m)
        l = jnp.sum(p1, axis=-1, keepdims=True) + jnp.sum(p2, axis=-1, keepdims=True)
        o = (_dot(p1.astype(BF16), vw) + _dot(p2.astype(BF16), vc)) / l
        acc = jnp.zeros((GRID_W, NAT_HG * C_DH), F32)
        for h, hm in enumerate(masks):
            acc = jnp.where(hm, o[h * GRID_W:(h + 1) * GRID_W], acc)
        o_ref[0, j * GRID_W:(j + 1) * GRID_W, :] = acc.astype(BF16)


def _nat(q, k, v, rpb, lc):
    bsz, tt, d = q.shape
    l = tt - lc
    rows = l // GRID_W
    assert rows >= WIN_R and rows % NAT_RB == 0 and lc % (NAT_RB * GRID_W) == 0
    hw = NAT_HG * C_DH
    ng = d // hw
    bias = (_nat_bias(rpb) * LOG2E).reshape(WIN_R, ng, NAT_HG, GRID_W, NAT_KEYS)
    kern = functools.partial(_nat_kernel, lc=lc, rows=rows)
    qb = NAT_RB * GRID_W
    kv = pl.BlockSpec((1, tt, hw), lambda b, g, r: (b, 0, g))
    return pl.pallas_call(
        kern,
        out_shape=jax.ShapeDtypeStruct((bsz, l, d), BF16),
        grid=(bsz, ng, rows // NAT_RB),
        in_specs=[
            pl.BlockSpec((1, qb, hw), lambda b, g, r: (b, lc // qb + r, g)),
            kv, kv,
            pl.BlockSpec((WIN_R, 1, NAT_HG, GRID_W, NAT_KEYS), lambda b, g, r: (0, g, 0, 0, 0)),
        ],
        out_specs=pl.BlockSpec((1, qb, hw), lambda b, g, r: (b, r, g)),
        compiler_params=_params(("parallel", "parallel", "arbitrary")),
        name="nat",
    )(q, k, v, bias)


def _mix1_kernel(o_ref, x_ref, mod_ref, wo_ref, ng_ref, wr_ref, x_out, f_out, w_out):
    m = mod_ref[0]
    x1 = x_ref[0] + m[2:3] * _dot(o_ref[0], wo_ref[...])
    x_out[0] = x1
    f = _rms(x1) * ng_ref[...] * (1.0 + m[4:5]) + m[3:4]
    f_out[0] = f.astype(BF16)
    logits = _dot(f, wr_ref[...], precision=HIGHEST)
    lane = lax.broadcasted_iota(jnp.int32, logits.shape, 1)
    logits = jnp.where(lane < N_EXPERTS, logits, -jnp.inf)
    m1 = jnp.max(logits, axis=-1, keepdims=True)
    i1 = jnp.min(jnp.where(logits == m1, lane, LANES), axis=-1, keepdims=True)
    rest = jnp.where(lane == i1, -jnp.inf, logits)
    m2 = jnp.max(rest, axis=-1, keepdims=True)
    i2 = jnp.min(jnp.where(rest == m2, lane, LANES), axis=-1, keepdims=True)
    e2 = jnp.exp(m2 - m1)
    g1 = 1.0 / (1.0 + e2)
    w_out[0] = jnp.where(lane == i1, g1, 0.0) + jnp.where(lane == i2, e2 * g1, 0.0)


def _mix1(o, xa, mod, w_out, ng, w_router, lc):
    bsz, tt, d = xa.shape
    l = tt - lc
    nct = lc // TM
    wr = jnp.concatenate([w_router, jnp.zeros((d, LANES - N_EXPERTS), F32)], axis=1)
    tok = lambda w: pl.BlockSpec((1, TM, w), lambda b, t: (b, t, 0))
    return pl.pallas_call(
        _mix1_kernel,
        out_shape=(jax.ShapeDtypeStruct((bsz, l, d), F32), jax.ShapeDtypeStruct((bsz, l, d), BF16),
                   jax.ShapeDtypeStruct((bsz, l, LANES), F32)),
        grid=(bsz, l // TM),
        in_specs=[tok(d), pl.BlockSpec((1, TM, d), lambda b, t: (b, t + nct, 0)),
                  pl.BlockSpec((1, 6, d), lambda b, t: (b, 0, 0)),
                  _const_spec((d, d)), _const_spec((1, d)), _const_spec(wr.shape)],
        out_specs=(tok(d), tok(d), tok(LANES)),
        compiler_params=_params(("parallel", "parallel")),
        name="mix1_router",
    )(o, xa, mod, w_out.astype(BF16), ng.reshape(1, d), wr)


def _moe_kernel(f_ref, w_ref, x_ref, mod_ref, wg_ref, wu_ref, wd_ref, fg_ref, o_ref, acc_ref):
    e = pl.program_id(2)
    c = pl.program_id(3)

    @pl.when((e == 0) & (c == 0))
    def _():
        acc_ref[...] = jnp.zeros_like(acc_ref)

    f = f_ref[0]
    hid = _silu(_dot(f, wg_ref[0].astype(BF16))) * _dot(f, wu_ref[0].astype(BF16))
    w = w_ref[0]
    lane = lax.broadcasted_iota(jnp.int32, w.shape, 1)
    we = jnp.sum(jnp.where(lane == e, w, 0.0), axis=-1, keepdims=True)
    acc_ref[...] += _dot((hid * we).astype(BF16), wd_ref[0].astype(BF16))

    @pl.when((e == pl.num_programs(2) - 1) & (c == pl.num_programs(3) - 1))
    def _():
        m = mod_ref[0]
        x2 = x_ref[0] + m[5:6] * acc_ref[...]
        o_ref[0] = _rms(x2) * fg_ref[...]


def _moe(f, w, x, mod, wg, wu, wd, fg):
    bsz, l, d = x.shape
    ne, _, dff = wg.shape
    tmx = _pick_chunk(l, (1024, 512, 256))
    tf = 512
    tok = lambda wd_: pl.BlockSpec((1, tmx, wd_), lambda b, t, e, c: (b, t, 0))
    return pl.pallas_call(
        _moe_kernel,
        out_shape=jax.ShapeDtypeStruct((bsz, l, d), F32),
        grid=(bsz, l // tmx, ne, dff // tf),
        in_specs=[
            tok(d), tok(LANES), tok(d),
            pl.BlockSpec((1, 6, d), lambda b, t, e, c: (b, 0, 0)),
            pl.BlockSpec((1, d, tf), lambda b, t, e, c: (e, 0, c)),
            pl.BlockSpec((1, d, tf), lambda b, t, e, c: (e, 0, c)),
            pl.BlockSpec((1, tf, d), lambda b, t, e, c: (e, c, 0)),
            pl.BlockSpec((1, d), lambda b, t, e, c: (0, 0)),
        ],
        out_specs=tok(d),
        scratch_shapes=[pltpu.VMEM((tmx, d), F32)],
        compiler_params=_params(("parallel", "parallel", "arbitrary", "arbitrary")),
        name="moe_ffn",
    )(f, w, x, mod, wg, wu, wd, fg.reshape(1, d))


def kernel(x, c, ctx, c_ctx, ada_w, ada_b, norm_g, ev_w_in, ev_conv_w, ev_a_log, ev_dt_bias, ev_gdn_norm_g, ev_q_norm_g, ev_w_uq, ev_kv_norm_g, ev_w_ukv, ev_w_out, ev_ffn_wg, ev_ffn_wu, ev_ffn_wd, od_w_qkv, od_rpb, od_w_out, od_router, od_exp_wg, od_exp_wu, od_exp_wd, final_norm_g):
    bsz, l, d = x.shape
    lc = ctx.shape[1]
    assert ada_w.shape[0] == 2 and bsz < MOD_ROWS and lc % TM == 0 and l % TM == 0
    xa = jnp.concatenate([ctx, x], axis=1)

    cs = jnp.concatenate([c, c_ctx[None], jnp.zeros((MOD_ROWS - bsz - 1, d), F32)], axis=0)
    mods = _ada_mod(cs, ada_w, ada_b).reshape(2, MOD_ROWS, 6, d)

    win, wq, wqr, wk, wv = _proj0_weights(ev_w_in[0], ev_w_uq[0], ev_w_ukv[0])
    qkv_pre, z, ab, q, k, v = _proj0(xa, mods[0], norm_g[0, 0],
                                     (win, ev_q_norm_g[0], wq, wqr, ev_kv_norm_g[0], wk, wv),
                                     _rope_tables(lc, l), lc)
    o_f, o_b = _gdn(qkv_pre, ab, ev_conv_w[0], ev_a_log[0], ev_dt_bias[0], lc)
    yb = _mla_attention(q, k, v, lc)
    xa = _mix0(o_f, o_b, z, yb, xa, mods[0], ev_gdn_norm_g[0], ev_w_out[0], norm_g[0, 1],
               ev_ffn_wg[0], ev_ffn_wu[0], ev_ffn_wd[0], lc)

    q1, k1, v1 = _proj1(xa, mods[1], norm_g[1, 0], od_w_qkv[0], lc)
    o1 = _nat(q1, k1, v1, od_rpb[0], lc)
    x1, f1, gates = _mix1(o1, xa, mods[1], od_w_out[0], norm_g[1, 1], od_router[0], lc)
    return _moe(f1, gates, x1, mods[1], od_exp_wg[0], od_exp_wu[0], od_exp_wd[0], final_norm_g)
```

```python
import functools

import numpy as np
import jax
import jax.numpy as jnp
from jax import lax
from jax.experimental import pallas as pl
from jax.experimental.pallas import tpu as pltpu

F32 = jnp.float32
BF16 = jnp.bfloat16
HIGHEST = lax.Precision.HIGHEST

GRID_W = 64
NORM_EPS = 1e-6
A_HEADS, A_DK, A_DV = 4, 128, 128
GDN_CHUNK_LOG2 = 7
GDN_CHUNK = 1 << GDN_CHUNK_LOG2
GDN_BASE_LOG2 = 3
B_HEADS, B_Q_RANK, B_KV_RANK, B_NOPE, B_ROPE, B_DV = 8, 384, 256, 64, 32, 64
ROPE_THETA = 10000.0
C_HEADS, C_DH = 16, 64
WIN_R, WIN_C = 8, 16
N_EXPERTS = 8
LANES = 128
MOD_ROWS = 8
NEG = -1e30
LOG2E = 1.4426950408889634
VMEM_LIMIT = 56 * 1024 * 1024

TM = 256


def _params(sem):
    return pltpu.CompilerParams(dimension_semantics=sem, vmem_limit_bytes=VMEM_LIMIT)


def _const_spec(shape):
    nd = len(shape)
    return pl.BlockSpec(shape, lambda *_: (0,) * nd, pipeline_mode=pl.Buffered(1))


def _rms(x):
    return x * lax.rsqrt(jnp.mean(x * x, axis=-1, keepdims=True) + NORM_EPS)


def _silu(x):
    return x * jax.nn.sigmoid(x)


def _onehot(mask):
    return jnp.where(mask, 1.0, 0.0).astype(BF16)


def _dot(a, b, **kw):
    return jnp.dot(a, b, preferred_element_type=F32, **kw)


def _dot_nt(a, b, **kw):
    return lax.dot_general(a, b, (((1,), (1,)), ((), ())), preferred_element_type=F32, **kw)


def _ada_kernel(s_ref, w_ref, b_ref, o_ref):
    s = _silu(s_ref[...])
    o_ref[0] = _dot(s, w_ref[0], precision=HIGHEST) + b_ref[0]


def _ada_mod(cs, ada_w, ada_b):
    depth, d, d6 = ada_w.shape
    tn = d6 // 4
    return pl.pallas_call(
        _ada_kernel,
        out_shape=jax.ShapeDtypeStruct((depth, MOD_ROWS, d6), F32),
        grid=(depth, d6 // tn),
        in_specs=[
            pl.BlockSpec((MOD_ROWS, d), lambda i, j: (0, 0)),
            pl.BlockSpec((1, d, tn), lambda i, j: (i, 0, j)),
            pl.BlockSpec((1, 1, tn), lambda i, j: (i, 0, j)),
        ],
        out_specs=pl.BlockSpec((1, MOD_ROWS, tn), lambda i, j: (i, 0, j)),
        compiler_params=_params(("parallel", "parallel")),
        name="ada_mod",
    )(cs, ada_w, ada_b.reshape(depth, 1, d6))


N_QKVZ = 4 * A_HEADS * A_DK
IN_OFF_AB = N_QKVZ
IN_OFF_CQ = IN_OFF_AB + LANES
IN_OFF_CKV = IN_OFF_CQ + B_Q_RANK
IN_OFF_KR = IN_OFF_CKV + B_KV_RANK
IN_COLS = IN_OFF_KR + LANES
HW = B_HEADS * LANES


def _proj0_kernel(x_ref, mod_ref, ng_ref, win_ref, qg_ref, wq_ref, wqr_ref, kvg_ref, wk_ref, wv_ref,
                  e_ref, er_ref, vone_ref, cq_ref, sq_ref, ck_ref, sk_ref,
                  qkv_ref, z_ref, ab_ref, q_ref, k_ref, vt_ref):
    x = x_ref[0]
    m = mod_ref[0]
    h = _rms(x) * ng_ref[...] * (1.0 + m[1:2]) + m[0:1]
    p = _dot(h.astype(BF16), win_ref[...])
    qkv_ref[0] = p[:, :3 * A_HEADS * A_DK]
    z_ref[0] = p[:, 3 * A_HEADS * A_DK:N_QKVZ]
    ab_ref[0] = p[:, IN_OFF_AB:IN_OFF_CQ]
    nq = (_rms(p[:, IN_OFF_CQ:IN_OFF_CKV]) * qg_ref[...]).astype(BF16)
    nkv = (_rms(p[:, IN_OFF_CKV:IN_OFF_KR]) * kvg_ref[...]).astype(BF16)
    krp = p[:, IN_OFF_KR:IN_COLS]
    qa = _dot(nq, wq_ref[...])
    qb = _dot(nq, wqr_ref[...])
    kn = _dot(nkv, wk_ref[...])
    vv = _dot(nkv, wv_ref[...]) + vone_ref[...]
    ka = _dot(krp, e_ref[...], precision=HIGHEST)
    kb = _dot(krp, er_ref[...], precision=HIGHEST)
    cq, sq, ck, sk = cq_ref[...], sq_ref[...], ck_ref[...], sk_ref[...]
    for hh in range(B_HEADS):
        sl = slice(hh * LANES, (hh + 1) * LANES)
        q_ref[0, hh] = (qa[:, sl] * cq + qb[:, sl] * sq).astype(BF16)
        k_ref[0, hh] = (kn[:, sl] + ka[:, sl] * ck + kb[:, sl] * sk).astype(BF16)
        vt_ref[0, hh] = vv[:, sl].T.astype(BF16)


def _rot_cols(w):
    q = B_ROPE // 4
    return jnp.concatenate([-w[:, q:2 * q], w[:, :q], -w[:, 3 * q:], w[:, 2 * q:3 * q]], axis=1)


def _proj0_weights(w_in, w_uq, w_ukv):
    d = w_in.shape[0]
    offs = np.cumsum([0, 512, 512, 512, 512, 8, 8, B_Q_RANK, B_KV_RANK, B_ROPE])
    zeros = lambda n: jnp.zeros((d, n), F32)
    kr = w_in[:, offs[8]:offs[9]]
    win = jnp.concatenate([
        w_in[:, :offs[4]],
        w_in[:, offs[4]:offs[6]], zeros(LANES - 16),
        w_in[:, offs[6]:offs[7]],
        w_in[:, offs[7]:offs[8]],
        kr, zeros(LANES - B_ROPE),
    ], axis=1).astype(BF16)
    dq = B_NOPE + B_ROPE
    wq3 = w_uq.reshape(B_Q_RANK, B_HEADS, dq)
    zq = jnp.zeros((B_Q_RANK, B_HEADS, LANES - dq), F32)
    wq = jnp.concatenate([wq3, zq], axis=2).reshape(B_Q_RANK, HW).astype(BF16)
    rot = jnp.stack([_rot_cols(wq3[:, hh, B_NOPE:]) for hh in range(B_HEADS)], axis=1)
    wqr = jnp.concatenate([jnp.zeros((B_Q_RANK, B_HEADS, B_NOPE), F32), rot, zq], axis=2)
    wqr = wqr.reshape(B_Q_RANK, HW).astype(BF16)
    wkv3 = w_ukv.reshape(B_KV_RANK, B_HEADS, B_NOPE + B_DV)
    zk = jnp.zeros((B_KV_RANK, B_HEADS, LANES - B_NOPE), F32)
    wk = jnp.concatenate([wkv3[:, :, :B_NOPE], zk], axis=2).reshape(B_KV_RANK, HW).astype(BF16)
    vpart = wkv3[:, :, B_NOPE:]
    zv = jnp.zeros_like(vpart)
    even = (jnp.arange(B_HEADS) % 2 == 0)[None, :, None]
    wv = jnp.concatenate([jnp.where(even, vpart, zv), jnp.where(even, zv, vpart)], axis=2)
    wv = wv.reshape(B_KV_RANK, HW).astype(BF16)
    return win, wq, wqr, wk, wv


def _rope_select():
    e = np.zeros((LANES, HW), np.float32)
    er = np.zeros((LANES, HW), np.float32)
    q = B_ROPE // 4
    rmat = np.zeros((B_ROPE, B_ROPE), np.float32)
    for i in range(q):
        rmat[q + i, i] = -1.0
        rmat[i, q + i] = 1.0
        rmat[3 * q + i, 2 * q + i] = -1.0
        rmat[2 * q + i, 3 * q + i] = 1.0
    for hh in range(B_HEADS):
        base = hh * LANES + B_NOPE
        e[:B_ROPE, base:base + B_ROPE] = np.eye(B_ROPE, dtype=np.float32)
        er[:B_ROPE, base:base + B_ROPE] = rmat
    vone = np.zeros((1, HW), np.float32)
    for hh in range(B_HEADS):
        lo = hh * LANES + (B_DV if hh % 2 == 0 else 0)
        vone[0, lo:lo + B_DV] = 1.0
    return jnp.asarray(e), jnp.asarray(er), jnp.asarray(vone)


def _rope_tables(lc, l):
    t = jnp.arange(l)
    row = (t // GRID_W).astype(F32)
    col = (t % GRID_W).astype(F32)
    half = B_ROPE // 2
    inv = ROPE_THETA ** (-jnp.arange(0, half, 2, dtype=F32) / half)
    ar = row[:, None] * inv[None, :]
    ac = col[:, None] * inv[None, :]
    ang = jnp.concatenate([ar, ar, ac, ac], axis=-1)
    cos = jnp.concatenate([jnp.ones((lc, B_ROPE), F32), jnp.cos(ang)], axis=0)
    sin = jnp.concatenate([jnp.zeros((lc, B_ROPE), F32), jnp.sin(ang)], axis=0)
    tt = lc + l
    scale = (B_NOPE + B_ROPE) ** -0.5 * LOG2E
    pad =jnp.zeros((tt, LANES - B_NOPE - B_ROPE), F32)
    z64 = jnp.zeros((tt, B_NOPE), F32)
    cq = jnp.concatenate([jnp.full((tt, B_NOPE), scale, F32), scale * cos, pad], axis=1)
    sq = jnp.concatenate([z64, scale * sin, pad], axis=1)
    ck = jnp.concatenate([z64, cos, pad], axis=1)
    sk = jnp.concatenate([z64, sin, pad], axis=1)
    return cq, sq, ck, sk


def _mod_row(b, t, n_ctx_tiles, bsz):
    return jnp.where(t < n_ctx_tiles, bsz, b)


def _proj0(xa, mod, ng, weights, tables, lc):
    bsz, tt, d = xa.shape
    win, qg, wq, wqr, kvg, wk, wv = weights
    e, er, vone = _rope_select()
    nct = lc // TM
    tok = lambda w: pl.BlockSpec((1, TM, w), lambda b, t: (b, t, 0))
    head = pl.BlockSpec((1, B_HEADS, TM, LANES), lambda b, t: (b, 0, t, 0))
    head_t = pl.BlockSpec((1, B_HEADS, LANES, TM), lambda b, t: (b, 0, 0, t))
    tab = pl.BlockSpec((TM, LANES), lambda b, t: (t, 0))
    hshape = jax.ShapeDtypeStruct((bsz, B_HEADS, tt, LANES), BF16)
    return pl.pallas_call(
        _proj0_kernel,
        out_shape=(
            jax.ShapeDtypeStruct((bsz, tt, 3 * A_HEADS * A_DK), F32),
            jax.ShapeDtypeStruct((bsz, tt, A_HEADS * A_DV), F32),
            jax.ShapeDtypeStruct((bsz, tt, LANES), F32),
            hshape, hshape, jax.ShapeDtypeStruct((bsz, B_HEADS, LANES, tt), BF16),
        ),
        grid=(bsz, tt // TM),
        in_specs=[
            tok(d),
            pl.BlockSpec((1, 6, d), lambda b, t: (_mod_row(b, t, nct, bsz), 0, 0)),
            _const_spec((1, d)), _const_spec(win.shape),
            _const_spec((1, B_Q_RANK)), _const_spec(wq.shape), _const_spec(wqr.shape),
            _const_spec((1, B_KV_RANK)), _const_spec(wk.shape), _const_spec(wv.shape),
            _const_spec(e.shape), _const_spec(er.shape), _const_spec(vone.shape),
            tab, tab, tab, tab,
        ],
        out_specs=(tok(3 * A_HEADS * A_DK), tok(A_HEADS * A_DV), tok(LANES), head, head, head_t),
        compiler_params=_params(("parallel", "parallel")),
        name="proj0",
    )(xa, mod, ng.reshape(1, d), win, qg.reshape(1, -1), wq, wqr, kvg.reshape(1, -1), wk, wv, e, er, vone,
      *tables)


A_CONV = 5
CONV_HALO = 8


def _split_bf16(a):
    hi = a.astype(BF16)
    return hi, (a - hi.astype(F32)).astype(BF16)


def _dot3(a, b):
    ah, al = _split_bf16(a)
    bh, bl = _split_bf16(b)
    return _dot(ah, bh) + (_dot(ah, bl) + _dot(al, bh))


def _bdot3(a, b):
    bd = lambda u, v: lax.dot_general(u, v, (((2,), (1,)), ((0,), (0,))), preferred_element_type=F32)
    ah, al = _split_bf16(a)
    bh, bl = _split_bf16(b)
    return bd(ah, bh) + (bd(ah, bl) + bd(al, bh))


def _gdn_pre_kernel(x_ref, prev_ref, next_ref, ab_ref, cw_ref, alog_ref, dtb_ref,
                    u_ref, wq_ref, kgt_ref, qk_ref, gl_ref, *, ncc, nc):
    c = GDN_CHUNK
    i = pl.program_id(1)
    halo = CONV_HALO
    prev_ok = ((i != 0) & (i != ncc)).astype(F32)
    next_ok = ((i != ncc - 1) & (i != nc - 1)).astype(F32)
    xe = jnp.concatenate([prev_ref[0] * prev_ok, x_ref[0], next_ref[0] * next_ok], axis=0)
    cw = cw_ref[...]
    taps = A_CONV
    y = sum(cw[j:j + 1] * xe[halo - taps // 2 + j:halo - taps // 2 + j + c] for j in range(taps))
    y = _silu(y)
    nq = A_HEADS * A_DK

    def l2n(t):
        return t * lax.rsqrt(jnp.sum(t * t, axis=-1, keepdims=True) + 1e-6)

    ab = ab_ref[0]
    g_all = -jnp.exp(alog_ref[...]) * (jnp.maximum(ab + dtb_ref[...], 0.0)
                                       + jnp.log1p(jnp.exp(-jnp.abs(ab + dtb_ref[...]))))
    beta_all = jax.nn.sigmoid(ab)
    g_all_t = g_all.T
    g_tot = jnp.sum(g_all, axis=0, keepdims=True)
    row = lax.broadcasted_iota(jnp.int32, (c, c), 0)
    col = lax.broadcasted_iota(jnp.int32, (c, c), 1)
    eye = (row == col).astype(F32)
    ns, rhs, qgs = [], [], []
    heads = []
    for h in range(A_HEADS):
        sl = slice(h * A_DK, (h + 1) * A_DK)
        kh = l2n(y[:, nq + h * A_DK:nq + (h + 1) * A_DK])
        heads.append((l2n(y[:, sl]) * (A_DK ** -0.5), kh, y[:, 2 * nq + h * A_DV:2 * nq + (h + 1) * A_DV],
                      kh.astype(BF16)))
    for d in range(2):
        incl = (row >= col) if d == 0 else (row <= col)
        strict = (row > col) if d == 0 else (row < col)
        m_incl = incl.astype(F32)
        gc_col = _dot(m_incl, g_all, precision=HIGHEST)
        gc_row = _dot_nt(g_all_t, m_incl, precision=HIGHEST)
        gl_rows = []
        for h in range(A_HEADS):
            qh, kh, vh, khb = heads[h]
            ln = d * A_HEADS + h
            beta = beta_all[:, 2 * A_HEADS + ln:2 * A_HEADS + ln + 1]
            gcc = gc_col[:, ln:ln + 1]
            gcr = gc_row[ln:ln + 1, :]
            gtot = g_tot[:, ln:ln + 1]
            decay = jnp.where(incl, jnp.exp(jnp.where(incl, gcc - gcr, 0.0)), 0.0)
            n = -jnp.where(strict, beta * _dot_nt(khb, khb) * decay, 0.0)
            eg = jnp.exp(gcc)
            ns.append(n)
            rhs.append(jnp.concatenate([vh * beta, kh * (beta * eg)], axis=1))
            qgs.append(qh * eg)
            kgt_ref[0, d, 0, h] = (kh * jnp.exp(gtot - gcc)).T.astype(BF16)
            qk_ref[0, d, 0, h] = jnp.where(incl, _dot_nt(qh.astype(BF16), khb) * decay, 0.0).astype(BF16)
            gl_rows.append(jnp.broadcast_to(jnp.exp(gtot), (1, LANES)))
        gl_ref[0, d, 0] = jnp.concatenate(gl_rows + gl_rows, axis=0)
    n = jnp.stack(ns, axis=0)
    same = lambda s: lax.shift_right_logical(row, s) == lax.shift_right_logical(col, s)
    nd = jnp.where(same(GDN_BASE_LOG2), n, 0.0)
    x = eye + nd
    p = _bdot3(nd, nd)
    z = _bdot3(p, jnp.concatenate([p, x], axis=2))
    x = x + z[:, :, c:]
    x = x + _bdot3(z[:, :, :c], x)
    for s in range(GDN_BASE_LOG2, GDN_CHUNK_LOG2):
        nl = jnp.where(same(s + 1) & jnp.logical_not(same(s)), n, 0.0)
        x = x + _bdot3(_bdot3(x, nl), x)
    y = _bdot3(x, jnp.stack(rhs, axis=0))
    for d in range(2):
        for h in range(A_HEADS):
            i = d * A_HEADS + h
            u_ref[0, d, 0, h] = y[i, :, :A_DV]
            wq_ref[0, d, 0, h] = jnp.concatenate([y[i, :, A_DV:], qgs[i]], axis=0).astype(BF16)


def _gdn_scan_kernel(*refs, bsz):
    ins, (of_ref, ob_ref, s_ref) = refs[:10], refs[10:]

    @pl.when(pl.program_id(0) == 0)
    def _():
        s_ref[...] = jnp.zeros_like(s_ref)

    c = GDN_CHUNK
    for d, o_ref in ((0, of_ref), (1, ob_ref)):
        u_ref, wq_ref, kgt_ref, qk_ref, gl_ref = ins[5 * d:5 * d + 5]
        for b in range(bsz):
            for h in range(A_HEADS):
                s = s_ref[b, d, h]
                ws = _dot(wq_ref[b, 0, 0, h], s.astype(BF16))
                v_new = (u_ref[b, 0, 0, h] - ws[:c]).astype(BF16)
                o_ref[b, :, h * A_DV:(h + 1) * A_DV] = ws[c:] + _dot(qk_ref[b, 0, 0, h], v_new)
                s_ref[b, d, h] = s * gl_ref[b, 0, 0, h:h + 1, :] + _dot(kgt_ref[b, 0, 0, h], v_new)


def _gdn(qkv_pre, ab, conv_w, a_log, dt_bias, lc):
    bsz, tt, wqkv = qkv_pre.shape
    c = GDN_CHUNK
    nc, ncc = tt // c, lc // c
    hb = c // CONV_HALO
    per = lambda r, w: pl.BlockSpec((1, 2, 1, A_HEADS, r, w), lambda b, i: (b, 0, i, 0, 0, 0))
    shp = lambda r, w, dt: jax.ShapeDtypeStruct((bsz, 2, nc, A_HEADS, r, w), dt)
    lane_row = lambda t: jnp.concatenate([t.reshape(1, -1), jnp.zeros((1, LANES - t.size), F32)], axis=1)
    cw = jnp.concatenate([conv_w, jnp.zeros((CONV_HALO - A_CONV, wqkv), F32)], axis=0)
    pre = pl.pallas_call(
        functools.partial(_gdn_pre_kernel, ncc=ncc, nc=nc),
        out_shape=(shp(c, A_DV, F32), shp(2 * c, A_DK, BF16), shp(A_DK, c, BF16), shp(c, c, BF16),
                   jax.ShapeDtypeStruct((bsz, 2, nc, 2 * A_HEADS, LANES), F32)),
        grid=(bsz, nc),
        in_specs=[pl.BlockSpec((1, c, wqkv), lambda b, i: (b, i, 0)),
                  pl.BlockSpec((1, CONV_HALO, wqkv), lambda b, i: (b, jnp.maximum(i * hb - 1, 0), 0)),
                  pl.BlockSpec((1, CONV_HALO, wqkv), lambda b, i: (b, jnp.minimum((i + 1) * hb, nc * hb - 1), 0)),
                  pl.BlockSpec((1, c, LANES), lambda b, i: (b, i, 0)),
                  _const_spec(cw.shape), _const_spec((1, LANES)), _const_spec((1, LANES))],
        out_specs=(per(c, A_DV), per(2 * c, A_DK), per(A_DK, c), per(c, c),
                   pl.BlockSpec((1, 2, 1, 2 * A_HEADS, LANES), lambda b, i: (b, 0, i, 0, 0))),
        compiler_params=_params(("parallel", "parallel")),
        name="gdn_pre",
    )(qkv_pre, qkv_pre, qkv_pre, ab, cw, lane_row(a_log), lane_row(dt_bias))

    def chunk(d, s):
        return s if d == 0 else jnp.where(s < ncc, ncc - 1 - s, nc - 1 - (s - ncc))

    in_specs = []
    for d in range(2):
        for arr in pre[:4]:
            r, w = arr.shape[-2:]
            in_specs.append(pl.BlockSpec((bsz, 1, 1, A_HEADS, r, w),
                                         lambda s, d=d: (0, d, chunk(d, s), 0, 0, 0)))
        in_specs.append(pl.BlockSpec((bsz, 1, 1, 2 * A_HEADS, LANES), lambda s, d=d: (0, d, chunk(d, s), 0, 0)))
    o_shape = jax.ShapeDtypeStruct((bsz, tt, A_HEADS * A_DV), F32)
    return pl.pallas_call(
        functools.partial(_gdn_scan_kernel, bsz=bsz),
        out_shape=(o_shape, o_shape),
        grid=(nc,),
        in_specs=in_specs,
        out_specs=tuple(pl.BlockSpec((bsz, c, A_HEADS * A_DV), lambda s, d=d: (0, chunk(d, s), 0))
                        for d in range(2)),
        scratch_shapes=[pltpu.VMEM((bsz, 2, A_HEADS, A_DK, A_DV), F32)],
        compiler_params=_params(("arbitrary",)),
        name="gdn_scan",
    )(*pre, *pre)


def _pick_chunk(n, options):
    for o in options:
        if n % o == 0:
            return o
    raise ValueError(f"no chunk size in {options} divides {n}")


def _mla_kernel(q_ref, k_ref, vt_ref, o_ref, *, lc, tq, tk, nk):
    t = pl.program_id(2)
    half = LANES // 2
    g = q_ref.shape[1]
    bnt = lambda a, b: lax.dot_general(a, b, (((2,), (2,)), ((0,), (0,))), preferred_element_type=F32)
    bnn = lambda a, b: lax.dot_general(a, b, (((2,), (1,)), ((0,), (0,))), preferred_element_type=F32)

    def update(q, m, acc, k_chunk, vt_chunk):
        s = bnt(k_chunk, q)
        m_new = jnp.maximum(m, jnp.max(s, axis=1, keepdims=True))
        p = jnp.exp2(s - m_new).astype(BF16)
        return m_new, jnp.exp2(m - m_new) * acc + bnn(vt_chunk, p)

    def finish(acc):
        parts = [acc[h, :half] / acc[h, half:] if h % 2 == 0 else acc[h, half:] / acc[h, :half]
                 for h in range(g)]
        o_ref[0] = jnp.concatenate(parts, axis=0).T

    init = (jnp.full((g, 1, tq), -jnp.inf, F32), jnp.zeros((g, LANES, tq), F32))

    @pl.when(t < lc // tq)
    def _():
        finish(update(q_ref[0], *init, k_ref[0, :, :lc, :], vt_ref[0, :, :, :lc])[1])

    @pl.when(t >= lc // tq)
    def _():
        q = q_ref[0]

        def body(j, carry):
            off = pl.multiple_of(j * tk, tk)
            return update(q, *carry, k_ref[0, :, pl.ds(off, tk), :], vt_ref[0, :, :, pl.ds(off, tk)])

        finish(lax.fori_loop(0, nk, body, init)[1])


MLA_G = 8


def _mla_attention(q, k, vt, lc):
    bsz, nh, tt, _ = q.shape
    tq = TM
    tk = _pick_chunk(tt, (768, 512, 256))
    g = MLA_G
    kern = functools.partial(_mla_kernel, lc=lc, tq=tq, tk=tk, nk=tt // tk)
    return pl.pallas_call(
        kern,
        out_shape=jax.ShapeDtypeStruct((bsz, tt, nh // 2 * LANES), F32),
        grid=(bsz, nh // g, tt // tq),
        in_specs=[pl.BlockSpec((1, g, tq, LANES), lambda b, hp, t: (b, hp, t, 0)),
                  pl.BlockSpec((1, g, tt, LANES), lambda b, hp, t: (b, hp, 0, 0), pipeline_mode=pl.Buffered(1)),
                  pl.BlockSpec((1, g, LANES, tt), lambda b, hp, t: (b, hp, 0, 0), pipeline_mode=pl.Buffered(1))],
        out_specs=pl.BlockSpec((1, tq, g // 2 * LANES), lambda b, hp, t: (b, t, hp)),
        compiler_params=_params(("parallel", "parallel", "parallel")),
        name="mla_attention",
    )(q, k, vt)


def _mix0_kernel(of_ref, ob_ref, z_ref, yb_ref, x_ref, mod_ref, gg_ref, woa_ref, wob_ref, ng_ref,
                 wg_ref, wu_ref, wd_ref, o_ref):
    o = of_ref[0] + ob_ref[0]
    z = z_ref[0]
    gg = gg_ref[...]
    parts = []
    for h in range(A_HEADS):
        sl = slice(h * A_DV, (h + 1) * A_DV)
        parts.append((_rms(o[:, sl]) * gg * _silu(z[:, sl])).astype(BF16))
    ya = jnp.concatenate(parts, axis=1)
    y = _dot(ya, woa_ref[...]) + _dot(yb_ref[0].astype(BF16), wob_ref[...])
    m = mod_ref[0]
    x1 = x_ref[0] + m[2:3] * y
    f = (_rms(x1) * ng_ref[...] * (1.0 + m[4:5]) + m[3:4]).astype(BF16)
    hid = _silu(_dot(f, wg_ref[...])) * _dot(f, wu_ref[...])
    o_ref[0] = x1 + m[5:6] * _dot(hid.astype(BF16), wd_ref[...])


def _mix0(o_f, o_b, z, yb, xa, mod, gdn_g, w_out, ng, wg, wu, wd, lc):
    bsz, tt, d = xa.shape
    nct = lc // TM
    na = A_HEADS * A_DV
    woa, wob = w_out[:na].astype(BF16), w_out[na:].astype(BF16)
    wg, wu, wd = wg.astype(BF16), wu.astype(BF16), wd.astype(BF16)
    tok = lambda w: pl.BlockSpec((1, TM, w), lambda b, t: (b, t, 0))
    return pl.pallas_call(
        _mix0_kernel,
        out_shape=jax.ShapeDtypeStruct((bsz, tt, d), F32),
        grid=(bsz, tt // TM),
        in_specs=[
            tok(na), tok(na), tok(na), tok(B_HEADS * B_DV), tok(d),
            pl.BlockSpec((1, 6, d), lambda b, t: (_mod_row(b, t, nct, bsz), 0, 0)),
            _const_spec((1, A_DV)), _const_spec(woa.shape), _const_spec(wob.shape), _const_spec((1, d)),
            _const_spec(wg.shape), _const_spec(wu.shape), _const_spec(wd.shape),
        ],
        out_specs=tok(d),
        compiler_params=_params(("parallel", "parallel")),
        name="mix0_ffn",
    )(o_f, o_b, z, yb, xa, mod, gdn_g.reshape(1, -1), woa, wob, ng.reshape(1, d), wg, wu, wd)


def _proj1_kernel(x_ref, mod_ref, ng_ref, w_ref, q_ref, k_ref, v_ref):
    m = mod_ref[0]
    h = (_rms(x_ref[0]) * ng_ref[...] * (1.0 + m[1:2]) + m[0:1]).astype(BF16)
    p = _dot(h, w_ref[...])
    d = q_ref.shape[-1]
    q_ref[0] = (p[:, :d] * (C_DH ** -0.5 * LOG2E)).astype(BF16)
    k_ref[0] = p[:, d:2 * d].astype(BF16)
    v_ref[0] = p[:, 2 * d:].astype(BF16)


def _proj1(xa, mod, ng, w_qkv, lc):
    bsz, tt, d = xa.shape
    nct = lc // TM
    w = w_qkv.astype(BF16)
    tok = pl.BlockSpec((1, TM, d), lambda b, t: (b, t, 0))
    shp = jax.ShapeDtypeStruct((bsz, tt, d), BF16)
    return pl.pallas_call(
        _proj1_kernel,
        out_shape=(shp, shp, shp),
        grid=(bsz, tt // TM),
        in_specs=[tok, pl.BlockSpec((1, 6, d), lambda b, t: (_mod_row(b, t, nct, bsz), 0, 0)),
                  _const_spec((1, d)), _const_spec(w.shape)],
        out_specs=(tok, tok, tok),
        compiler_params=_params(("parallel", "parallel")),
        name="proj1",
    )(xa, mod, ng.reshape(1, d), w)


NAT_HG = 4
NAT_KEYS = WIN_R * GRID_W


def _nat_bias(rpb):
    qc = np.arange(GRID_W)[:, None]
    kc = np.arange(GRID_W)[None, :]
    c0 = np.clip(qc - WIN_C // 2, 0, GRID_W - WIN_C)
    inside = (kc >= c0) & (kc < c0 + WIN_C)
    dc = kc - qc + (WIN_C - 1)
    onehot = ((np.arange(2 * WIN_C - 1)[:, None, None] == dc[None]) & inside[None]).astype(np.float32)
    tab = jnp.einsum('hrd,dqk->hqrk', rpb, jnp.asarray(onehot), precision=HIGHEST)
    tab = tab + jnp.asarray(np.where(inside, 0.0, NEG).astype(np.float32))[None, :, None, :]
    tab = jnp.stack([tab[:, :, WIN_R - 1 - v:2 * WIN_R - 1 - v, :] for v in range(WIN_R)], axis=0)
    return tab.reshape(WIN_R, rpb.shape[0], GRID_W, NAT_KEYS)


NAT_RB = 4


def _nat_kernel(q_ref, k_ref, v_ref, bias_ref, o_ref, *, lc, rows):
    i = pl.program_id(2)
    kc = k_ref[0, :lc, :]
    vc = v_ref[0, :lc, :]
    lane = lax.broadcasted_iota(jnp.int32, (GRID_W, NAT_HG * C_DH), 1)
    masks = [(lane >= h * C_DH) & (lane < (h + 1) * C_DH) for h in range(NAT_HG)]
    for j in range(NAT_RB):
        r = i * NAT_RB + j
        r0 = jnp.clip(r - WIN_R // 2, 0, rows - WIN_R)
        start = pl.multiple_of(lc + r0 * GRID_W, GRID_W)
        kw = k_ref[0, pl.ds(start, NAT_KEYS), :]
        vw = v_ref[0, pl.ds(start, NAT_KEYS), :]
        q = q_ref[0, j * GRID_W:(j + 1) * GRID_W, :]
        qs = jnp.concatenate([jnp.where(hm, q, jnp.zeros_like(q)) for hm in masks], axis=0)
        s1 = _dot_nt(qs, kw) + bias_ref[r - r0, 0].reshape(NAT_HG * GRID_W, NAT_KEYS)
        s2 = _dot_nt(qs, kc)
        m = jnp.maximum(jnp.max(s1, axis=-1, keepdims=True), jnp.max(s2, axis=-1, keepdims=True))
        p1 = jnp.exp2(s1 - m)
        p2 = jnp.exp2(s2 - m)
        l = jnp.sum(p1, axis=-1, keepdims=True) + jnp.sum(p2, axis=-1, keepdims=True)
        o = (_dot(p1.astype(BF16), vw) + _dot(p2.astype(BF16), vc)) / l
        acc = jnp.zeros((GRID_W, NAT_HG * C_DH), F32)
        for h, hm in enumerate(masks):
            acc = jnp.where(hm, o[h * GRID_W:(h + 1) * GRID_W], acc)
        o_ref[0, j * GRID_W:(j + 1) * GRID_W, :] = acc.astype(BF16)


def _nat(q, k, v, rpb, lc):
    bsz, tt, d = q.shape
    l = tt - lc
    rows = l // GRID_W
    assert rows >= WIN_R and rows % NAT_RB == 0 and lc % (NAT_RB * GRID_W) == 0
    hw = NAT_HG * C_DH
    ng = d // hw
    bias = (_nat_bias(rpb) * LOG2E).reshape(WIN_R, ng, NAT_HG, GRID_W, NAT_KEYS)
    kern = functools.partial(_nat_kernel, lc=lc, rows=rows)
    qb = NAT_RB * GRID_W
    kv = pl.BlockSpec((1, tt, hw), lambda b, g, r: (b, 0, g))
    return pl.pallas_call(
        kern,
        out_shape=jax.ShapeDtypeStruct((bsz, l, d), BF16),
        grid=(bsz, ng, rows // NAT_RB),
        in_specs=[
            pl.BlockSpec((1, qb, hw), lambda b, g, r: (b, lc // qb + r, g)),
            kv, kv,
            pl.BlockSpec((WIN_R, 1, NAT_HG, GRID_W, NAT_KEYS), lambda b, g, r: (0, g, 0, 0, 0)),
        ],
        out_specs=pl.BlockSpec((1, qb, hw), lambda b, g, r: (b, r, g)),
        compiler_params=_params(("parallel", "parallel", "arbitrary")),
        name="nat",
    )(q, k, v, bias)


def _mix1_kernel(o_ref, x_ref, mod_ref, wo_ref, ng_ref, wr_ref, x_out, f_out, w_out, rank_out, cnt_out,
                 carry_ref, *, group):
    t = pl.program_id(1)
    m = mod_ref[0]
    x1 = x_ref[0] + m[2:3] * _dot(o_ref[0], wo_ref[...])
    x_out[0] = x1
    f = _rms(x1) * ng_ref[...] * (1.0 + m[4:5]) + m[3:4]
    f_out[0] = f.astype(BF16)
    logits = _dot(f, wr_ref[...], precision=HIGHEST)
    lane = lax.broadcasted_iota(jnp.int32, logits.shape, 1)
    logits = jnp.where(lane < N_EXPERTS, logits, -jnp.inf)
    m1 = jnp.max(logits, axis=-1, keepdims=True)
    i1 = jnp.min(jnp.where(logits == m1, lane, LANES), axis=-1, keepdims=True)
    rest = jnp.where(lane == i1, -jnp.inf, logits)
    m2 = jnp.max(rest, axis=-1, keepdims=True)
    i2 = jnp.min(jnp.where(rest == m2, lane, LANES), axis=-1, keepdims=True)
    e2 = jnp.exp(m2 - m1)
    g1 = 1.0 / (1.0 + e2)
    w_out[0] = jnp.where(lane == i1, g1, 0.0) + jnp.where(lane == i2, e2 * g1, 0.0)

    @pl.when(t % group == 0)
    def _():
        carry_ref[...] = jnp.zeros_like(carry_ref)

    sel = (lane == i1) | (lane == i2)
    tm = logits.shape[0]
    tri = lax.broadcasted_iota(jnp.int32, (tm, tm), 0) >= lax.broadcasted_iota(jnp.int32, (tm, tm), 1)
    csum = _dot(_onehot(tri), _onehot(sel))
    carry = carry_ref[...]
    rank_out[0] = jnp.where(sel, carry + csum - 1.0, -1.0)
    carry = carry + csum[tm - 1:tm, :]
    carry_ref[...] = carry
    cnt_out[0, 0] = carry


def _mix1(o, xa, mod, w_out, ng, w_router, lc, tmx):
    bsz, tt, d = xa.shape
    l = tt - lc
    nct = lc // TM
    group = tmx // TM
    wr = jnp.concatenate([w_router, jnp.zeros((d, LANES - N_EXPERTS), F32)], axis=1)
    tok = lambda w: pl.BlockSpec((1, TM, w), lambda b, t: (b, t, 0))
    return pl.pallas_call(
        functools.partial(_mix1_kernel, group=group),
        out_shape=(jax.ShapeDtypeStruct((bsz, l, d), F32), jax.ShapeDtypeStruct((bsz, l, d), BF16),
                   jax.ShapeDtypeStruct((bsz, l, LANES), F32), jax.ShapeDtypeStruct((bsz, l, LANES), F32),
                   jax.ShapeDtypeStruct((bsz, l // tmx, 1, LANES), F32)),
        grid=(bsz, l // TM),
        in_specs=[tok(d), pl.BlockSpec((1, TM, d), lambda b, t: (b, t + nct, 0)),
                  pl.BlockSpec((1, 6, d), lambda b, t: (b, 0, 0)),
                  _const_spec((d, d)), _const_spec((1, d)), _const_spec(wr.shape)],
        out_specs=(tok(d), tok(d), tok(LANES), tok(LANES),
                   pl.BlockSpec((1, 1, 1, LANES), lambda b, t: (b, t // group, 0, 0))),
        scratch_shapes=[pltpu.VMEM((1, LANES), F32)],
        compiler_params=_params(("parallel", "arbitrary")),
        name="mix1_router",
    )(o, xa, mod, w_out.astype(BF16), ng.reshape(1, d), wr)


MOE_TOKENS = (2048, 1024)
MOE_GATHER = 256
MOE_ROWS = 128
MOE_BIG = 4
MOE_FF = 512
MOE_COMBINE_ROWS = 512
MOE_VMEM_LIMIT = 60 * 1024 * 1024


def _moe_kernel(cnt_ref, f_ref, rank_t_ref, rank_c_ref, gate_ref, wg_ref, wu_ref, wd_ref, o_ref, xs_ref, y_ref):
    i, e, c = pl.program_id(0), pl.program_id(1), pl.program_id(2)
    n = cnt_ref[i * pl.num_programs(1) + e]
    n_rows = (n + MOE_ROWS - 1) // MOE_ROWS
    n_gather = (n + MOE_GATHER - 1) // MOE_GATHER
    tmx = f_ref.shape[0]

    @pl.when((e == 0) & (c == 0))
    def _():
        o_ref[...] = jnp.zeros_like(o_ref)

    @pl.when(c == 0)
    def _():
        rank_t = rank_t_ref[0]

        def gather(s, carry):
            base = pl.multiple_of(s * MOE_GATHER, MOE_GATHER)
            slot = lax.broadcasted_iota(jnp.int32, (MOE_GATHER, tmx), 0) + base
            sel = _onehot(rank_t == slot)
            xs_ref[pl.ds(base, MOE_GATHER), :] = _dot(sel, f_ref[...]).astype(BF16)
            y_ref[pl.ds(base, MOE_GATHER), :] = jnp.zeros((MOE_GATHER, y_ref.shape[1]), F32)
            return carry

        lax.fori_loop(0, n_gather, gather, 0)

    def expert(base, rows):
        x = xs_ref[pl.ds(base, rows), :]
        hid = _silu(_dot(x, wg_ref[0])) * _dot(x, wu_ref[0])
        y_ref[pl.ds(base, rows), :] += _dot(hid.astype(BF16), wd_ref[0])

    big = MOE_ROWS * MOE_BIG

    def big_block(s, carry):
        expert(pl.multiple_of(s * big, big), big)
        return carry

    lax.fori_loop(0, n_rows // MOE_BIG, big_block, 0)
    done = (n_rows // MOE_BIG) * MOE_BIG
    part = MOE_BIG // 2
    while part >= 1:
        take = ((n_rows - done) // part) > 0

        @pl.when(take)
        def _(done=done, part=part):
            expert(pl.multiple_of(done * MOE_ROWS, MOE_ROWS), part * MOE_ROWS)

        done = done + jnp.where(take, part, 0)
        part //= 2

    @pl.when(c == pl.num_programs(2) - 1)
    def _():
        def combine(s, carry):
            base = pl.multiple_of(s * MOE_GATHER, MOE_GATHER)
            ys = y_ref[pl.ds(base, MOE_GATHER), :].astype(BF16)
            slot = lax.broadcasted_iota(jnp.int32, (MOE_COMBINE_ROWS, MOE_GATHER), 1) + base
            for q in range(tmx // MOE_COMBINE_ROWS):
                rows = slice(q * MOE_COMBINE_ROWS, (q + 1) * MOE_COMBINE_ROWS)
                sel = _onehot(rank_c_ref[0, rows, :] == slot)
                o_ref[rows, :] += gate_ref[0, rows, :] * _dot(sel, ys)
            return carry

        lax.fori_loop(0, n_gather, combine, 0)


def _moe(f, gates, rank, cnt, wg, wu, wd, tmx):
    bsz, l, d = f.shape
    ne, _, dff = wg.shape
    n = bsz * l
    nt = n // tmx
    per_expert = lambda a: jnp.transpose(a.reshape(n, LANES)[:, :ne])
    rank_e = per_expert(rank).astype(jnp.int32)
    counts = cnt.reshape(nt, LANES)[:, :ne].astype(jnp.int32).reshape(nt * ne)
    grid_spec = pltpu.PrefetchScalarGridSpec(
        num_scalar_prefetch=1,
        grid=(nt, ne, dff // MOE_FF),
        in_specs=[
            pl.BlockSpec((tmx, d), lambda i, e, c, cnt: (i, 0)),
            pl.BlockSpec((1, 1, tmx), lambda i, e, c, cnt: (e, 0, i)),
            pl.BlockSpec((1, tmx, 1), lambda i, e, c, cnt: (e, i, 0)),
            pl.BlockSpec((1, tmx, 1), lambda i, e, c, cnt: (e, i, 0)),
            pl.BlockSpec((1, d, MOE_FF), lambda i, e, c, cnt: (e, 0, c)),
            pl.BlockSpec((1, d, MOE_FF), lambda i, e, c, cnt: (e, 0, c)),
            pl.BlockSpec((1, MOE_FF, d), lambda i, e, c, cnt: (e, c, 0)),
        ],
        out_specs=pl.BlockSpec((tmx, d), lambda i, e, c, cnt: (i, 0)),
        scratch_shapes=[pltpu.VMEM((tmx, d), BF16), pltpu.VMEM((tmx, d), F32)],
    )
    return pl.pallas_call(
        _moe_kernel,
        out_shape=jax.ShapeDtypeStruct((n, d), F32),
        grid_spec=grid_spec,
        compiler_params=pltpu.CompilerParams(dimension_semantics=("parallel", "arbitrary", "arbitrary"),
                                             vmem_limit_bytes=MOE_VMEM_LIMIT),
        name="moe_ffn",
    )(counts, f.reshape(n, d), rank_e.reshape(ne, 1, n), rank_e.reshape(ne, n, 1),
      per_expert(gates).reshape(ne, n, 1), wg.astype(BF16), wu.astype(BF16), wd.astype(BF16)).reshape(bsz, l, d)


def _final_kernel(x_ref, y_ref, mod_ref, fg_ref, o_ref):
    x2 = x_ref[0] + mod_ref[0][5:6] * y_ref[0]
    o_ref[0] = _rms(x2) * fg_ref[...]


def _final(x, y, mod, fg):
    bsz, l, d = x.shape
    tok = pl.BlockSpec((1, TM, d), lambda b, t: (b, t, 0))
    return pl.pallas_call(
        _final_kernel,
        out_shape=jax.ShapeDtypeStruct((bsz, l, d), F32),
        grid=(bsz, l // TM),
        in_specs=[tok, tok, pl.BlockSpec((1, 6, d), lambda b, t: (b, 0, 0)), _const_spec((1, d))],
        out_specs=tok,
        compiler_params=_params(("parallel", "parallel")),
        name="final_norm",
    )(x, y, mod, fg.reshape(1, d))


def kernel(x, c, ctx, c_ctx, ada_w, ada_b, norm_g, ev_w_in, ev_conv_w, ev_a_log, ev_dt_bias, ev_gdn_norm_g, ev_q_norm_g, ev_w_uq, ev_kv_norm_g, ev_w_ukv, ev_w_out, ev_ffn_wg, ev_ffn_wu, ev_ffn_wd, od_w_qkv, od_rpb, od_w_out, od_router, od_exp_wg, od_exp_wu, od_exp_wd, final_norm_g):
    bsz, l, d = x.shape
    lc = ctx.shape[1]
    assert ada_w.shape[0] == 2 and bsz < MOD_ROWS and lc % TM == 0 and l % TM == 0
    xa = jnp.concatenate([ctx, x], axis=1)

    cs = jnp.concatenate([c, c_ctx[None], jnp.zeros((MOD_ROWS - bsz - 1, d), F32)], axis=0)
    mods = _ada_mod(cs, ada_w, ada_b).reshape(2, MOD_ROWS, 6, d)

    win, wq, wqr, wk, wv = _proj0_weights(ev_w_in[0], ev_w_uq[0], ev_w_ukv[0])
    qkv_pre, z, ab, q, k, v = _proj0(xa, mods[0], norm_g[0, 0],
                                     (win, ev_q_norm_g[0], wq, wqr, ev_kv_norm_g[0], wk, wv),
                                     _rope_tables(lc, l), lc)
    o_f, o_b = _gdn(qkv_pre, ab, ev_conv_w[0], ev_a_log[0], ev_dt_bias[0], lc)
    yb = _mla_attention(q, k, v, lc)
    xa = _mix0(o_f, o_b, z, yb, xa, mods[0], ev_gdn_norm_g[0], ev_w_out[0], norm_g[0, 1],
               ev_ffn_wg[0], ev_ffn_wu[0], ev_ffn_wd[0], lc)

    q1, k1, v1 = _proj1(xa, mods[1], norm_g[1, 0], od_w_qkv[0], lc)
    o1 = _nat(q1, k1, v1, od_rpb[0], lc)
    tmx = _pick_chunk(l, MOE_TOKENS)
    x1, f1, gates, rank, cnt = _mix1(o1, xa, mods[1], od_w_out[0], norm_g[1, 1], od_router[0], lc, tmx)
    y = _moe(f1, gates, rank, cnt, od_exp_wg[0], od_exp_wu[0], od_exp_wd[0], tmx)
    return _final(x1, y, mods[1], final_norm_g)
```

```python
import functools

import numpy as np
import jax
import jax.numpy as jnp
from jax import lax
from jax.experimental import pallas as pl
from jax.experimental.pallas import tpu as pltpu

F32 = jnp.float32
BF16 = jnp.bfloat16
HIGHEST = lax.Precision.HIGHEST

GRID_W = 64
NORM_EPS = 1e-6
A_HEADS, A_DK, A_DV = 4, 128, 128
GDN_CHUNK_LOG2 = 7
GDN_CHUNK = 1 << GDN_CHUNK_LOG2
GDN_BASE_LOG2 = 3
B_HEADS, B_Q_RANK, B_KV_RANK, B_NOPE, B_ROPE, B_DV = 8, 384, 256, 64, 32, 64
MLA_VROWS = B_DV + 16
ROPE_THETA = 10000.0
C_HEADS, C_DH = 16, 64
WIN_R, WIN_C = 8, 16
N_EXPERTS = 8
LANES = 128
MOD_ROWS = 8
NEG = -1e30
LOG2E = 1.4426950408889634
VMEM_LIMIT = 56 * 1024 * 1024

TM = 256


def _params(sem):
    return pltpu.CompilerParams(dimension_semantics=sem, vmem_limit_bytes=VMEM_LIMIT)


def _const_spec(shape):
    nd = len(shape)
    return pl.BlockSpec(shape, lambda *_: (0,) * nd, pipeline_mode=pl.Buffered(1))


def _rms(x):
    return x * lax.rsqrt(jnp.mean(x * x, axis=-1, keepdims=True) + NORM_EPS)


def _silu(x):
    return x * jax.nn.sigmoid(x)


def _onehot(mask):
    return jnp.where(mask, 1.0, 0.0).astype(BF16)


def _dot(a, b, **kw):
    return jnp.dot(a, b, preferred_element_type=F32, **kw)


def _dot_nt(a, b, **kw):
    return lax.dot_general(a, b, (((1,), (1,)), ((), ())), preferred_element_type=F32, **kw)


def _ada_kernel(s_ref, w_ref, b_ref, o_ref):
    s = _silu(s_ref[...])
    o_ref[0] = _dot(s, w_ref[0], precision=HIGHEST) + b_ref[0]


def _ada_mod(cs, ada_w, ada_b):
    depth, d, d6 = ada_w.shape
    tn = d6 // 4
    return pl.pallas_call(
        _ada_kernel,
        out_shape=jax.ShapeDtypeStruct((depth, MOD_ROWS, d6), F32),
        grid=(depth, d6 // tn),
        in_specs=[
            pl.BlockSpec((MOD_ROWS, d), lambda i, j: (0, 0)),
            pl.BlockSpec((1, d, tn), lambda i, j: (i, 0, j)),
            pl.BlockSpec((1, 1, tn), lambda i, j: (i, 0, j)),
        ],
        out_specs=pl.BlockSpec((1, MOD_ROWS, tn), lambda i, j: (i, 0, j)),
        compiler_params=_params(("parallel", "parallel")),
        name="ada_mod",
    )(cs, ada_w, ada_b.reshape(depth, 1, d6))


N_QKVZ = 4 * A_HEADS * A_DK
IN_OFF_AB = N_QKVZ
IN_OFF_CQ = IN_OFF_AB + LANES
IN_OFF_CKV = IN_OFF_CQ + B_Q_RANK
IN_OFF_KR = IN_OFF_CKV + B_KV_RANK
IN_COLS = IN_OFF_KR + LANES
HW = B_HEADS * LANES


def _proj0_kernel(x_ref, mod_ref, ng_ref, win_ref, qg_ref, wq_ref, wqr_ref, kvg_ref, wk_ref, wv_ref,
                  vone_ref, cq_ref, sq_ref, ck_ref, sk_ref,
                  qkv_ref, z_ref, ab_ref, q_ref, k_ref, vt_ref):
    x = x_ref[0]
    m = mod_ref[0]
    h = _rms(x) * ng_ref[...] * (1.0 + m[1:2]) + m[0:1]
    p = _dot(h.astype(BF16), win_ref[...])
    qkv_ref[0] = p[:, :3 * A_HEADS * A_DK]
    z_ref[0] = p[:, 3 * A_HEADS * A_DK:N_QKVZ]
    ab_ref[0] = p[:, IN_OFF_AB:IN_OFF_CQ]
    nq = (_rms(p[:, IN_OFF_CQ:IN_OFF_CKV]) * qg_ref[...]).astype(BF16)
    nkv = (_rms(p[:, IN_OFF_CKV:IN_OFF_KR]) * kvg_ref[...]).astype(BF16)
    krp = p[:, IN_OFF_KR:IN_COLS]
    qa = _dot(nq, wq_ref[...])
    qb = _dot(nq, wqr_ref[...])
    kn = _dot(nkv, wk_ref[...])
    vv = _dot(nkv, wv_ref[...]) + vone_ref[...]
    cq, sq, ck, sk = cq_ref[...], sq_ref[...], ck_ref[...], sk_ref[...]
    kr = pltpu.roll(krp, B_NOPE, axis=1) * ck + pltpu.roll(krp, B_NOPE - B_ROPE, axis=1) * sk
    for hh in range(B_HEADS):
        sl = slice(hh * LANES, (hh + 1) * LANES)
        q_ref[0, hh] = (qa[:, sl] * cq + qb[:, sl] * sq).astype(BF16)
        k_ref[0, hh] = (kn[:, sl] + kr).astype(BF16)
        vt_ref[0, hh] = vv[:, sl].T[:MLA_VROWS].astype(BF16)


def _rot_cols(w):
    q = B_ROPE // 4
    return jnp.concatenate([-w[:, q:2 * q], w[:, :q], -w[:, 3 * q:], w[:, 2 * q:3 * q]], axis=1)


def _proj0_weights(w_in, w_uq, w_ukv):
    d = w_in.shape[0]
    offs = np.cumsum([0, 512, 512, 512, 512, 8, 8, B_Q_RANK, B_KV_RANK, B_ROPE])
    zeros = lambda n: jnp.zeros((d, n), F32)
    kr = w_in[:, offs[8]:offs[9]]
    win = jnp.concatenate([
        w_in[:, :offs[4]],
        w_in[:, offs[4]:offs[6]], zeros(LANES - 16),
        w_in[:, offs[6]:offs[7]],
        w_in[:, offs[7]:offs[8]],
        kr, _rot_cols(kr), zeros(LANES - 2 * B_ROPE),
    ], axis=1).astype(BF16)
    dq = B_NOPE + B_ROPE
    wq3 = w_uq.reshape(B_Q_RANK, B_HEADS, dq)
    zq = jnp.zeros((B_Q_RANK, B_HEADS, LANES - dq), F32)
    wq = jnp.concatenate([wq3, zq], axis=2).reshape(B_Q_RANK, HW).astype(BF16)
    rot = jnp.stack([_rot_cols(wq3[:, hh, B_NOPE:]) for hh in range(B_HEADS)], axis=1)
    wqr = jnp.concatenate([jnp.zeros((B_Q_RANK, B_HEADS, B_NOPE), F32), rot, zq], axis=2)
    wqr = wqr.reshape(B_Q_RANK, HW).astype(BF16)
    wkv3 = w_ukv.reshape(B_KV_RANK, B_HEADS, B_NOPE + B_DV)
    zk = jnp.zeros((B_KV_RANK, B_HEADS, LANES - B_NOPE), F32)
    wk = jnp.concatenate([wkv3[:, :, :B_NOPE], zk], axis=2).reshape(B_KV_RANK, HW).astype(BF16)
    vpart = wkv3[:, :, B_NOPE:]
    wv = jnp.concatenate([vpart, jnp.zeros_like(vpart)], axis=2).reshape(B_KV_RANK, HW).astype(BF16)
    return win, wq, wqr, wk, wv


def _value_ones():
    vone = np.zeros((1, HW), np.float32)
    for hh in range(B_HEADS):
        vone[0, hh * LANES + B_DV:hh * LANES + MLA_VROWS] = 1.0
    return jnp.asarray(vone)


def _rope_tables(lc, l):
    t = jnp.arange(l)
    row = (t // GRID_W).astype(F32)
    col = (t % GRID_W).astype(F32)
    half = B_ROPE // 2
    inv = ROPE_THETA ** (-jnp.arange(0, half, 2, dtype=F32) / half)
    ar = row[:, None] * inv[None, :]
    ac = col[:, None] * inv[None, :]
    ang = jnp.concatenate([ar, ar, ac, ac], axis=-1)
    cos = jnp.concatenate([jnp.ones((lc, B_ROPE), F32), jnp.cos(ang)], axis=0)
    sin = jnp.concatenate([jnp.zeros((lc, B_ROPE), F32), jnp.sin(ang)], axis=0)
    tt = lc + l
    scale = (B_NOPE + B_ROPE) ** -0.5 * LOG2E
    pad =jnp.zeros((tt, LANES - B_NOPE - B_ROPE), F32)
    z64 = jnp.zeros((tt, B_NOPE), F32)
    cq = jnp.concatenate([jnp.full((tt, B_NOPE), scale, F32), scale * cos, pad], axis=1)
    sq = jnp.concatenate([z64, scale * sin, pad], axis=1)
    ck = jnp.concatenate([z64, cos, pad], axis=1)
    sk = jnp.concatenate([z64, sin, pad], axis=1)
    return cq, sq, ck, sk


def _mod_row(b, t, n_ctx_tiles, bsz):
    return jnp.where(t < n_ctx_tiles, bsz, b)


def _proj0(xa, mod, ng, weights, tables, lc):
    bsz, tt, d = xa.shape
    win, qg, wq, wqr, kvg, wk, wv = weights
    vone = _value_ones()
    nct = lc // TM
    tok = lambda w: pl.BlockSpec((1, TM, w), lambda b, t: (b, t, 0))
    head = pl.BlockSpec((1, B_HEADS, TM, LANES), lambda b, t: (b, 0, t, 0))
    head_t = pl.BlockSpec((1, B_HEADS, MLA_VROWS, TM), lambda b, t: (b, 0, 0, t))
    tab = pl.BlockSpec((TM, LANES), lambda b, t: (t, 0))
    hshape = jax.ShapeDtypeStruct((bsz, B_HEADS, tt, LANES), BF16)
    return pl.pallas_call(
        _proj0_kernel,
        out_shape=(
            jax.ShapeDtypeStruct((bsz, tt, 3 * A_HEADS * A_DK), F32),
            jax.ShapeDtypeStruct((bsz, tt, A_HEADS * A_DV), F32),
            jax.ShapeDtypeStruct((bsz, tt, LANES), F32),
            hshape, hshape, jax.ShapeDtypeStruct((bsz, B_HEADS, MLA_VROWS, tt), BF16),
        ),
        grid=(bsz, tt // TM),
        in_specs=[
            tok(d),
            pl.BlockSpec((1, 6, d), lambda b, t: (_mod_row(b, t, nct, bsz), 0, 0)),
            _const_spec((1, d)), _const_spec(win.shape),
            _const_spec((1, B_Q_RANK)), _const_spec(wq.shape), _const_spec(wqr.shape),
            _const_spec((1, B_KV_RANK)), _const_spec(wk.shape), _const_spec(wv.shape),
            _const_spec(vone.shape),
            tab, tab, tab, tab,
        ],
        out_specs=(tok(3 * A_HEADS * A_DK), tok(A_HEADS * A_DV), tok(LANES), head, head, head_t),
        compiler_params=_params(("parallel", "parallel")),
        name="proj0",
    )(xa, mod, ng.reshape(1, d), win, qg.reshape(1, -1), wq, wqr, kvg.reshape(1, -1), wk, wv, vone,
      *tables)


A_CONV = 5
CONV_HALO = 8


def _split_bf16(a):
    hi = a.astype(BF16)
    return hi, (a - hi.astype(F32)).astype(BF16)


def _dot3(a, b):
    ah, al = _split_bf16(a)
    bh, bl = _split_bf16(b)
    return _dot(ah, bh) + (_dot(ah, bl) + _dot(al, bh))


def _bdot(a, b):
    return lax.dot_general(a.astype(BF16), b.astype(BF16), (((2,), (1,)), ((0,), (0,))),
                           preferred_element_type=F32)


def _gdn_pre_kernel(x_ref, prev_ref, next_ref, ab_ref, cw_ref, shift_ref, alog_ref, dtb_ref,
                    u_ref, wq_ref, kgt_ref, qk_ref, gl_ref, *, ncc, nc):
    c = GDN_CHUNK
    i = pl.program_id(1)
    prev_ok = ((i != 0) & (i != ncc)).astype(F32)
    next_ok = ((i != ncc - 1) & (i != nc - 1)).astype(F32)
    xe = jnp.concatenate([prev_ref[0] * prev_ok, x_ref[0], next_ref[0] * next_ok], axis=0)
    cw = cw_ref[...]
    xw = jnp.concatenate([(xe * cw[j:j + 1]).astype(BF16) for j in range(A_CONV)], axis=0)
    y = _silu(_dot(shift_ref[...], xw))
    nq = A_HEADS * A_DK

    def l2n(t):
        return t * lax.rsqrt(jnp.sum(t * t, axis=-1, keepdims=True) + 1e-6)

    ab = ab_ref[0]
    g_all = -jnp.exp(alog_ref[...]) * (jnp.maximum(ab + dtb_ref[...], 0.0)
                                       + jnp.log1p(jnp.exp(-jnp.abs(ab + dtb_ref[...]))))
    beta_all = jax.nn.sigmoid(ab)
    g_all_t = g_all.T
    g_tot = jnp.sum(g_all, axis=0, keepdims=True)
    row = lax.broadcasted_iota(jnp.int32, (c, c), 0)
    col = lax.broadcasted_iota(jnp.int32, (c, c), 1)
    eye = (row == col).astype(F32)
    ns, rhs, qgs = [], [], []
    heads = []
    for h in range(A_HEADS):
        sl = slice(h * A_DK, (h + 1) * A_DK)
        kh = l2n(y[:, nq + h * A_DK:nq + (h + 1) * A_DK])
        heads.append((l2n(y[:, sl]) * (A_DK ** -0.5), kh, y[:, 2 * nq + h * A_DV:2 * nq + (h + 1) * A_DV],
                      kh.astype(BF16)))
    for d in range(2):
        incl = (row >= col) if d == 0 else (row <= col)
        strict = (row > col) if d == 0 else (row < col)
        m_incl = incl.astype(F32)
        gc_col = _dot(m_incl, g_all, precision=HIGHEST)
        gc_row = _dot_nt(g_all_t, m_incl, precision=HIGHEST)
        gl_rows = []
        for h in range(A_HEADS):
            qh, kh, vh, khb = heads[h]
            ln = d * A_HEADS + h
            beta = beta_all[:, 2 * A_HEADS + ln:2 * A_HEADS + ln + 1]
            gcc = gc_col[:, ln:ln + 1]
            gcr = gc_row[ln:ln + 1, :]
            gtot = g_tot[:, ln:ln + 1]
            decay = jnp.where(incl, jnp.exp(jnp.where(incl, gcc - gcr, 0.0)), 0.0)
            n = -jnp.where(strict, beta * _dot_nt(khb, khb) * decay, 0.0)
            eg = jnp.exp(gcc)
            ns.append(n)
            rhs.append(jnp.concatenate([vh * beta, kh * (beta * eg)], axis=1))
            qgs.append(qh * eg)
            kgt_ref[0, d, 0, h] = (kh * jnp.exp(gtot - gcc)).T.astype(BF16)
            qk_ref[0, d, 0, h] = jnp.where(incl, _dot_nt(qh.astype(BF16), khb) * decay, 0.0).astype(BF16)
            gl_rows.append(jnp.broadcast_to(jnp.exp(gtot), (1, LANES)))
        gl_ref[0, d, 0] = jnp.concatenate(gl_rows + gl_rows, axis=0)
    n = jnp.stack(ns, axis=0)
    same = lambda s: lax.shift_right_logical(row, s) == lax.shift_right_logical(col, s)
    nd = jnp.where(same(GDN_BASE_LOG2), n, 0.0)
    x = eye + nd
    p = _bdot(nd, nd)
    z = _bdot(p, jnp.concatenate([p, x], axis=2))
    x = x + z[:, :, c:]
    x = x + _bdot(z[:, :, :c], x)
    for s in range(GDN_BASE_LOG2, GDN_CHUNK_LOG2):
        nl = jnp.where(same(s + 1) & jnp.logical_not(same(s)), n, 0.0)
        x = x + _bdot(_bdot(x, nl), x)
    y = _bdot(x, jnp.stack(rhs, axis=0))
    for d in range(2):
        for h in range(A_HEADS):
            i = d * A_HEADS + h
            u_ref[0, d, 0, h] = y[i, :, :A_DV]
            wq_ref[0, d, 0, h] = jnp.concatenate([y[i, :, A_DV:], qgs[i]], axis=0).astype(BF16)


def _gdn_scan_kernel(*refs, bsz):
    ins, (of_ref, ob_ref, s_ref) = refs[:10], refs[10:]

    @pl.when(pl.program_id(0) == 0)
    def _():
        s_ref[...] = jnp.zeros_like(s_ref)

    c = GDN_CHUNK
    n = bsz * A_HEADS
    bdot = lambda a, b: lax.dot_general(a, b, (((2,), (1,)), ((0,), (0,))), preferred_element_type=F32)
    for d, o_ref in ((0, of_ref), (1, ob_ref)):
        u_ref, wq_ref, kgt_ref, qk_ref, gl_ref = ins[5 * d:5 * d + 5]
        flat = lambda ref: ref[:, 0, 0].reshape(n, *ref.shape[-2:])
        s = s_ref[:, d].reshape(n, A_DK, A_DV)
        ws = bdot(flat(wq_ref), s.astype(BF16))
        v_new = (flat(u_ref) - ws[:, :c]).astype(BF16)
        o = ws[:, c:] + bdot(flat(qk_ref), v_new)
        gl = gl_ref[:, 0, 0, :A_HEADS, :].reshape(n, 1, LANES)
        s_ref[:, d] = (s * gl + bdot(flat(kgt_ref), v_new)).reshape(bsz, A_HEADS, A_DK, A_DV)
        for b in range(bsz):
            for h in range(A_HEADS):
                o_ref[b, :, h * A_DV:(h + 1) * A_DV] = o[b * A_HEADS + h]


def _gdn(qkv_pre, ab, conv_w, a_log, dt_bias, lc):
    bsz, tt, wqkv = qkv_pre.shape
    c = GDN_CHUNK
    nc, ncc = tt // c, lc // c
    hb = c // CONV_HALO
    per = lambda r, w: pl.BlockSpec((1, 2, 1, A_HEADS, r, w), lambda b, i: (b, 0, i, 0, 0, 0))
    shp = lambda r, w, dt: jax.ShapeDtypeStruct((bsz, 2, nc, A_HEADS, r, w), dt)
    lane_row = lambda t: jnp.concatenate([t.reshape(1, -1), jnp.zeros((1, LANES - t.size), F32)], axis=1)
    cw = jnp.concatenate([conv_w, jnp.zeros((CONV_HALO - A_CONV, wqkv), F32)], axis=0)
    rows = c + 2 * CONV_HALO
    shift = np.zeros((c, A_CONV * rows), np.float32)
    for j in range(A_CONV):
        shift[np.arange(c), j * rows + np.arange(c) + CONV_HALO - A_CONV // 2 + j] = 1.0
    shift = jnp.asarray(shift, BF16)
    pre = pl.pallas_call(
        functools.partial(_gdn_pre_kernel, ncc=ncc, nc=nc),
        out_shape=(shp(c, A_DV, F32), shp(2 * c, A_DK, BF16), shp(A_DK, c, BF16), shp(c, c, BF16),
                   jax.ShapeDtypeStruct((bsz, 2, nc, 2 * A_HEADS, LANES), F32)),
        grid=(bsz, nc),
        in_specs=[pl.BlockSpec((1, c, wqkv), lambda b, i: (b, i, 0)),
                  pl.BlockSpec((1, CONV_HALO, wqkv), lambda b, i: (b, jnp.maximum(i * hb - 1, 0), 0)),
                  pl.BlockSpec((1, CONV_HALO, wqkv), lambda b, i: (b, jnp.minimum((i + 1) * hb, nc * hb - 1), 0)),
                  pl.BlockSpec((1, c, LANES), lambda b, i: (b, i, 0)),
                  _const_spec(cw.shape), _const_spec(shift.shape), _const_spec((1, LANES)),
                  _const_spec((1, LANES))],
        out_specs=(per(c, A_DV), per(2 * c, A_DK), per(A_DK, c), per(c, c),
                   pl.BlockSpec((1, 2, 1, 2 * A_HEADS, LANES), lambda b, i: (b, 0, i, 0, 0))),
        compiler_params=_params(("parallel", "parallel")),
        name="gdn_pre",
    )(qkv_pre, qkv_pre, qkv_pre, ab, cw, shift, lane_row(a_log), lane_row(dt_bias))

    def chunk(d, s):
        return s if d == 0 else jnp.where(s < ncc, ncc - 1 - s, nc - 1 - (s - ncc))

    in_specs = []
    for d in range(2):
        for arr in pre[:4]:
            r, w = arr.shape[-2:]
            in_specs.append(pl.BlockSpec((bsz, 1, 1, A_HEADS, r, w),
                                         lambda s, d=d: (0, d, chunk(d, s), 0, 0, 0)))
        in_specs.append(pl.BlockSpec((bsz, 1, 1, 2 * A_HEADS, LANES), lambda s, d=d: (0, d, chunk(d, s), 0, 0)))
    o_shape = jax.ShapeDtypeStruct((bsz, tt, A_HEADS * A_DV), F32)
    return pl.pallas_call(
        functools.partial(_gdn_scan_kernel, bsz=bsz),
        out_shape=(o_shape, o_shape),
        grid=(nc,),
        in_specs=in_specs,
        out_specs=tuple(pl.BlockSpec((bsz, c, A_HEADS * A_DV), lambda s, d=d: (0, chunk(d, s), 0))
                        for d in range(2)),
        scratch_shapes=[pltpu.VMEM((bsz, 2, A_HEADS, A_DK, A_DV), F32)],
        compiler_params=_params(("arbitrary",)),
        name="gdn_scan",
    )(*pre, *pre)


def _pick_chunk(n, options):
    for o in options:
        if n % o == 0:
            return o
    raise ValueError(f"no chunk size in {options} divides {n}")


def _mla_kernel(q_ref, k_ref, vt_ref, o_ref, *, lc, tq, tk, nk):
    t = pl.program_id(2)
    g = q_ref.shape[1]
    bnt = lambda a, b: lax.dot_general(a, b, (((2,), (2,)), ((0,), (0,))), preferred_element_type=F32)
    bnn = lambda a, b: lax.dot_general(a, b, (((2,), (1,)), ((0,), (0,))), preferred_element_type=F32)

    def update(q, m, acc, k_chunk, vt_chunk):
        s = bnt(k_chunk, q)
        m_new = jnp.maximum(m, jnp.max(s, axis=1, keepdims=True))
        p = jnp.exp2(s - m_new).astype(BF16)
        return m_new, jnp.exp2(m - m_new) * acc + bnn(vt_chunk, p)

    def finish(acc):
        parts = [acc[h, :B_DV] / acc[h, B_DV:B_DV + 1] for h in range(g)]
        o_ref[0] = jnp.concatenate(parts, axis=0).T

    init = (jnp.full((g, 1, tq), -jnp.inf, F32), jnp.zeros((g, MLA_VROWS, tq), F32))

    @pl.when(t < lc // tq)
    def _():
        finish(update(q_ref[0], *init, k_ref[0, :, :lc, :], vt_ref[0, :, :, :lc])[1])

    @pl.when(t >= lc // tq)
    def _():
        q = q_ref[0]

        def body(j, carry):
            off = pl.multiple_of(j * tk, tk)
            return update(q, *carry, k_ref[0, :, pl.ds(off, tk), :], vt_ref[0, :, :, pl.ds(off, tk)])

        finish(lax.fori_loop(0, nk, body, init)[1])


MLA_G = 8


def _mla_attention(q, k, vt, lc):
    bsz, nh, tt, _ = q.shape
    tq = TM
    tk = _pick_chunk(tt, (768, 512, 256))
    g = MLA_G
    kern = functools.partial(_mla_kernel, lc=lc, tq=tq, tk=tk, nk=tt // tk)
    return pl.pallas_call(
        kern,
        out_shape=jax.ShapeDtypeStruct((bsz, tt, nh * B_DV), F32),
        grid=(bsz, nh // g, tt // tq),
        in_specs=[pl.BlockSpec((1, g, tq, LANES), lambda b, hp, t: (b, hp, t, 0)),
                  pl.BlockSpec((1, g, tt, LANES), lambda b, hp, t: (b, hp, 0, 0), pipeline_mode=pl.Buffered(1)),
                  pl.BlockSpec((1, g, MLA_VROWS, tt), lambda b, hp, t: (b, hp, 0, 0),
                               pipeline_mode=pl.Buffered(1))],
        out_specs=pl.BlockSpec((1, tq, g * B_DV), lambda b, hp, t: (b, t, hp)),
        compiler_params=_params(("parallel", "parallel", "parallel")),
        name="mla_attention",
    )(q, k, vt)


def _mix0_kernel(of_ref, ob_ref, z_ref, yb_ref, x_ref, mod_ref, gg_ref, woa_ref, wob_ref, ng_ref,
                 wg_ref, wu_ref, wd_ref, o_ref):
    o = of_ref[0] + ob_ref[0]
    z = z_ref[0]
    gg = gg_ref[...]
    parts = []
    for h in range(A_HEADS):
        sl = slice(h * A_DV, (h + 1) * A_DV)
        parts.append((_rms(o[:, sl]) * gg * _silu(z[:, sl])).astype(BF16))
    ya = jnp.concatenate(parts, axis=1)
    y = _dot(ya, woa_ref[...]) + _dot(yb_ref[0].astype(BF16), wob_ref[...])
    m = mod_ref[0]
    x1 = x_ref[0] + m[2:3] * y
    f = (_rms(x1) * ng_ref[...] * (1.0 + m[4:5]) + m[3:4]).astype(BF16)
    hid = _silu(_dot(f, wg_ref[...])) * _dot(f, wu_ref[...])
    o_ref[0] = x1 + m[5:6] * _dot(hid.astype(BF16), wd_ref[...])


def _mix0(o_f, o_b, z, yb, xa, mod, gdn_g, w_out, ng, wg, wu, wd, lc):
    bsz, tt, d = xa.shape
    nct = lc // TM
    na = A_HEADS * A_DV
    woa, wob = w_out[:na].astype(BF16), w_out[na:].astype(BF16)
    wg, wu, wd = wg.astype(BF16), wu.astype(BF16), wd.astype(BF16)
    tok = lambda w: pl.BlockSpec((1, TM, w), lambda b, t: (b, t, 0))
    return pl.pallas_call(
        _mix0_kernel,
        out_shape=jax.ShapeDtypeStruct((bsz, tt, d), F32),
        grid=(bsz, tt // TM),
        in_specs=[
            tok(na), tok(na), tok(na), tok(B_HEADS * B_DV), tok(d),
            pl.BlockSpec((1, 6, d), lambda b, t: (_mod_row(b, t, nct, bsz), 0, 0)),
            _const_spec((1, A_DV)), _const_spec(woa.shape), _const_spec(wob.shape), _const_spec((1, d)),
            _const_spec(wg.shape), _const_spec(wu.shape), _const_spec(wd.shape),
        ],
        out_specs=tok(d),
        compiler_params=_params(("parallel", "parallel")),
        name="mix0_ffn",
    )(o_f, o_b, z, yb, xa, mod, gdn_g.reshape(1, -1), woa, wob, ng.reshape(1, d), wg, wu, wd)


def _proj1_kernel(x_ref, mod_ref, ng_ref, w_ref, q_ref, k_ref, v_ref):
    m = mod_ref[0]
    h = (_rms(x_ref[0]) * ng_ref[...] * (1.0 + m[1:2]) + m[0:1]).astype(BF16)
    p = _dot(h, w_ref[...])
    d = q_ref.shape[-1]
    q_ref[0] = (p[:, :d] * (C_DH ** -0.5 * LOG2E)).astype(BF16)
    k_ref[0] = p[:, d:2 * d].astype(BF16)
    v_ref[0] = p[:, 2 * d:].astype(BF16)


def _proj1(xa, mod, ng, w_qkv, lc):
    bsz, tt, d = xa.shape
    nct = lc // TM
    w = w_qkv.astype(BF16)
    tok = pl.BlockSpec((1, TM, d), lambda b, t: (b, t, 0))
    shp = jax.ShapeDtypeStruct((bsz, tt, d), BF16)
    return pl.pallas_call(
        _proj1_kernel,
        out_shape=(shp, shp, shp),
        grid=(bsz, tt // TM),
        in_specs=[tok, pl.BlockSpec((1, 6, d), lambda b, t: (_mod_row(b, t, nct, bsz), 0, 0)),
                  _const_spec((1, d)), _const_spec(w.shape)],
        out_specs=(tok, tok, tok),
        compiler_params=_params(("parallel", "parallel")),
        name="proj1",
    )(xa, mod, ng.reshape(1, d), w)


NAT_HG = 4
NAT_KEYS = WIN_R * GRID_W


def _nat_bias(rpb):
    qc = np.arange(GRID_W)[:, None]
    kc = np.arange(GRID_W)[None, :]
    c0 = np.clip(qc - WIN_C // 2, 0, GRID_W - WIN_C)
    inside = (kc >= c0) & (kc < c0 + WIN_C)
    dc = kc - qc + (WIN_C - 1)
    onehot = ((np.arange(2 * WIN_C - 1)[:, None, None] == dc[None]) & inside[None]).astype(np.float32)
    tab = jnp.einsum('hrd,dqk->hqrk', rpb, jnp.asarray(onehot), precision=HIGHEST)
    tab = tab + jnp.asarray(np.where(inside, 0.0, NEG).astype(np.float32))[None, :, None, :]
    tab = jnp.stack([tab[:, :, WIN_R - 1 - v:2 * WIN_R - 1 - v, :] for v in range(WIN_R)], axis=0)
    return tab.reshape(WIN_R, rpb.shape[0], GRID_W, NAT_KEYS)


NAT_RB = 4


def _nat_kernel(q_ref, k_ref, v_ref, bias_ref, o_ref, *, lc, rows):
    i = pl.program_id(2)
    kc = k_ref[0, :lc, :]
    vc = v_ref[0, :lc, :]
    lane = lax.broadcasted_iota(jnp.int32, (GRID_W, NAT_HG * C_DH), 1)
    masks = [(lane >= h * C_DH) & (lane < (h + 1) * C_DH) for h in range(NAT_HG)]
    for j in range(NAT_RB):
        r = i * NAT_RB + j
        r0 = jnp.clip(r - WIN_R // 2, 0, rows - WIN_R)
        start = pl.multiple_of(lc + r0 * GRID_W, GRID_W)
        kw = k_ref[0, pl.ds(start, NAT_KEYS), :]
        vw = v_ref[0, pl.ds(start, NAT_KEYS), :]
        q = q_ref[0, j * GRID_W:(j + 1) * GRID_W, :]
        qs = jnp.concatenate([jnp.where(hm, q, jnp.zeros_like(q)) for hm in masks], axis=0)
        s1 = _dot_nt(qs, kw) + bias_ref[r - r0, 0].reshape(NAT_HG * GRID_W, NAT_KEYS)
        s2 = _dot_nt(qs, kc)
        m = jnp.maximum(jnp.max(s1, axis=-1, keepdims=True), jnp.max(s2, axis=-1, keepdims=True))
        p1 = jnp.exp2(s1 - m)
        p2 = jnp.exp2(s2 - m)
        l = jnp.sum(p1, axis=-1, keepdims=True) + jnp.sum(p2, axis=-1, keepdims=True)
        o = (_dot(p1.astype(BF16), vw) + _dot(p2.astype(BF16), vc)) / l
        acc = jnp.zeros((GRID_W, NAT_HG * C_DH), F32)
        for h, hm in enumerate(masks):
            acc = jnp.where(hm, o[h * GRID_W:(h + 1) * GRID_W], acc)
        o_ref[0, j * GRID_W:(j + 1) * GRID_W, :] = acc.astype(BF16)


def _nat(q, k, v, rpb, lc):
    bsz, tt, d = q.shape
    l = tt - lc
    rows = l // GRID_W
    assert rows >= WIN_R and rows % NAT_RB == 0 and lc % (NAT_RB * GRID_W) == 0
    hw = NAT_HG * C_DH
    ng = d // hw
    bias = (_nat_bias(rpb) * LOG2E).reshape(WIN_R, ng, NAT_HG, GRID_W, NAT_KEYS)
    kern = functools.partial(_nat_kernel, lc=lc, rows=rows)
    qb = NAT_RB * GRID_W
    kv = pl.BlockSpec((1, tt, hw), lambda b, g, r: (b, 0, g))
    return pl.pallas_call(
        kern,
        out_shape=jax.ShapeDtypeStruct((bsz, l, d), BF16),
        grid=(bsz, ng, rows // NAT_RB),
        in_specs=[
            pl.BlockSpec((1, qb, hw), lambda b, g, r: (b, lc // qb + r, g)),
            kv, kv,
            pl.BlockSpec((WIN_R, 1, NAT_HG, GRID_W, NAT_KEYS), lambda b, g, r: (0, g, 0, 0, 0)),
        ],
        out_specs=pl.BlockSpec((1, qb, hw), lambda b, g, r: (b, r, g)),
        compiler_params=_params(("parallel", "parallel", "arbitrary")),
        name="nat",
    )(q, k, v, bias)


def _mix1_kernel(o_ref, x_ref, mod_ref, wo_ref, ng_ref, wr_ref, x_out, f_out, w_out, rank_out, cnt_out,
                 carry_ref, *, group):
    t = pl.program_id(1)
    m = mod_ref[0]
    x1 = x_ref[0] + m[2:3] * _dot(o_ref[0], wo_ref[...])
    x_out[0] = x1
    f = _rms(x1) * ng_ref[...] * (1.0 + m[4:5]) + m[3:4]
    f_out[0] = f.astype(BF16)
    logits = _dot3(f, wr_ref[...])
    lane = lax.broadcasted_iota(jnp.int32, logits.shape, 1)
    logits = jnp.where(lane < N_EXPERTS, logits, -jnp.inf)
    m1 = jnp.max(logits, axis=-1, keepdims=True)
    i1 = jnp.min(jnp.where(logits == m1, lane, LANES), axis=-1, keepdims=True)
    rest = jnp.where(lane == i1, -jnp.inf, logits)
    m2 = jnp.max(rest, axis=-1, keepdims=True)
    i2 = jnp.min(jnp.where(rest == m2, lane, LANES), axis=-1, keepdims=True)
    e2 = jnp.exp(m2 - m1)
    g1 = 1.0 / (1.0 + e2)
    w_out[0] = jnp.where(lane == i1, g1, 0.0) + jnp.where(lane == i2, e2 * g1, 0.0)

    @pl.when(t % group == 0)
    def _():
        carry_ref[...] = jnp.zeros_like(carry_ref)

    sel = (lane == i1) | (lane == i2)
    tm = logits.shape[0]
    tri = lax.broadcasted_iota(jnp.int32, (tm, tm), 0) >= lax.broadcasted_iota(jnp.int32, (tm, tm), 1)
    csum = _dot(_onehot(tri), _onehot(sel))
    carry = carry_ref[...]
    rank_out[0] = jnp.where(sel, carry + csum - 1.0, -1.0)
    carry = carry + csum[tm - 1:tm, :]
    carry_ref[...] = carry
    cnt_out[0, 0] = carry


def _mix1(o, xa, mod, w_out, ng, w_router, lc, tmx):
    bsz, tt, d = xa.shape
    l = tt - lc
    nct = lc // TM
    group = tmx // TM
    wr = jnp.concatenate([w_router, jnp.zeros((d, LANES - N_EXPERTS), F32)], axis=1)
    tok = lambda w: pl.BlockSpec((1, TM, w), lambda b, t: (b, t, 0))
    return pl.pallas_call(
        functools.partial(_mix1_kernel, group=group),
        out_shape=(jax.ShapeDtypeStruct((bsz, l, d), F32), jax.ShapeDtypeStruct((bsz, l, d), BF16),
                   jax.ShapeDtypeStruct((bsz, l, LANES), F32), jax.ShapeDtypeStruct((bsz, l, LANES), F32),
                   jax.ShapeDtypeStruct((bsz, l // tmx, 1, LANES), F32)),
        grid=(bsz, l // TM),
        in_specs=[tok(d), pl.BlockSpec((1, TM, d), lambda b, t: (b, t + nct, 0)),
                  pl.BlockSpec((1, 6, d), lambda b, t: (b, 0, 0)),
                  _const_spec((d, d)), _const_spec((1, d)), _const_spec(wr.shape)],
        out_specs=(tok(d), tok(d), tok(LANES), tok(LANES),
                   pl.BlockSpec((1, 1, 1, LANES), lambda b, t: (b, t // group, 0, 0))),
        scratch_shapes=[pltpu.VMEM((1, LANES), F32)],
        compiler_params=_params(("parallel", "arbitrary")),
        name="mix1_router",
    )(o, xa, mod, w_out.astype(BF16), ng.reshape(1, d), wr)


MOE_TOKENS = (2048, 1024)
MOE_GATHER = 256
MOE_ROWS = 128
MOE_BIG = 4
MOE_FF = 512
MOE_COMBINE_ROWS = 512
MOE_VMEM_LIMIT = 60 * 1024 * 1024


def _moe_kernel(cnt_ref, f_ref, rank_t_ref, rank_c_ref, gate_ref, wg_ref, wu_ref, wd_ref, o_ref, xs_ref, y_ref):
    i, e, c = pl.program_id(0), pl.program_id(1), pl.program_id(2)
    n = cnt_ref[i * pl.num_programs(1) + e]
    n_rows = (n + MOE_ROWS - 1) // MOE_ROWS
    n_gather = (n + MOE_GATHER - 1) // MOE_GATHER
    tmx = f_ref.shape[0]

    @pl.when((e == 0) & (c == 0))
    def _():
        o_ref[...] = jnp.zeros_like(o_ref)

    @pl.when(c == 0)
    def _():
        rank_t = rank_t_ref[0]

        def gather(s, carry):
            base = pl.multiple_of(s * MOE_GATHER, MOE_GATHER)
            slot = lax.broadcasted_iota(jnp.int32, (MOE_GATHER, tmx), 0) + base
            sel = _onehot(rank_t == slot)
            xs_ref[pl.ds(base, MOE_GATHER), :] = _dot(sel, f_ref[...]).astype(BF16)
            y_ref[pl.ds(base, MOE_GATHER), :] = jnp.zeros((MOE_GATHER, y_ref.shape[1]), F32)
            return carry

        lax.fori_loop(0, n_gather, gather, 0)

    def expert(base, rows):
        x = xs_ref[pl.ds(base, rows), :]
        hid = _silu(_dot(x, wg_ref[0])) * _dot(x, wu_ref[0])
        y_ref[pl.ds(base, rows), :] += _dot(hid.astype(BF16), wd_ref[0])

    big = MOE_ROWS * MOE_BIG

    def big_block(s, carry):
        expert(pl.multiple_of(s * big, big), big)
        return carry

    lax.fori_loop(0, n_rows // MOE_BIG, big_block, 0)
    done = (n_rows // MOE_BIG) * MOE_BIG
    part = MOE_BIG // 2
    while part >= 1:
        take = ((n_rows - done) // part) > 0

        @pl.when(take)
        def _(done=done, part=part):
            expert(pl.multiple_of(done * MOE_ROWS, MOE_ROWS), part * MOE_ROWS)

        done = done + jnp.where(take, part, 0)
        part //= 2

    @pl.when(c == pl.num_programs(2) - 1)
    def _():
        def combine(s, carry):
            base = pl.multiple_of(s * MOE_GATHER, MOE_GATHER)
            ys = y_ref[pl.ds(base, MOE_GATHER), :].astype(BF16)
            slot = lax.broadcasted_iota(jnp.int32, (MOE_COMBINE_ROWS, MOE_GATHER), 1) + base
            for q in range(tmx // MOE_COMBINE_ROWS):
                rows = slice(q * MOE_COMBINE_ROWS, (q + 1) * MOE_COMBINE_ROWS)
                sel = _onehot(rank_c_ref[0, rows, :] == slot)
                o_ref[rows, :] += gate_ref[0, rows, :] * _dot(sel, ys)
            return carry

        lax.fori_loop(0, n_gather, combine, 0)


def _moe(f, gates, rank, cnt, wg, wu, wd, tmx):
    bsz, l, d = f.shape
    ne, _, dff = wg.shape
    n = bsz * l
    nt = n // tmx
    per_expert = lambda a: jnp.transpose(a.reshape(n, LANES)[:, :ne])
    rank_e = per_expert(rank).astype(jnp.int32)
    counts = cnt.reshape(nt, LANES)[:, :ne].astype(jnp.int32).reshape(nt * ne)
    grid_spec = pltpu.PrefetchScalarGridSpec(
        num_scalar_prefetch=1,
        grid=(nt, ne, dff // MOE_FF),
        in_specs=[
            pl.BlockSpec((tmx, d), lambda i, e, c, cnt: (i, 0)),
            pl.BlockSpec((1, 1, tmx), lambda i, e, c, cnt: (e, 0, i)),
            pl.BlockSpec((1, tmx, 1), lambda i, e, c, cnt: (e, i, 0)),
            pl.BlockSpec((1, tmx, 1), lambda i, e, c, cnt: (e, i, 0)),
            pl.BlockSpec((1, d, MOE_FF), lambda i, e, c, cnt: (e, 0, c)),
            pl.BlockSpec((1, d, MOE_FF), lambda i, e, c, cnt: (e, 0, c)),
            pl.BlockSpec((1, MOE_FF, d), lambda i, e, c, cnt: (e, c, 0)),
        ],
        out_specs=pl.BlockSpec((tmx, d), lambda i, e, c, cnt: (i, 0)),
        scratch_shapes=[pltpu.VMEM((tmx, d), BF16), pltpu.VMEM((tmx, d), F32)],
    )
    return pl.pallas_call(
        _moe_kernel,
        out_shape=jax.ShapeDtypeStruct((n, d), F32),
        grid_spec=grid_spec,
        compiler_params=pltpu.CompilerParams(dimension_semantics=("parallel", "arbitrary", "arbitrary"),
                                             vmem_limit_bytes=MOE_VMEM_LIMIT),
        name="moe_ffn",
    )(counts, f.reshape(n, d), rank_e.reshape(ne, 1, n), rank_e.reshape(ne, n, 1),
      per_expert(gates).reshape(ne, n, 1), wg.astype(BF16), wu.astype(BF16), wd.astype(BF16)).reshape(bsz, l, d)


def _final_kernel(x_ref, y_ref, mod_ref, fg_ref, o_ref):
    x2 = x_ref[0] + mod_ref[0][5:6] * y_ref[0]
    o_ref[0] = _rms(x2) * fg_ref[...]


def _final(x, y, mod, fg):
    bsz, l, d = x.shape
    tok = pl.BlockSpec((1, TM, d), lambda b, t: (b, t, 0))
    return pl.pallas_call(
        _final_kernel,
        out_shape=jax.ShapeDtypeStruct((bsz, l, d), F32),
        grid=(bsz, l // TM),
        in_specs=[tok, tok, pl.BlockSpec((1, 6, d), lambda b, t: (b, 0, 0)), _const_spec((1, d))],
        out_specs=tok,
        compiler_params=_params(("parallel", "parallel")),
        name="final_norm",
    )(x, y, mod, fg.reshape(1, d))


def kernel(x, c, ctx, c_ctx, ada_w, ada_b, norm_g, ev_w_in, ev_conv_w, ev_a_log, ev_dt_bias, ev_gdn_norm_g, ev_q_norm_g, ev_w_uq, ev_kv_norm_g, ev_w_ukv, ev_w_out, ev_ffn_wg, ev_ffn_wu, ev_ffn_wd, od_w_qkv, od_rpb, od_w_out, od_router, od_exp_wg, od_exp_wu, od_exp_wd, final_norm_g):
    bsz, l, d = x.shape
    lc = ctx.shape[1]
    assert ada_w.shape[0] == 2 and bsz < MOD_ROWS and lc % TM == 0 and l % TM == 0
    xa = jnp.concatenate([ctx, x], axis=1)

    cs = jnp.concatenate([c, c_ctx[None], jnp.zeros((MOD_ROWS - bsz - 1, d), F32)], axis=0)
    mods = _ada_mod(cs, ada_w, ada_b).reshape(2, MOD_ROWS, 6, d)

    win, wq, wqr, wk, wv = _proj0_weights(ev_w_in[0], ev_w_uq[0], ev_w_ukv[0])
    qkv_pre, z, ab, q, k, v = _proj0(xa, mods[0], norm_g[0, 0],
                                     (win, ev_q_norm_g[0], wq, wqr, ev_kv_norm_g[0], wk, wv),
                                     _rope_tables(lc, l), lc)
    o_f, o_b = _gdn(qkv_pre, ab, ev_conv_w[0], ev_a_log[0], ev_dt_bias[0], lc)
    yb = _mla_attention(q, k, v, lc)
    xa = _mix0(o_f, o_b, z, yb, xa, mods[0], ev_gdn_norm_g[0], ev_w_out[0], norm_g[0, 1],
               ev_ffn_wg[0], ev_ffn_wu[0], ev_ffn_wd[0], lc)

    q1, k1, v1 = _proj1(xa, mods[1], norm_g[1, 0], od_w_qkv[0], lc)
    o1 = _nat(q1, k1, v1, od_rpb[0], lc)
    tmx = _pick_chunk(l, MOE_TOKENS)
    x1, f1, gates, rank, cnt = _mix1(o1, xa, mods[1], od_w_out[0], norm_g[1, 1], od_router[0], lc, tmx)
    y = _moe(f1, gates, rank, cnt, od_exp_wg[0], od_exp_wu[0], od_exp_wd[0], tmx)
    return _final(x1, y, mods[1], final_norm_g)
```

```python
import functools

import numpy as np
import jax
import jax.numpy as jnp
from jax import lax
from jax.experimental import pallas as pl
from jax.experimental.pallas import tpu as pltpu

F32 = jnp.float32
BF16 = jnp.bfloat16
HIGHEST = lax.Precision.HIGHEST

GRID_W = 64
NORM_EPS = 1e-6
A_HEADS, A_DK, A_DV = 4, 128, 128
GDN_CHUNK_LOG2 = 7
GDN_CHUNK = 1 << GDN_CHUNK_LOG2
GDN_BASE_LOG2 = 3
B_HEADS, B_Q_RANK, B_KV_RANK, B_NOPE, B_ROPE, B_DV = 8, 384, 256, 64, 32, 64
MLA_VROWS = B_DV + 16
ROPE_THETA = 10000.0
C_HEADS, C_DH = 16, 64
WIN_R, WIN_C = 8, 16
N_EXPERTS = 8
LANES = 128
MOD_ROWS = 8
NEG = -1e30
LOG2E = 1.4426950408889634
VMEM_LIMIT = 56 * 1024 * 1024

TM = 256


def _params(sem):
    return pltpu.CompilerParams(dimension_semantics=sem, vmem_limit_bytes=VMEM_LIMIT)


def _const_spec(shape):
    nd = len(shape)
    return pl.BlockSpec(shape, lambda *_: (0,) * nd, pipeline_mode=pl.Buffered(1))


def _rms(x):
    return x * lax.rsqrt(jnp.mean(x * x, axis=-1, keepdims=True) + NORM_EPS)


def _silu(x):
    return x * jax.nn.sigmoid(x)


def _onehot(mask):
    return jnp.where(mask, 1.0, 0.0).astype(BF16)


def _dot(a, b, **kw):
    return jnp.dot(a, b, preferred_element_type=F32, **kw)


def _dot_nt(a, b, **kw):
    return lax.dot_general(a, b, (((1,), (1,)), ((), ())), preferred_element_type=F32, **kw)


def _ada_kernel(s_ref, w_ref, b_ref, o_ref):
    s = _silu(s_ref[...])
    o_ref[0] = _dot(s, w_ref[0], precision=HIGHEST) + b_ref[0]


def _ada_mod(cs, ada_w, ada_b):
    depth, d, d6 = ada_w.shape
    tn = d6 // 4
    return pl.pallas_call(
        _ada_kernel,
        out_shape=jax.ShapeDtypeStruct((depth, MOD_ROWS, d6), F32),
        grid=(depth, d6 // tn),
        in_specs=[
            pl.BlockSpec((MOD_ROWS, d), lambda i, j: (0, 0)),
            pl.BlockSpec((1, d, tn), lambda i, j: (i, 0, j)),
            pl.BlockSpec((1, 1, tn), lambda i, j: (i, 0, j)),
        ],
        out_specs=pl.BlockSpec((1, MOD_ROWS, tn), lambda i, j: (i, 0, j)),
        compiler_params=_params(("parallel", "parallel")),
        name="ada_mod",
    )(cs, ada_w, ada_b.reshape(depth, 1, d6))


N_QKVZ = 4 * A_HEADS * A_DK
IN_OFF_AB = N_QKVZ
IN_OFF_CQ = IN_OFF_AB + LANES
IN_OFF_CKV = IN_OFF_CQ + B_Q_RANK
IN_OFF_KR = IN_OFF_CKV + B_KV_RANK
IN_COLS = IN_OFF_KR + LANES
HW = B_HEADS * LANES


def _proj0_kernel(x_ref, mod_ref, ng_ref, win_ref, qg_ref, wq_ref, wqr_ref, kvg_ref, wk_ref, wv_ref,
                  vone_ref, cq_ref, sq_ref, ck_ref, sk_ref,
                  qkv_ref, z_ref, ab_ref, q_ref, k_ref, vt_ref):
    x = x_ref[0]
    m = mod_ref[0]
    h = _rms(x) * ng_ref[...] * (1.0 + m[1:2]) + m[0:1]
    p = _dot(h.astype(BF16), win_ref[...])
    qkv_ref[0] = p[:, :3 * A_HEADS * A_DK]
    z_ref[0] = p[:, 3 * A_HEADS * A_DK:N_QKVZ]
    ab_ref[0] = p[:, IN_OFF_AB:IN_OFF_CQ]
    nq = (_rms(p[:, IN_OFF_CQ:IN_OFF_CKV]) * qg_ref[...]).astype(BF16)
    nkv = (_rms(p[:, IN_OFF_CKV:IN_OFF_KR]) * kvg_ref[...]).astype(BF16)
    krp = p[:, IN_OFF_KR:IN_COLS]
    qa = _dot(nq, wq_ref[...])
    qb = _dot(nq, wqr_ref[...])
    kn = _dot(nkv, wk_ref[...])
    vv = _dot(nkv, wv_ref[...]) + vone_ref[...]
    cq, sq, ck, sk = cq_ref[...], sq_ref[...], ck_ref[...], sk_ref[...]
    kr = pltpu.roll(krp, B_NOPE, axis=1) * ck + pltpu.roll(krp, B_NOPE - B_ROPE, axis=1) * sk
    for hh in range(B_HEADS):
        sl = slice(hh * LANES, (hh + 1) * LANES)
        q_ref[0, hh] = (qa[:, sl] * cq + qb[:, sl] * sq).astype(BF16)
        k_ref[0, hh] = (kn[:, sl] + kr).astype(BF16)
        vt_ref[0, hh] = vv[:, sl].T[:MLA_VROWS].astype(BF16)


def _rot_cols(w):
    q = B_ROPE // 4
    return jnp.concatenate([-w[:, q:2 * q], w[:, :q], -w[:, 3 * q:], w[:, 2 * q:3 * q]], axis=1)


def _proj0_weights(w_in, w_uq, w_ukv):
    d = w_in.shape[0]
    offs = np.cumsum([0, 512, 512, 512, 512, 8, 8, B_Q_RANK, B_KV_RANK, B_ROPE])
    zeros = lambda n: jnp.zeros((d, n), F32)
    kr = w_in[:, offs[8]:offs[9]]
    win = jnp.concatenate([
        w_in[:, :offs[4]],
        w_in[:, offs[4]:offs[6]], zeros(LANES - 16),
        w_in[:, offs[6]:offs[7]],
        w_in[:, offs[7]:offs[8]],
        kr, _rot_cols(kr), zeros(LANES - 2 * B_ROPE),
    ], axis=1).astype(BF16)
    dq = B_NOPE + B_ROPE
    wq3 = w_uq.reshape(B_Q_RANK, B_HEADS, dq)
    zq = jnp.zeros((B_Q_RANK, B_HEADS, LANES - dq), F32)
    wq = jnp.concatenate([wq3, zq], axis=2).reshape(B_Q_RANK, HW).astype(BF16)
    rot = jnp.stack([_rot_cols(wq3[:, hh, B_NOPE:]) for hh in range(B_HEADS)], axis=1)
    wqr = jnp.concatenate([jnp.zeros((B_Q_RANK, B_HEADS, B_NOPE), F32), rot, zq], axis=2)
    wqr = wqr.reshape(B_Q_RANK, HW).astype(BF16)
    wkv3 = w_ukv.reshape(B_KV_RANK, B_HEADS, B_NOPE + B_DV)
    zk = jnp.zeros((B_KV_RANK, B_HEADS, LANES - B_NOPE), F32)
    wk = jnp.concatenate([wkv3[:, :, :B_NOPE], zk], axis=2).reshape(B_KV_RANK, HW).astype(BF16)
    vpart = wkv3[:, :, B_NOPE:]
    wv = jnp.concatenate([vpart, jnp.zeros_like(vpart)], axis=2).reshape(B_KV_RANK, HW).astype(BF16)
    return win, wq, wqr, wk, wv


def _value_ones():
    vone = np.zeros((1, HW), np.float32)
    for hh in range(B_HEADS):
        vone[0, hh * LANES + B_DV:hh * LANES + MLA_VROWS] = 1.0
    return jnp.asarray(vone)


def _rope_tables(lc, l):
    t = np.arange(l)
    half = B_ROPE // 2
    inv = ROPE_THETA ** (-np.arange(0, half, 2, dtype=np.float64) / half)
    ar = (t // GRID_W)[:, None] * inv[None, :]
    ac = (t % GRID_W)[:, None] * inv[None, :]
    ang = np.concatenate([ar, ar, ac, ac], axis=-1)
    cos = np.concatenate([np.ones((lc, B_ROPE)), np.cos(ang)], axis=0)
    sin = np.concatenate([np.zeros((lc, B_ROPE)), np.sin(ang)], axis=0)
    tt = lc + l
    scale = (B_NOPE + B_ROPE) ** -0.5 * LOG2E
    pad = np.zeros((tt, LANES - B_NOPE - B_ROPE))
    z64 = np.zeros((tt, B_NOPE))
    cq = np.concatenate([np.full((tt, B_NOPE), scale), scale * cos, pad], axis=1)
    sq = np.concatenate([z64, scale * sin, pad], axis=1)
    ck = np.concatenate([z64, cos, pad], axis=1)
    sk = np.concatenate([z64, sin, pad], axis=1)
    return tuple(jnp.asarray(a, F32) for a in (cq, sq, ck, sk))


def _mod_row(b, t, n_ctx_tiles, bsz):
    return jnp.where(t < n_ctx_tiles, bsz, b)


def _proj0(xa, mod, ng, weights, tables, lc):
    bsz, tt, d = xa.shape
    win, qg, wq, wqr, kvg, wk, wv = weights
    vone = _value_ones()
    nct = lc // TM
    tok = lambda w: pl.BlockSpec((1, TM, w), lambda b, t: (b, t, 0))
    head = pl.BlockSpec((1, B_HEADS, TM, LANES), lambda b, t: (b, 0, t, 0))
    head_t = pl.BlockSpec((1, B_HEADS, MLA_VROWS, TM), lambda b, t: (b, 0, 0, t))
    tab = pl.BlockSpec((TM, LANES), lambda b, t: (t, 0))
    hshape = jax.ShapeDtypeStruct((bsz, B_HEADS, tt, LANES), BF16)
    return pl.pallas_call(
        _proj0_kernel,
        out_shape=(
            jax.ShapeDtypeStruct((bsz, tt, 3 * A_HEADS * A_DK), F32),
            jax.ShapeDtypeStruct((bsz, tt, A_HEADS * A_DV), F32),
            jax.ShapeDtypeStruct((bsz, tt, LANES), F32),
            hshape, hshape, jax.ShapeDtypeStruct((bsz, B_HEADS, MLA_VROWS, tt), BF16),
        ),
        grid=(bsz, tt // TM),
        in_specs=[
            tok(d),
            pl.BlockSpec((1, 6, d), lambda b, t: (_mod_row(b, t, nct, bsz), 0, 0)),
            _const_spec((1, d)), _const_spec(win.shape),
            _const_spec((1, B_Q_RANK)), _const_spec(wq.shape), _const_spec(wqr.shape),
            _const_spec((1, B_KV_RANK)), _const_spec(wk.shape), _const_spec(wv.shape),
            _const_spec(vone.shape),
            tab, tab, tab, tab,
        ],
        out_specs=(tok(3 * A_HEADS * A_DK), tok(A_HEADS * A_DV), tok(LANES), head, head, head_t),
        compiler_params=_params(("parallel", "parallel")),
        name="proj0",
    )(xa, mod, ng.reshape(1, d), win, qg.reshape(1, -1), wq, wqr, kvg.reshape(1, -1), wk, wv, vone,
      *tables)


A_CONV = 5
CONV_HALO = 8


def _split_bf16(a):
    hi = a.astype(BF16)
    return hi, (a - hi.astype(F32)).astype(BF16)


def _dot3(a, b):
    ah, al = _split_bf16(a)
    bh, bl = _split_bf16(b)
    return _dot(ah, bh) + (_dot(ah, bl) + _dot(al, bh))


def _bdot(a, b):
    return lax.dot_general(a.astype(BF16), b.astype(BF16), (((2,), (1,)), ((0,), (0,))),
                           preferred_element_type=F32)


def _gdn_pre_kernel(x_ref, prev_ref, next_ref, ab_ref, cw_ref, shift_ref, alog_ref, dtb_ref,
                    u_ref, wq_ref, kgt_ref, qk_ref, gl_ref, *, ncc, nc):
    c = GDN_CHUNK
    i = pl.program_id(1)
    prev_ok = ((i != 0) & (i != ncc)).astype(F32)
    next_ok = ((i != ncc - 1) & (i != nc - 1)).astype(F32)
    xe = jnp.concatenate([prev_ref[0] * prev_ok, x_ref[0], next_ref[0] * next_ok], axis=0)
    cw = cw_ref[...]
    xw = jnp.concatenate([(xe * cw[j:j + 1]).astype(BF16) for j in range(A_CONV)], axis=0)
    y = _silu(_dot(shift_ref[...], xw))
    nq = A_HEADS * A_DK

    def l2n(t):
        return t * lax.rsqrt(jnp.sum(t * t, axis=-1, keepdims=True) + 1e-6)

    ab = ab_ref[0]
    g_all = -jnp.exp(alog_ref[...]) * (jnp.maximum(ab + dtb_ref[...], 0.0)
                                       + jnp.log1p(jnp.exp(-jnp.abs(ab + dtb_ref[...]))))
    beta_all = jax.nn.sigmoid(ab)
    g_all_t = g_all.T
    g_tot = jnp.sum(g_all, axis=0, keepdims=True)
    row = lax.broadcasted_iota(jnp.int32, (c, c), 0)
    col = lax.broadcasted_iota(jnp.int32, (c, c), 1)
    eye = (row == col).astype(F32)
    ns, rhs, qgs = [], [], []
    heads = []
    for h in range(A_HEADS):
        sl = slice(h * A_DK, (h + 1) * A_DK)
        kh = l2n(y[:, nq + h * A_DK:nq + (h + 1) * A_DK])
        heads.append((l2n(y[:, sl]) * (A_DK ** -0.5), kh, y[:, 2 * nq + h * A_DV:2 * nq + (h + 1) * A_DV],
                      kh.astype(BF16)))
    for d in range(2):
        incl = (row >= col) if d == 0 else (row <= col)
        strict = (row > col) if d == 0 else (row < col)
        m_incl = incl.astype(F32)
        gc_col = _dot(m_incl, g_all, precision=HIGHEST)
        gc_row = _dot_nt(g_all_t, m_incl, precision=HIGHEST)
        gl_rows = []
        for h in range(A_HEADS):
            qh, kh, vh, khb = heads[h]
            ln = d * A_HEADS + h
            beta = beta_all[:, 2 * A_HEADS + ln:2 * A_HEADS + ln + 1]
            gcc = gc_col[:, ln:ln + 1]
            gcr = gc_row[ln:ln + 1, :]
            gtot = g_tot[:, ln:ln + 1]
            decay = jnp.where(incl, jnp.exp(jnp.where(incl, gcc - gcr, 0.0)), 0.0)
            n = -jnp.where(strict, beta * _dot_nt(khb, khb) * decay, 0.0)
            eg = jnp.exp(gcc)
            ns.append(n)
            rhs.append(jnp.concatenate([vh * beta, kh * (beta * eg)], axis=1))
            qgs.append(qh * eg)
            kgt_ref[0, d, 0, h] = (kh * jnp.exp(gtot - gcc)).T.astype(BF16)
            qk_ref[0, d, 0, h] = jnp.where(incl, _dot_nt(qh.astype(BF16), khb) * decay, 0.0).astype(BF16)
            gl_rows.append(jnp.broadcast_to(jnp.exp(gtot), (1, LANES)))
        gl_ref[0, d, 0] = jnp.concatenate(gl_rows + gl_rows, axis=0)
    n = jnp.stack(ns, axis=0)
    same = lambda s: lax.shift_right_logical(row, s) == lax.shift_right_logical(col, s)
    nd = jnp.where(same(GDN_BASE_LOG2), n, 0.0)
    x = eye + nd
    p = _bdot(nd, nd)
    z = _bdot(p, jnp.concatenate([p, x], axis=2))
    x = x + z[:, :, c:]
    x = x + _bdot(z[:, :, :c], x)
    for s in range(GDN_BASE_LOG2, GDN_CHUNK_LOG2):
        nl = jnp.where(same(s + 1) & jnp.logical_not(same(s)), n, 0.0)
        x = x + _bdot(_bdot(x, nl), x)
    y = _bdot(x, jnp.stack(rhs, axis=0))
    for d in range(2):
        for h in range(A_HEADS):
            i = d * A_HEADS + h
            u_ref[0, d, 0, h] = y[i, :, :A_DV]
            wq_ref[0, d, 0, h] = jnp.concatenate([y[i, :, A_DV:], qgs[i]], axis=0).astype(BF16)


def _gdn_scan_kernel(*refs, bsz):
    ins, (of_ref, ob_ref, s_ref) = refs[:10], refs[10:]

    @pl.when(pl.program_id(0) == 0)
    def _():
        s_ref[...] = jnp.zeros_like(s_ref)

    c = GDN_CHUNK
    n = bsz * A_HEADS
    bdot = lambda a, b: lax.dot_general(a, b, (((2,), (1,)), ((0,), (0,))), preferred_element_type=F32)
    for d, o_ref in ((0, of_ref), (1, ob_ref)):
        u_ref, wq_ref, kgt_ref, qk_ref, gl_ref = ins[5 * d:5 * d + 5]
        flat = lambda ref: ref[:, 0, 0].reshape(n, *ref.shape[-2:])
        s = s_ref[:, d].reshape(n, A_DK, A_DV)
        ws = bdot(flat(wq_ref), s.astype(BF16))
        v_new = (flat(u_ref) - ws[:, :c]).astype(BF16)
        o = ws[:, c:] + bdot(flat(qk_ref), v_new)
        gl = gl_ref[:, 0, 0, :A_HEADS, :].reshape(n, 1, LANES)
        s_ref[:, d] = (s * gl + bdot(flat(kgt_ref), v_new)).reshape(bsz, A_HEADS, A_DK, A_DV)
        for b in range(bsz):
            for h in range(A_HEADS):
                o_ref[b, :, h * A_DV:(h + 1) * A_DV] = o[b * A_HEADS + h]


def _gdn(qkv_pre, ab, conv_w, a_log, dt_bias, lc):
    bsz, tt, wqkv = qkv_pre.shape
    c = GDN_CHUNK
    nc, ncc = tt // c, lc // c
    hb = c // CONV_HALO
    per = lambda r, w: pl.BlockSpec((1, 2, 1, A_HEADS, r, w), lambda b, i: (b, 0, i, 0, 0, 0))
    shp = lambda r, w, dt: jax.ShapeDtypeStruct((bsz, 2, nc, A_HEADS, r, w), dt)
    lane_row = lambda t: jnp.concatenate([t.reshape(1, -1), jnp.zeros((1, LANES - t.size), F32)], axis=1)
    cw = jnp.concatenate([conv_w, jnp.zeros((CONV_HALO - A_CONV, wqkv), F32)], axis=0)
    rows = c + 2 * CONV_HALO
    shift = np.zeros((c, A_CONV * rows), np.float32)
    for j in range(A_CONV):
        shift[np.arange(c), j * rows + np.arange(c) + CONV_HALO - A_CONV // 2 + j] = 1.0
    shift = jnp.asarray(shift, BF16)
    pre = pl.pallas_call(
        functools.partial(_gdn_pre_kernel, ncc=ncc, nc=nc),
        out_shape=(shp(c, A_DV, F32), shp(2 * c, A_DK, BF16), shp(A_DK, c, BF16), shp(c, c, BF16),
                   jax.ShapeDtypeStruct((bsz, 2, nc, 2 * A_HEADS, LANES), F32)),
        grid=(bsz, nc),
        in_specs=[pl.BlockSpec((1, c, wqkv), lambda b, i: (b, i, 0)),
                  pl.BlockSpec((1, CONV_HALO, wqkv), lambda b, i: (b, jnp.maximum(i * hb - 1, 0), 0)),
                  pl.BlockSpec((1, CONV_HALO, wqkv), lambda b, i: (b, jnp.minimum((i + 1) * hb, nc * hb - 1), 0)),
                  pl.BlockSpec((1, c, LANES), lambda b, i: (b, i, 0)),
                  _const_spec(cw.shape), _const_spec(shift.shape), _const_spec((1, LANES)),
                  _const_spec((1, LANES))],
        out_specs=(per(c, A_DV), per(2 * c, A_DK), per(A_DK, c), per(c, c),
                   pl.BlockSpec((1, 2, 1, 2 * A_HEADS, LANES), lambda b, i: (b, 0, i, 0, 0))),
        compiler_params=_params(("parallel", "parallel")),
        name="gdn_pre",
    )(qkv_pre, qkv_pre, qkv_pre, ab, cw, shift, lane_row(a_log), lane_row(dt_bias))

    def chunk(d, s):
        return s if d == 0 else jnp.where(s < ncc, ncc - 1 - s, nc - 1 - (s - ncc))

    in_specs = []
    for d in range(2):
        for arr in pre[:4]:
            r, w = arr.shape[-2:]
            in_specs.append(pl.BlockSpec((bsz, 1, 1, A_HEADS, r, w),
                                         lambda s, d=d: (0, d, chunk(d, s), 0, 0, 0)))
        in_specs.append(pl.BlockSpec((bsz, 1, 1, 2 * A_HEADS, LANES), lambda s, d=d: (0, d, chunk(d, s), 0, 0)))
    o_shape = jax.ShapeDtypeStruct((bsz, tt, A_HEADS * A_DV), F32)
    return pl.pallas_call(
        functools.partial(_gdn_scan_kernel, bsz=bsz),
        out_shape=(o_shape, o_shape),
        grid=(nc,),
        in_specs=in_specs,
        out_specs=tuple(pl.BlockSpec((bsz, c, A_HEADS * A_DV), lambda s, d=d: (0, chunk(d, s), 0))
                        for d in range(2)),
        scratch_shapes=[pltpu.VMEM((bsz, 2, A_HEADS, A_DK, A_DV), F32)],
        compiler_params=_params(("arbitrary",)),
        name="gdn_scan",
    )(*pre, *pre)


def _pick_chunk(n, options):
    for o in options:
        if n % o == 0:
            return o
    raise ValueError(f"no chunk size in {options} divides {n}")


def _mla_kernel(q_ref, k_ref, vt_ref, o_ref, sa_ref, sb_ref, pa_ref, pb_ref, *, lc, tq, tk, nk):
    t = pl.program_id(2)
    g = q_ref.shape[1]
    bnt = lambda a, b: lax.dot_general(a, b, (((2,), (2,)), ((0,), (0,))), preferred_element_type=F32)
    bnn = lambda a, b: lax.dot_general(a, b, (((2,), (1,)), ((0,), (0,))), preferred_element_type=F32)

    def absorb(s, m, acc, vt_chunk):
        m_new = jnp.maximum(m, jnp.max(s, axis=1, keepdims=True))
        p = jnp.exp2(s - m_new).astype(BF16)
        return m_new, jnp.exp2(m - m_new) * acc + bnn(vt_chunk, p)

    def update(q, m, acc, k_chunk, vt_chunk):
        return absorb(bnt(k_chunk, q), m, acc, vt_chunk)

    def finish(acc):
        parts = [acc[h, :B_DV] / acc[h, B_DV:B_DV + 1] for h in range(g)]
        o_ref[0] = jnp.concatenate(parts, axis=0).T

    init = (jnp.full((g, 1, tq), -jnp.inf, F32), jnp.zeros((g, MLA_VROWS, tq), F32))

    @pl.when(t < lc // tq)
    def _():
        finish(update(q_ref[0], *init, k_ref[0, :, :lc, :], vt_ref[0, :, :, :lc])[1])

    @pl.when(t >= lc // tq)
    def _():
        q = q_ref[0]

        s_bufs, p_bufs = (sa_ref, sb_ref), (pa_ref, pb_ref)
        chunk = lambda c: pl.ds(pl.multiple_of(c * tk, tk), tk)

        def stage_s(c, par):
            s_bufs[par][...] = bnt(k_ref[0, :, chunk(c), :], q)

        def stage_x(par, m):
            s = s_bufs[par][...]
            m_new = jnp.maximum(m, jnp.max(s, axis=1, keepdims=True))
            p_bufs[par][...] = jnp.exp2(s - m_new).astype(BF16)
            return m_new, jnp.exp2(m - m_new)

        def stage_v(c, par, alpha, acc):
            return alpha * acc + bnn(vt_ref[0, :, :, chunk(c)], p_bufs[par][...])

        def step(c, par, state, do_s=True, do_x=True):
            m, alpha, acc = state
            if do_s:
                stage_s(c + 2, par)
            if do_x:
                m_next, alpha_next = stage_x(1 - par, m)
            else:
                m_next, alpha_next = m, alpha
            return m_next, alpha_next, stage_v(c, par, alpha, acc)

        m0, acc0 = init
        stage_s(0, 0)
        if nk > 1:
            stage_s(1, 1)
        m1, alpha1 = stage_x(0, m0)
        state = (m1, alpha1, acc0)
        pairs = max(nk - 2, 0) // 2

        def body(i, state):
            return step(2 * i + 1, 1, step(2 * i, 0, state))

        state = lax.fori_loop(0, pairs, body, state)
        for c in range(2 * pairs, nk):
            state = step(c, c % 2, state, do_s=c + 2 < nk, do_x=c + 1 < nk)
        finish(state[2])


MLA_G = 8
MLA_VMEM_LIMIT = 60 * 1024 * 1024


def _mla_attention(q, k, vt, lc):
    bsz, nh, tt, _ = q.shape
    tq = TM
    tk = _pick_chunk(tt, (768, 512, 256))
    g = MLA_G
    kern = functools.partial(_mla_kernel, lc=lc, tq=tq, tk=tk, nk=tt // tk)
    return pl.pallas_call(
        kern,
        out_shape=jax.ShapeDtypeStruct((bsz, tt, nh * B_DV), F32),
        grid=(bsz, nh // g, tt // tq),
        in_specs=[pl.BlockSpec((1, g, tq, LANES), lambda b, hp, t: (b, hp, t, 0)),
                  pl.BlockSpec((1, g, tt, LANES), lambda b, hp, t: (b, hp, 0, 0), pipeline_mode=pl.Buffered(1)),
                  pl.BlockSpec((1, g, MLA_VROWS, tt), lambda b, hp, t: (b, hp, 0, 0),
                               pipeline_mode=pl.Buffered(1))],
        out_specs=pl.BlockSpec((1, tq, g * B_DV), lambda b, hp, t: (b, t, hp)),
        scratch_shapes=[pltpu.VMEM((g, tk, tq), F32), pltpu.VMEM((g, tk, tq), F32),
                        pltpu.VMEM((g, tk, tq), BF16), pltpu.VMEM((g, tk, tq), BF16)],
        compiler_params=pltpu.CompilerParams(dimension_semantics=("parallel", "parallel", "parallel"),
                                             vmem_limit_bytes=MLA_VMEM_LIMIT),
        name="mla_attention",
    )(q, k, vt)


def _mix0_kernel(of_ref, ob_ref, z_ref, yb_ref, x_ref, mod_ref, gg_ref, woa_ref, wob_ref, ng_ref,
                 wg_ref, wu_ref, wd_ref, o_ref):
    o = of_ref[0] + ob_ref[0]
    z = z_ref[0]
    gg = gg_ref[...]
    parts = []
    for h in range(A_HEADS):
        sl = slice(h * A_DV, (h + 1) * A_DV)
        parts.append((_rms(o[:, sl]) * gg * _silu(z[:, sl])).astype(BF16))
    ya = jnp.concatenate(parts, axis=1)
    y = _dot(ya, woa_ref[...]) + _dot(yb_ref[0].astype(BF16), wob_ref[...])
    m = mod_ref[0]
    x1 = x_ref[0] + m[2:3] * y
    f = (_rms(x1) * ng_ref[...] * (1.0 + m[4:5]) + m[3:4]).astype(BF16)
    hid = _silu(_dot(f, wg_ref[...])) * _dot(f, wu_ref[...])
    o_ref[0] = x1 + m[5:6] * _dot(hid.astype(BF16), wd_ref[...])


def _mix0(o_f, o_b, z, yb, xa, mod, gdn_g, w_out, ng, wg, wu, wd, lc):
    bsz, tt, d = xa.shape
    nct = lc // TM
    na = A_HEADS * A_DV
    woa, wob = w_out[:na].astype(BF16), w_out[na:].astype(BF16)
    wg, wu, wd = wg.astype(BF16), wu.astype(BF16), wd.astype(BF16)
    tok = lambda w: pl.BlockSpec((1, TM, w), lambda b, t: (b, t, 0))
    return pl.pallas_call(
        _mix0_kernel,
        out_shape=jax.ShapeDtypeStruct((bsz, tt, d), F32),
        grid=(bsz, tt // TM),
        in_specs=[
            tok(na), tok(na), tok(na), tok(B_HEADS * B_DV), tok(d),
            pl.BlockSpec((1, 6, d), lambda b, t: (_mod_row(b, t, nct, bsz), 0, 0)),
            _const_spec((1, A_DV)), _const_spec(woa.shape), _const_spec(wob.shape), _const_spec((1, d)),
            _const_spec(wg.shape), _const_spec(wu.shape), _const_spec(wd.shape),
        ],
        out_specs=tok(d),
        compiler_params=_params(("parallel", "parallel")),
        name="mix0_ffn",
    )(o_f, o_b, z, yb, xa, mod, gdn_g.reshape(1, -1), woa, wob, ng.reshape(1, d), wg, wu, wd)


def _proj1_kernel(x_ref, mod_ref, ng_ref, w_ref, q_ref, k_ref, v_ref):
    m = mod_ref[0]
    h = (_rms(x_ref[0]) * ng_ref[...] * (1.0 + m[1:2]) + m[0:1]).astype(BF16)
    p = _dot(h, w_ref[...])
    d = q_ref.shape[-1]
    q_ref[0] = (p[:, :d] * (C_DH ** -0.5 * LOG2E)).astype(BF16)
    k_ref[0] = p[:, d:2 * d].astype(BF16)
    v_ref[0] = p[:, 2 * d:].astype(BF16)


def _proj1(xa, mod, ng, w_qkv, lc):
    bsz, tt, d = xa.shape
    nct = lc // TM
    w = w_qkv.astype(BF16)
    tok = pl.BlockSpec((1, TM, d), lambda b, t: (b, t, 0))
    shp = jax.ShapeDtypeStruct((bsz, tt, d), BF16)
    return pl.pallas_call(
        _proj1_kernel,
        out_shape=(shp, shp, shp),
        grid=(bsz, tt // TM),
        in_specs=[tok, pl.BlockSpec((1, 6, d), lambda b, t: (_mod_row(b, t, nct, bsz), 0, 0)),
                  _const_spec((1, d)), _const_spec(w.shape)],
        out_specs=(tok, tok, tok),
        compiler_params=_params(("parallel", "parallel")),
        name="proj1",
    )(xa, mod, ng.reshape(1, d), w)


NAT_HG = 4
NAT_KEYS = WIN_R * GRID_W


def _nat_bias(rpb):
    qc = np.arange(GRID_W)[:, None]
    kc = np.arange(GRID_W)[None, :]
    c0 = np.clip(qc - WIN_C // 2, 0, GRID_W - WIN_C)
    inside = (kc >= c0) & (kc < c0 + WIN_C)
    dc = kc - qc + (WIN_C - 1)
    onehot = ((np.arange(2 * WIN_C - 1)[:, None, None] == dc[None]) & inside[None]).astype(np.float32)
    tab = jnp.einsum('hrd,dqk->hqrk', rpb, jnp.asarray(onehot), precision=HIGHEST)
    tab = tab + jnp.asarray(np.where(inside, 0.0, NEG).astype(np.float32))[None, :, None, :]
    tab = jnp.stack([tab[:, :, WIN_R - 1 - v:2 * WIN_R - 1 - v, :] for v in range(WIN_R)], axis=0)
    return tab.reshape(WIN_R, rpb.shape[0], GRID_W, NAT_KEYS)


NAT_RB = 4


def _nat_kernel(q_ref, k_ref, v_ref, bias_ref, o_ref, *, lc, rows):
    i = pl.program_id(2)
    kc = k_ref[0, :lc, :]
    vc = v_ref[0, :lc, :]
    lane = lax.broadcasted_iota(jnp.int32, (GRID_W, NAT_HG * C_DH), 1)
    masks = [(lane >= h * C_DH) & (lane < (h + 1) * C_DH) for h in range(NAT_HG)]
    for j in range(NAT_RB):
        r = i * NAT_RB + j
        r0 = jnp.clip(r - WIN_R // 2, 0, rows - WIN_R)
        start = pl.multiple_of(lc + r0 * GRID_W, GRID_W)
        kw = k_ref[0, pl.ds(start, NAT_KEYS), :]
        vw = v_ref[0, pl.ds(start, NAT_KEYS), :]
        q = q_ref[0, j * GRID_W:(j + 1) * GRID_W, :]
        qs = jnp.concatenate([jnp.where(hm, q, jnp.zeros_like(q)) for hm in masks], axis=0)
        s1 = _dot_nt(qs, kw) + bias_ref[r - r0, 0].reshape(NAT_HG * GRID_W, NAT_KEYS)
        s2 = _dot_nt(qs, kc)
        m = jnp.maximum(jnp.max(s1, axis=-1, keepdims=True), jnp.max(s2, axis=-1, keepdims=True))
        p1 = jnp.exp2(s1 - m)
        p2 = jnp.exp2(s2 - m)
        l = jnp.sum(p1, axis=-1, keepdims=True) + jnp.sum(p2, axis=-1, keepdims=True)
        o = (_dot(p1.astype(BF16), vw) + _dot(p2.astype(BF16), vc)) / l
        acc = jnp.zeros((GRID_W, NAT_HG * C_DH), F32)
        for h, hm in enumerate(masks):
            acc = jnp.where(hm, o[h * GRID_W:(h + 1) * GRID_W], acc)
        o_ref[0, j * GRID_W:(j + 1) * GRID_W, :] = acc.astype(BF16)


def _nat(q, k, v, rpb, lc):
    bsz, tt, d = q.shape
    l = tt - lc
    rows = l // GRID_W
    assert rows >= WIN_R and rows % NAT_RB == 0 and lc % (NAT_RB * GRID_W) == 0
    hw = NAT_HG * C_DH
    ng = d // hw
    bias = (_nat_bias(rpb) * LOG2E).reshape(WIN_R, ng, NAT_HG, GRID_W, NAT_KEYS)
    kern = functools.partial(_nat_kernel, lc=lc, rows=rows)
    qb = NAT_RB * GRID_W
    kv = pl.BlockSpec((1, tt, hw), lambda b, g, r: (b, 0, g))
    return pl.pallas_call(
        kern,
        out_shape=jax.ShapeDtypeStruct((bsz, l, d), BF16),
        grid=(bsz, ng, rows // NAT_RB),
        in_specs=[
            pl.BlockSpec((1, qb, hw), lambda b, g, r: (b, lc // qb + r, g)),
            kv, kv,
            pl.BlockSpec((WIN_R, 1, NAT_HG, GRID_W, NAT_KEYS), lambda b, g, r: (0, g, 0, 0, 0)),
        ],
        out_specs=pl.BlockSpec((1, qb, hw), lambda b, g, r: (b, r, g)),
        compiler_params=_params(("parallel", "parallel", "arbitrary")),
        name="nat",
    )(q, k, v, bias)


def _mix1_kernel(o_ref, x_ref, mod_ref, wo_ref, ng_ref, wr_ref, x_out, f_out, w_out, rank_out, cnt_out,
                 carry_ref, *, group):
    t = pl.program_id(1)
    m = mod_ref[0]
    x1 = x_ref[0] + m[2:3] * _dot(o_ref[0], wo_ref[...])
    x_out[0] = x1
    f = _rms(x1) * ng_ref[...] * (1.0 + m[4:5]) + m[3:4]
    f_out[0] = f.astype(BF16)
    logits = _dot3(f, wr_ref[...])
    lane = lax.broadcasted_iota(jnp.int32, logits.shape, 1)
    logits = jnp.where(lane < N_EXPERTS, logits, -jnp.inf)
    m1 = jnp.max(logits, axis=-1, keepdims=True)
    i1 = jnp.min(jnp.where(logits == m1, lane, LANES), axis=-1, keepdims=True)
    rest = jnp.where(lane == i1, -jnp.inf, logits)
    m2 = jnp.max(rest, axis=-1, keepdims=True)
    i2 = jnp.min(jnp.where(rest == m2, lane, LANES), axis=-1, keepdims=True)
    e2 = jnp.exp(m2 - m1)
    g1 = 1.0 / (1.0 + e2)
    w_out[0] = jnp.where(lane == i1, g1, 0.0) + jnp.where(lane == i2, e2 * g1, 0.0)

    @pl.when(t % group == 0)
    def _():
        carry_ref[...] = jnp.zeros_like(carry_ref)

    sel = (lane == i1) | (lane == i2)
    tm = logits.shape[0]
    tri = lax.broadcasted_iota(jnp.int32, (tm, tm), 0) >= lax.broadcasted_iota(jnp.int32, (tm, tm), 1)
    csum = _dot(_onehot(tri), _onehot(sel))
    carry = carry_ref[...]
    rank_out[0] = jnp.where(sel, carry + csum - 1.0, -1.0)
    carry = carry + csum[tm - 1:tm, :]
    carry_ref[...] = carry
    cnt_out[0, 0] = carry


def _mix1(o, xa, mod, w_out, ng, w_router, lc, tmx):
    bsz, tt, d = xa.shape
    l = tt - lc
    nct = lc // TM
    group = tmx // TM
    wr = jnp.concatenate([w_router, jnp.zeros((d, LANES - N_EXPERTS), F32)], axis=1)
    tok = lambda w: pl.BlockSpec((1, TM, w), lambda b, t: (b, t, 0))
    return pl.pallas_call(
        functools.partial(_mix1_kernel, group=group),
        out_shape=(jax.ShapeDtypeStruct((bsz, l, d), F32), jax.ShapeDtypeStruct((bsz, l, d), BF16),
                   jax.ShapeDtypeStruct((bsz, l, LANES), F32), jax.ShapeDtypeStruct((bsz, l, LANES), F32),
                   jax.ShapeDtypeStruct((bsz, l // tmx, 1, LANES), F32)),
        grid=(bsz, l // TM),
        in_specs=[tok(d), pl.BlockSpec((1, TM, d), lambda b, t: (b, t + nct, 0)),
                  pl.BlockSpec((1, 6, d), lambda b, t: (b, 0, 0)),
                  _const_spec((d, d)), _const_spec((1, d)), _const_spec(wr.shape)],
        out_specs=(tok(d), tok(d), tok(LANES), tok(LANES),
                   pl.BlockSpec((1, 1, 1, LANES), lambda b, t: (b, t // group, 0, 0))),
        scratch_shapes=[pltpu.VMEM((1, LANES), F32)],
        compiler_params=_params(("parallel", "arbitrary")),
        name="mix1_router",
    )(o, xa, mod, w_out.astype(BF16), ng.reshape(1, d), wr)


MOE_TOKENS = (2048, 1024)
MOE_GATHER = 256
MOE_ROWS = 128
MOE_BIG = 4
MOE_FF = 512
MOE_COMBINE_ROWS = 512
MOE_VMEM_LIMIT = 60 * 1024 * 1024


def _moe_kernel(cnt_ref, f_ref, rank_t_ref, rank_c_ref, gate_ref, wg_ref, wu_ref, wd_ref, o_ref, xs_ref, y_ref):
    i, e, c = pl.program_id(0), pl.program_id(1), pl.program_id(2)
    n = cnt_ref[i * pl.num_programs(1) + e]
    n_rows = (n + MOE_ROWS - 1) // MOE_ROWS
    n_gather = (n + MOE_GATHER - 1) // MOE_GATHER
    tmx = f_ref.shape[0]

    @pl.when((e == 0) & (c == 0))
    def _():
        o_ref[...] = jnp.zeros_like(o_ref)

    @pl.when(c == 0)
    def _():
        rank_t = rank_t_ref[0]

        def gather(s, carry):
            base = pl.multiple_of(s * MOE_GATHER, MOE_GATHER)
            slot = lax.broadcasted_iota(jnp.int32, (MOE_GATHER, tmx), 0) + base
            sel = _onehot(rank_t == slot)
            xs_ref[pl.ds(base, MOE_GATHER), :] = _dot(sel, f_ref[...]).astype(BF16)
            y_ref[pl.ds(base, MOE_GATHER), :] = jnp.zeros((MOE_GATHER, y_ref.shape[1]), F32)
            return carry

        lax.fori_loop(0, n_gather, gather, 0)

    def expert(base, rows):
        x = xs_ref[pl.ds(base, rows), :]
        hid = _silu(_dot(x, wg_ref[0])) * _dot(x, wu_ref[0])
        y_ref[pl.ds(base, rows), :] += _dot(hid.astype(BF16), wd_ref[0])

    big = MOE_ROWS * MOE_BIG

    def big_block(s, carry):
        expert(pl.multiple_of(s * big, big), big)
        return carry

    lax.fori_loop(0, n_rows // MOE_BIG, big_block, 0)
    done = (n_rows // MOE_BIG) * MOE_BIG
    part = MOE_BIG // 2
    while part >= 1:
        take = ((n_rows - done) // part) > 0

        @pl.when(take)
        def _(done=done, part=part):
            expert(pl.multiple_of(done * MOE_ROWS, MOE_ROWS), part * MOE_ROWS)

        done = done + jnp.where(take, part, 0)
        part //= 2

    @pl.when(c == pl.num_programs(2) - 1)
    def _():
        def combine(s, carry):
            base = pl.multiple_of(s * MOE_GATHER, MOE_GATHER)
            ys = y_ref[pl.ds(base, MOE_GATHER), :].astype(BF16)
            slot = lax.broadcasted_iota(jnp.int32, (MOE_COMBINE_ROWS, MOE_GATHER), 1) + base
            for q in range(tmx // MOE_COMBINE_ROWS):
                rows = slice(q * MOE_COMBINE_ROWS, (q + 1) * MOE_COMBINE_ROWS)
                sel = _onehot(rank_c_ref[0, rows, :] == slot)
                o_ref[rows, :] += gate_ref[0, rows, :] * _dot(sel, ys)
            return carry

        lax.fori_loop(0, n_gather, combine, 0)


def _moe(f, gates, rank, cnt, wg, wu, wd, tmx):
    bsz, l, d = f.shape
    ne, _, dff = wg.shape
    n = bsz * l
    nt = n // tmx
    per_expert = lambda a: jnp.transpose(a.reshape(n, LANES)[:, :ne])
    rank_e = per_expert(rank).astype(jnp.int32)
    counts = cnt.reshape(nt, LANES)[:, :ne].astype(jnp.int32).reshape(nt * ne)
    grid_spec = pltpu.PrefetchScalarGridSpec(
        num_scalar_prefetch=1,
        grid=(nt, ne, dff // MOE_FF),
        in_specs=[
            pl.BlockSpec((tmx, d), lambda i, e, c, cnt: (i, 0)),
            pl.BlockSpec((1, 1, tmx), lambda i, e, c, cnt: (e, 0, i)),
            pl.BlockSpec((1, tmx, 1), lambda i, e, c, cnt: (e, i, 0)),
            pl.BlockSpec((1, tmx, 1), lambda i, e, c, cnt: (e, i, 0)),
            pl.BlockSpec((1, d, MOE_FF), lambda i, e, c, cnt: (e, 0, c)),
            pl.BlockSpec((1, d, MOE_FF), lambda i, e, c, cnt: (e, 0, c)),
            pl.BlockSpec((1, MOE_FF, d), lambda i, e, c, cnt: (e, c, 0)),
        ],
        out_specs=pl.BlockSpec((tmx, d), lambda i, e, c, cnt: (i, 0)),
        scratch_shapes=[pltpu.VMEM((tmx, d), BF16), pltpu.VMEM((tmx, d), F32)],
    )
    return pl.pallas_call(
        _moe_kernel,
        out_shape=jax.ShapeDtypeStruct((n, d), F32),
        grid_spec=grid_spec,
        compiler_params=pltpu.CompilerParams(dimension_semantics=("parallel", "arbitrary", "arbitrary"),
                                             vmem_limit_bytes=MOE_VMEM_LIMIT),
        name="moe_ffn",
    )(counts, f.reshape(n, d), rank_e.reshape(ne, 1, n), rank_e.reshape(ne, n, 1),
      per_expert(gates).reshape(ne, n, 1), wg.astype(BF16), wu.astype(BF16), wd.astype(BF16)).reshape(bsz, l, d)


def _final_kernel(x_ref, y_ref, mod_ref, fg_ref, o_ref):
    x2 = x_ref[0] + mod_ref[0][5:6] * y_ref[0]
    o_ref[0] = _rms(x2) * fg_ref[...]


def _final(x, y, mod, fg):
    bsz, l, d = x.shape
    tok = pl.BlockSpec((1, TM, d), lambda b, t: (b, t, 0))
    return pl.pallas_call(
        _final_kernel,
        out_shape=jax.ShapeDtypeStruct((bsz, l, d), F32),
        grid=(bsz, l // TM),
        in_specs=[tok, tok, pl.BlockSpec((1, 6, d), lambda b, t: (b, 0, 0)), _const_spec((1, d))],
        out_specs=tok,
        compiler_params=_params(("parallel", "parallel")),
        name="final_norm",
    )(x, y, mod, fg.reshape(1, d))


def kernel(x, c, ctx, c_ctx, ada_w, ada_b, norm_g, ev_w_in, ev_conv_w, ev_a_log, ev_dt_bias, ev_gdn_norm_g, ev_q_norm_g, ev_w_uq, ev_kv_norm_g, ev_w_ukv, ev_w_out, ev_ffn_wg, ev_ffn_wu, ev_ffn_wd, od_w_qkv, od_rpb, od_w_out, od_router, od_exp_wg, od_exp_wu, od_exp_wd, final_norm_g):
    bsz, l, d = x.shape
    lc = ctx.shape[1]
    assert ada_w.shape[0] == 2 and bsz < MOD_ROWS and lc % TM == 0 and l % TM == 0
    xa = jnp.concatenate([ctx, x], axis=1)

    cs = jnp.concatenate([c, c_ctx[None], jnp.zeros((MOD_ROWS - bsz - 1, d), F32)], axis=0)
    mods = _ada_mod(cs, ada_w, ada_b).reshape(2, MOD_ROWS, 6, d)

    win, wq, wqr, wk, wv = _proj0_weights(ev_w_in[0], ev_w_uq[0], ev_w_ukv[0])
    qkv_pre, z, ab, q, k, v = _proj0(xa, mods[0], norm_g[0, 0],
                                     (win, ev_q_norm_g[0], wq, wqr, ev_kv_norm_g[0], wk, wv),
                                     _rope_tables(lc, l), lc)
    o_f, o_b = _gdn(qkv_pre, ab, ev_conv_w[0], ev_a_log[0], ev_dt_bias[0], lc)
    yb = _mla_attention(q, k, v, lc)
    xa = _mix0(o_f, o_b, z, yb, xa, mods[0], ev_gdn_norm_g[0], ev_w_out[0], norm_g[0, 1],
               ev_ffn_wg[0], ev_ffn_wu[0], ev_ffn_wd[0], lc)

    q1, k1, v1 = _proj1(xa, mods[1], norm_g[1, 0], od_w_qkv[0], lc)
    o1 = _nat(q1, k1, v1, od_rpb[0], lc)
    tmx = _pick_chunk(l, MOE_TOKENS)
    x1, f1, gates, rank, cnt = _mix1(o1, xa, mods[1], od_w_out[0], norm_g[1, 1], od_router[0], lc, tmx)
    y = _moe(f1, gates, rank, cnt, od_exp_wg[0], od_exp_wu[0], od_exp_wd[0], tmx)
    return _final(x1, y, mods[1], final_norm_g)
```

```python
import functools

import numpy as np
import jax
import jax.numpy as jnp
from jax import lax
from jax.experimental import pallas as pl
from jax.experimental.pallas import tpu as pltpu

F32 = jnp.float32
BF16 = jnp.bfloat16
HIGHEST = lax.Precision.HIGHEST

GRID_W = 64
NORM_EPS = 1e-6
A_HEADS, A_DK, A_DV = 4, 128, 128
GDN_CHUNK_LOG2 = 7
GDN_CHUNK = 1 << GDN_CHUNK_LOG2
GDN_BASE_LOG2 = 3
B_HEADS, B_Q_RANK, B_KV_RANK, B_NOPE, B_ROPE, B_DV = 8, 384, 256, 64, 32, 64
MLA_VROWS = B_DV + 16
ROPE_THETA = 10000.0
C_HEADS, C_DH = 16, 64
WIN_R, WIN_C = 8, 16
N_EXPERTS = 8
LANES = 128
MOD_ROWS = 8
NEG = -1e30
LOG2E = 1.4426950408889634
VMEM_LIMIT = 56 * 1024 * 1024

TM = 256


def _params(sem):
    return pltpu.CompilerParams(dimension_semantics=sem, vmem_limit_bytes=VMEM_LIMIT)


def _const_spec(shape):
    nd = len(shape)
    return pl.BlockSpec(shape, lambda *_: (0,) * nd, pipeline_mode=pl.Buffered(1))


def _rms(x):
    return x * lax.rsqrt(jnp.mean(x * x, axis=-1, keepdims=True) + NORM_EPS)


def _silu(x):
    return x * jax.nn.sigmoid(x)


def _onehot(mask):
    return jnp.where(mask, 1.0, 0.0).astype(BF16)


def _dot(a, b, **kw):
    return jnp.dot(a, b, preferred_element_type=F32, **kw)


def _dot_nt(a, b, **kw):
    return lax.dot_general(a, b, (((1,), (1,)), ((), ())), preferred_element_type=F32, **kw)


def _ada_kernel(s_ref, w_ref, b_ref, o_ref):
    s = _silu(s_ref[...])
    o_ref[0] = _dot(s, w_ref[0], precision=HIGHEST) + b_ref[0]


def _ada_mod(cs, ada_w, ada_b):
    depth, d, d6 = ada_w.shape
    tn = d6 // 4
    return pl.pallas_call(
        _ada_kernel,
        out_shape=jax.ShapeDtypeStruct((depth, MOD_ROWS, d6), F32),
        grid=(depth, d6 // tn),
        in_specs=[
            pl.BlockSpec((MOD_ROWS, d), lambda i, j: (0, 0)),
            pl.BlockSpec((1, d, tn), lambda i, j: (i, 0, j)),
            pl.BlockSpec((1, 1, tn), lambda i, j: (i, 0, j)),
        ],
        out_specs=pl.BlockSpec((1, MOD_ROWS, tn), lambda i, j: (i, 0, j)),
        compiler_params=_params(("parallel", "parallel")),
        name="ada_mod",
    )(cs, ada_w, ada_b.reshape(depth, 1, d6))


N_QKVZ = 4 * A_HEADS * A_DK
IN_OFF_AB = N_QKVZ
IN_OFF_CQ = IN_OFF_AB + LANES
IN_OFF_CKV = IN_OFF_CQ + B_Q_RANK
IN_OFF_KR = IN_OFF_CKV + B_KV_RANK
IN_COLS = IN_OFF_KR + LANES
HW = B_HEADS * LANES


def _joint_tile(c_ref, x_ref, nct):
    return jnp.where(pl.program_id(1) < nct, c_ref[0], x_ref[0])


def _joint_specs(d, nct):
    return [pl.BlockSpec((1, TM, d), lambda b, t: (b, jnp.minimum(t, nct - 1), 0)),
            pl.BlockSpec((1, TM, d), lambda b, t: (b, jnp.maximum(t - nct, 0), 0))]


def _proj0_kernel(c_ref, x_ref, mod_ref, ng_ref, win_ref, qg_ref, wq_ref, wqr_ref, kvg_ref, wk_ref, wv_ref,
                  vone_ref, cq_ref, sq_ref, ck_ref, sk_ref,
                  qkv_ref, z_ref, ab_ref, q_ref, k_ref, vt_ref, *, nct):
    x = _joint_tile(c_ref, x_ref, nct)
    m = mod_ref[0]
    h = _rms(x) * ng_ref[...] * (1.0 + m[1:2]) + m[0:1]
    p = _dot(h.astype(BF16), win_ref[...])
    qkv_ref[0] = p[:, :3 * A_HEADS * A_DK]
    z_ref[0] = p[:, 3 * A_HEADS * A_DK:N_QKVZ]
    ab_ref[0] = p[:, IN_OFF_AB:IN_OFF_CQ]
    nq = (_rms(p[:, IN_OFF_CQ:IN_OFF_CKV]) * qg_ref[...]).astype(BF16)
    nkv = (_rms(p[:, IN_OFF_CKV:IN_OFF_KR]) * kvg_ref[...]).astype(BF16)
    krp = p[:, IN_OFF_KR:IN_COLS]
    qa = _dot(nq, wq_ref[...])
    qb = _dot(nq, wqr_ref[...])
    kn = _dot(nkv, wk_ref[...])
    vv = _dot(nkv, wv_ref[...]) + vone_ref[...]
    cq, sq, ck, sk = cq_ref[...], sq_ref[...], ck_ref[...], sk_ref[...]
    kr = pltpu.roll(krp, B_NOPE, axis=1) * ck + pltpu.roll(krp, B_NOPE - B_ROPE, axis=1) * sk
    for hh in range(B_HEADS):
        sl = slice(hh * LANES, (hh + 1) * LANES)
        q_ref[0, hh] = (qa[:, sl] * cq + qb[:, sl] * sq).astype(BF16)
        k_ref[0, hh] = (kn[:, sl] + kr).astype(BF16)
        vt_ref[0, hh] = vv[:, sl].T[:MLA_VROWS].astype(BF16)


def _rot_cols(w):
    q = B_ROPE // 4
    return jnp.concatenate([-w[:, q:2 * q], w[:, :q], -w[:, 3 * q:], w[:, 2 * q:3 * q]], axis=1)


def _proj0_weights(w_in, w_uq, w_ukv):
    d = w_in.shape[0]
    offs = np.cumsum([0, 512, 512, 512, 512, 8, 8, B_Q_RANK, B_KV_RANK, B_ROPE])
    zeros = lambda n: jnp.zeros((d, n), F32)
    kr = w_in[:, offs[8]:offs[9]]
    win = jnp.concatenate([
        w_in[:, :offs[4]],
        w_in[:, offs[4]:offs[6]], zeros(LANES - 16),
        w_in[:, offs[6]:offs[7]],
        w_in[:, offs[7]:offs[8]],
        kr, _rot_cols(kr), zeros(LANES - 2 * B_ROPE),
    ], axis=1).astype(BF16)
    dq = B_NOPE + B_ROPE
    wq3 = w_uq.reshape(B_Q_RANK, B_HEADS, dq)
    zq = jnp.zeros((B_Q_RANK, B_HEADS, LANES - dq), F32)
    wq = jnp.concatenate([wq3, zq], axis=2).reshape(B_Q_RANK, HW).astype(BF16)
    rot = jnp.stack([_rot_cols(wq3[:, hh, B_NOPE:]) for hh in range(B_HEADS)], axis=1)
    wqr = jnp.concatenate([jnp.zeros((B_Q_RANK, B_HEADS, B_NOPE), F32), rot, zq], axis=2)
    wqr = wqr.reshape(B_Q_RANK, HW).astype(BF16)
    wkv3 = w_ukv.reshape(B_KV_RANK, B_HEADS, B_NOPE + B_DV)
    zk = jnp.zeros((B_KV_RANK, B_HEADS, LANES - B_NOPE), F32)
    wk = jnp.concatenate([wkv3[:, :, :B_NOPE], zk], axis=2).reshape(B_KV_RANK, HW).astype(BF16)
    vpart = wkv3[:, :, B_NOPE:]
    wv = jnp.concatenate([vpart, jnp.zeros_like(vpart)], axis=2).reshape(B_KV_RANK, HW).astype(BF16)
    return win, wq, wqr, wk, wv


def _value_ones():
    vone = np.zeros((1, HW), np.float32)
    for hh in range(B_HEADS):
        vone[0, hh * LANES + B_DV:hh * LANES + MLA_VROWS] = 1.0
    return jnp.asarray(vone)


def _rope_tables(lc, l):
    t = np.arange(l)
    half = B_ROPE // 2
    inv = ROPE_THETA ** (-np.arange(0, half, 2, dtype=np.float64) / half)
    ar = (t // GRID_W)[:, None] * inv[None, :]
    ac = (t % GRID_W)[:, None] * inv[None, :]
    ang = np.concatenate([ar, ar, ac, ac], axis=-1)
    cos = np.concatenate([np.ones((lc, B_ROPE)), np.cos(ang)], axis=0)
    sin = np.concatenate([np.zeros((lc, B_ROPE)), np.sin(ang)], axis=0)
    tt = lc + l
    scale = (B_NOPE + B_ROPE) ** -0.5 * LOG2E
    pad = np.zeros((tt, LANES - B_NOPE - B_ROPE))
    z64 = np.zeros((tt, B_NOPE))
    cq = np.concatenate([np.full((tt, B_NOPE), scale), scale * cos, pad], axis=1)
    sq = np.concatenate([z64, scale * sin, pad], axis=1)
    ck = np.concatenate([z64, cos, pad], axis=1)
    sk = np.concatenate([z64, sin, pad], axis=1)
    return tuple(jnp.asarray(a, F32) for a in (cq, sq, ck, sk))


def _mod_row(b, t, n_ctx_tiles, bsz):
    return jnp.where(t < n_ctx_tiles, bsz, b)


def _proj0(ctx, x, mod, ng, weights, tables):
    bsz, l, d = x.shape
    lc = ctx.shape[1]
    tt = lc + l
    win, qg, wq, wqr, kvg, wk, wv = weights
    vone = _value_ones()
    nct = lc // TM
    tok = lambda w: pl.BlockSpec((1, TM, w), lambda b, t: (b, t, 0))
    head = pl.BlockSpec((1, B_HEADS, TM, LANES), lambda b, t: (b, 0, t, 0))
    head_t = pl.BlockSpec((1, B_HEADS, MLA_VROWS, TM), lambda b, t: (b, 0, 0, t))
    tab = pl.BlockSpec((TM, LANES), lambda b, t: (t, 0))
    hshape = jax.ShapeDtypeStruct((bsz, B_HEADS, tt, LANES), BF16)
    return pl.pallas_call(
        functools.partial(_proj0_kernel, nct=nct),
        out_shape=(
            jax.ShapeDtypeStruct((bsz, tt, 3 * A_HEADS * A_DK), F32),
            jax.ShapeDtypeStruct((bsz, tt, A_HEADS * A_DV), F32),
            jax.ShapeDtypeStruct((bsz, tt, LANES), F32),
            hshape, hshape, jax.ShapeDtypeStruct((bsz, B_HEADS, MLA_VROWS, tt), BF16),
        ),
        grid=(bsz, tt // TM),
        in_specs=_joint_specs(d, nct) + [
            pl.BlockSpec((1, 6, d), lambda b, t: (_mod_row(b, t, nct, bsz), 0, 0)),
            _const_spec((1, d)), _const_spec(win.shape),
            _const_spec((1, B_Q_RANK)), _const_spec(wq.shape), _const_spec(wqr.shape),
            _const_spec((1, B_KV_RANK)), _const_spec(wk.shape), _const_spec(wv.shape),
            _const_spec(vone.shape),
            tab, tab, tab, tab,
        ],
        out_specs=(tok(3 * A_HEADS * A_DK), tok(A_HEADS * A_DV), tok(LANES), head, head, head_t),
        compiler_params=_params(("parallel", "parallel")),
        name="proj0",
    )(ctx, x, mod, ng.reshape(1, d), win, qg.reshape(1, -1), wq, wqr, kvg.reshape(1, -1), wk, wv, vone,
      *tables)


A_CONV = 5
CONV_HALO = 8


def _split_bf16(a):
    hi = a.astype(BF16)
    return hi, (a - hi.astype(F32)).astype(BF16)


def _split3_bf16(a):
    hi, rest = a.astype(BF16), a
    rest = rest - hi.astype(F32)
    mid = rest.astype(BF16)
    return hi, mid, (rest - mid.astype(F32)).astype(BF16)


def _dot3(a, b):
    ah, al = _split_bf16(a)
    bh, bl = _split_bf16(b)
    return _dot(ah, bh) + (_dot(ah, bl) + _dot(al, bh))


def _bdot(a, b):
    return lax.dot_general(a.astype(BF16), b.astype(BF16), (((2,), (1,)), ((0,), (0,))),
                           preferred_element_type=F32)


def _gdn_pre_kernel(x_ref, prev_ref, next_ref, ab_ref, cw_ref, shift_ref, alog_ref, dtb_ref,
                    u_ref, wq_ref, kgt_ref, qk_ref, gl_ref, *, ncc, nc):
    c = GDN_CHUNK
    i = pl.program_id(0)
    bsz = x_ref.shape[0]
    prev_ok = ((i != 0) & (i != ncc)).astype(F32)
    next_ok = ((i != ncc - 1) & (i != nc - 1)).astype(F32)
    cw = cw_ref[...]
    nq = A_HEADS * A_DK
    row = lax.broadcasted_iota(jnp.int32, (c, c), 0)
    col = lax.broadcasted_iota(jnp.int32, (c, c), 1)
    eye = (row == col).astype(F32)
    tri = [((row >= col), (row > col)), ((row <= col), (row < col))]
    ns, rhs, qgs = [], [], []

    def l2n(t):
        return t * lax.rsqrt(jnp.sum(t * t, axis=-1, keepdims=True) + 1e-6)

    for b in range(bsz):
        xe = jnp.concatenate([prev_ref[b] * prev_ok, x_ref[b], next_ref[b] * next_ok], axis=0)
        xw = jnp.concatenate([(xe * cw[j:j + 1]).astype(BF16) for j in range(A_CONV)], axis=0)
        y = _silu(_dot(shift_ref[...], xw))
        ab = ab_ref[b]
        g_all = -jnp.exp(alog_ref[...]) * (jnp.maximum(ab + dtb_ref[...], 0.0)
                                           + jnp.log1p(jnp.exp(-jnp.abs(ab + dtb_ref[...]))))
        beta_all = jax.nn.sigmoid(ab)
        g_parts = _split3_bf16(g_all)
        gt_parts = _split3_bf16(g_all.T)
        g_tot = jnp.sum(g_all, axis=0, keepdims=True)
        heads = []
        for h in range(A_HEADS):
            sl = slice(h * A_DK, (h + 1) * A_DK)
            kh = l2n(y[:, nq + h * A_DK:nq + (h + 1) * A_DK])
            heads.append((l2n(y[:, sl]) * (A_DK ** -0.5), kh, y[:, 2 * nq + h * A_DV:2 * nq + (h + 1) * A_DV],
                          kh.astype(BF16)))
        for d in range(2):
            incl, strict = tri[d]
            m_incl = _onehot(incl)
            gc_col = sum(_dot(m_incl, p) for p in g_parts)
            gc_row = sum(_dot_nt(p, m_incl) for p in gt_parts)
            gl_rows = []
            for h in range(A_HEADS):
                qh, kh, vh, khb = heads[h]
                ln = d * A_HEADS + h
                beta = beta_all[:, 2 * A_HEADS + ln:2 * A_HEADS + ln + 1]
                gcc = gc_col[:, ln:ln + 1]
                gcr = gc_row[ln:ln + 1, :]
                gtot = g_tot[:, ln:ln + 1]
                decay = jnp.where(incl, jnp.exp(jnp.where(incl, gcc - gcr, 0.0)), 0.0)
                n = -jnp.where(strict, beta * _dot_nt(khb, khb) * decay, 0.0)
                eg = jnp.exp(gcc)
                ns.append(n)
                rhs.append(jnp.concatenate([vh * beta, kh * (beta * eg)], axis=1))
                qgs.append(qh * eg)
                kgt_ref[b, d, 0, h] = (kh * jnp.exp(gtot - gcc)).T.astype(BF16)
                qk_ref[b, d, 0, h] = jnp.where(incl, _dot_nt(qh.astype(BF16), khb) * decay, 0.0).astype(BF16)
                gl_rows.append(jnp.broadcast_to(jnp.exp(gtot), (1, LANES)))
            gl_ref[b, d, 0] = jnp.concatenate(gl_rows + gl_rows, axis=0)
    n = jnp.stack(ns, axis=0)
    same = lambda s: lax.shift_right_logical(row, s) == lax.shift_right_logical(col, s)
    nd = jnp.where(same(GDN_BASE_LOG2), n, 0.0)
    x = eye + nd
    p = _bdot(nd, nd)
    z = _bdot(p, jnp.concatenate([p, x], axis=2))
    x = x + z[:, :, c:]
    x = x + _bdot(z[:, :, :c], x)
    for s in range(GDN_BASE_LOG2, GDN_CHUNK_LOG2):
        nl = jnp.where(same(s + 1) & jnp.logical_not(same(s)), n, 0.0)
        x = x + _bdot(_bdot(x, nl), x)
    y = _bdot(x, jnp.stack(rhs, axis=0))
    for b in range(bsz):
        for d in range(2):
            for h in range(A_HEADS):
                k = (b * 2 + d) * A_HEADS + h
                u_ref[b, d, 0, h] = y[k, :, :A_DV]
                wq_ref[b, d, 0, h] = jnp.concatenate([y[k, :, A_DV:], qgs[k]], axis=0).astype(BF16)


def _gdn_scan_kernel(*refs, bsz):
    ins, (of_ref, ob_ref, s_ref) = refs[:10], refs[10:]

    @pl.when(pl.program_id(0) == 0)
    def _():
        s_ref[...] = jnp.zeros_like(s_ref)

    c = GDN_CHUNK
    n = bsz * A_HEADS
    bdot = lambda a, b: lax.dot_general(a, b, (((2,), (1,)), ((0,), (0,))), preferred_element_type=F32)
    for d, o_ref in ((0, of_ref), (1, ob_ref)):
        u_ref, wq_ref, kgt_ref, qk_ref, gl_ref = ins[5 * d:5 * d + 5]
        flat = lambda ref: ref[:, 0, 0].reshape(n, *ref.shape[-2:])
        s = s_ref[:, d].reshape(n, A_DK, A_DV)
        ws = bdot(flat(wq_ref), s.astype(BF16))
        v_new = (flat(u_ref) - ws[:, :c]).astype(BF16)
        o = ws[:, c:] + bdot(flat(qk_ref), v_new)
        gl = gl_ref[:, 0, 0, :A_HEADS, :].reshape(n, 1, LANES)
        s_ref[:, d] = (s * gl + bdot(flat(kgt_ref), v_new)).reshape(bsz, A_HEADS, A_DK, A_DV)
        for b in range(bsz):
            for h in range(A_HEADS):
                o_ref[b, :, h * A_DV:(h + 1) * A_DV] = o[b * A_HEADS + h]


def _gdn(qkv_pre, ab, conv_w, a_log, dt_bias, lc):
    bsz, tt, wqkv = qkv_pre.shape
    c = GDN_CHUNK
    nc, ncc = tt // c, lc // c
    hb = c // CONV_HALO
    per = lambda r, w: pl.BlockSpec((bsz, 2, 1, A_HEADS, r, w), lambda i: (0, 0, i, 0, 0, 0))
    shp = lambda r, w, dt: jax.ShapeDtypeStruct((bsz, 2, nc, A_HEADS, r, w), dt)
    lane_row = lambda t: jnp.concatenate([t.reshape(1, -1), jnp.zeros((1, LANES - t.size), F32)], axis=1)
    cw = jnp.concatenate([conv_w, jnp.zeros((CONV_HALO - A_CONV, wqkv), F32)], axis=0)
    rows = c + 2 * CONV_HALO
    shift = np.zeros((c, A_CONV * rows), np.float32)
    for j in range(A_CONV):
        shift[np.arange(c), j * rows + np.arange(c) + CONV_HALO - A_CONV // 2 + j] = 1.0
    shift = jnp.asarray(shift, BF16)
    pre = pl.pallas_call(
        functools.partial(_gdn_pre_kernel, ncc=ncc, nc=nc),
        out_shape=(shp(c, A_DV, F32), shp(2 * c, A_DK, BF16), shp(A_DK, c, BF16), shp(c, c, BF16),
                   jax.ShapeDtypeStruct((bsz, 2, nc, 2 * A_HEADS, LANES), F32)),
        grid=(nc,),
        in_specs=[pl.BlockSpec((bsz, c, wqkv), lambda i: (0, i, 0)),
                  pl.BlockSpec((bsz, CONV_HALO, wqkv), lambda i: (0, jnp.maximum(i * hb - 1, 0), 0)),
                  pl.BlockSpec((bsz, CONV_HALO, wqkv), lambda i: (0, jnp.minimum((i + 1) * hb, nc * hb - 1), 0)),
                  pl.BlockSpec((bsz, c, LANES), lambda i: (0, i, 0)),
                  _const_spec(cw.shape), _const_spec(shift.shape), _const_spec((1, LANES)),
                  _const_spec((1, LANES))],
        out_specs=(per(c, A_DV), per(2 * c, A_DK), per(A_DK, c), per(c, c),
                   pl.BlockSpec((bsz, 2, 1, 2 * A_HEADS, LANES), lambda i: (0, 0, i, 0, 0))),
        compiler_params=_params(("parallel",)),
        name="gdn_pre",
    )(qkv_pre, qkv_pre, qkv_pre, ab, cw, shift, lane_row(a_log), lane_row(dt_bias))

    def chunk(d, s):
        return s if d == 0 else jnp.where(s < ncc, ncc - 1 - s, nc - 1 - (s - ncc))

    in_specs = []
    for d in range(2):
        for arr in pre[:4]:
            r, w = arr.shape[-2:]
            in_specs.append(pl.BlockSpec((bsz, 1, 1, A_HEADS, r, w),
                                         lambda s, d=d: (0, d, chunk(d, s), 0, 0, 0)))
        in_specs.append(pl.BlockSpec((bsz, 1, 1, 2 * A_HEADS, LANES), lambda s, d=d: (0, d, chunk(d, s), 0, 0)))
    o_shape = jax.ShapeDtypeStruct((bsz, tt, A_HEADS * A_DV), F32)
    return pl.pallas_call(
        functools.partial(_gdn_scan_kernel, bsz=bsz),
        out_shape=(o_shape, o_shape),
        grid=(nc,),
        in_specs=in_specs,
        out_specs=tuple(pl.BlockSpec((bsz, c, A_HEADS * A_DV), lambda s, d=d: (0, chunk(d, s), 0))
                        for d in range(2)),
        scratch_shapes=[pltpu.VMEM((bsz, 2, A_HEADS, A_DK, A_DV), F32)],
        compiler_params=_params(("arbitrary",)),
        name="gdn_scan",
    )(*pre, *pre)


def _pick_chunk(n, options):
    for o in options:
        if n % o == 0:
            return o
    raise ValueError(f"no chunk size in {options} divides {n}")


def _mla_kernel(q_ref, k_ref, vt_ref, o_ref, sa_ref, sb_ref, pa_ref, pb_ref, *, lc, tq, tk, nk):
    t = pl.program_id(2)
    g = q_ref.shape[1]
    bnt = lambda a, b: lax.dot_general(a, b, (((2,), (2,)), ((0,), (0,))), preferred_element_type=F32)
    bnn = lambda a, b: lax.dot_general(a, b, (((2,), (1,)), ((0,), (0,))), preferred_element_type=F32)

    def absorb(s, m, acc, vt_chunk):
        m_new = jnp.maximum(m, jnp.max(s, axis=1, keepdims=True))
        p = jnp.exp2(s - m_new).astype(BF16)
        return m_new, jnp.exp2(m - m_new) * acc + bnn(vt_chunk, p)

    def update(q, m, acc, k_chunk, vt_chunk):
        return absorb(bnt(k_chunk, q), m, acc, vt_chunk)

    def finish(acc):
        parts = [acc[h, :B_DV] / acc[h, B_DV:B_DV + 1] for h in range(g)]
        o_ref[0] = jnp.concatenate(parts, axis=0).T

    init = (jnp.full((g, 1, tq), -jnp.inf, F32), jnp.zeros((g, MLA_VROWS, tq), F32))

    @pl.when(t < lc // tq)
    def _():
        finish(update(q_ref[0], *init, k_ref[0, :, :lc, :], vt_ref[0, :, :, :lc])[1])

    @pl.when(t >= lc // tq)
    def _():
        q = q_ref[0]

        s_bufs, p_bufs = (sa_ref, sb_ref), (pa_ref, pb_ref)
        chunk = lambda c: pl.ds(pl.multiple_of(c * tk, tk), tk)

        def stage_s(c, par):
            s_bufs[par][...] = bnt(k_ref[0, :, chunk(c), :], q)

        def stage_x(par, m):
            s = s_bufs[par][...]
            m_new = jnp.maximum(m, jnp.max(s, axis=1, keepdims=True))
            p_bufs[par][...] = jnp.exp2(s - m_new).astype(BF16)
            return m_new, jnp.exp2(m - m_new)

        def stage_v(c, par, alpha, acc):
            return alpha * acc + bnn(vt_ref[0, :, :, chunk(c)], p_bufs[par][...])

        def step(c, par, state, do_s=True, do_x=True):
            m, alpha, acc = state
            if do_s:
                stage_s(c + 2, par)
            if do_x:
                m_next, alpha_next = stage_x(1 - par, m)
            else:
                m_next, alpha_next = m, alpha
            return m_next, alpha_next, stage_v(c, par, alpha, acc)

        m0, acc0 = init
        stage_s(0, 0)
        if nk > 1:
            stage_s(1, 1)
        m1, alpha1 = stage_x(0, m0)
        state = (m1, alpha1, acc0)
        pairs = max(nk - 2, 0) // 2

        def body(i, state):
            return step(2 * i + 1, 1, step(2 * i, 0, state))

        state = lax.fori_loop(0, pairs, body, state)
        for c in range(2 * pairs, nk):
            state = step(c, c % 2, state, do_s=c + 2 < nk, do_x=c + 1 < nk)
        finish(state[2])


MLA_G = 8
MLA_VMEM_LIMIT = 60 * 1024 * 1024


def _mla_attention(q, k, vt, lc):
    bsz, nh, tt, _ = q.shape
    tq = TM
    tk = _pick_chunk(tt, (768, 512, 256))
    g = MLA_G
    kern = functools.partial(_mla_kernel, lc=lc, tq=tq, tk=tk, nk=tt // tk)
    return pl.pallas_call(
        kern,
        out_shape=jax.ShapeDtypeStruct((bsz, tt, nh * B_DV), F32),
        grid=(bsz, nh // g, tt // tq),
        in_specs=[pl.BlockSpec((1, g, tq, LANES), lambda b, hp, t: (b, hp, t, 0)),
                  pl.BlockSpec((1, g, tt, LANES), lambda b, hp, t: (b, hp, 0, 0), pipeline_mode=pl.Buffered(1)),
                  pl.BlockSpec((1, g, MLA_VROWS, tt), lambda b, hp, t: (b, hp, 0, 0),
                               pipeline_mode=pl.Buffered(1))],
        out_specs=pl.BlockSpec((1, tq, g * B_DV), lambda b, hp, t: (b, t, hp)),
        scratch_shapes=[pltpu.VMEM((g, tk, tq), F32), pltpu.VMEM((g, tk, tq), F32),
                        pltpu.VMEM((g, tk, tq), BF16), pltpu.VMEM((g, tk, tq), BF16)],
        compiler_params=pltpu.CompilerParams(dimension_semantics=("parallel", "parallel", "parallel"),
                                             vmem_limit_bytes=MLA_VMEM_LIMIT),
        name="mla_attention",
    )(q, k, vt)


def _mix0_kernel(of_ref, ob_ref, z_ref, yb_ref, c_ref, x_ref, mod_ref, gg_ref, woa_ref, wob_ref, ng_ref,
                 wg_ref, wu_ref, wd_ref, o_ref, *, nct):
    o = of_ref[0] + ob_ref[0]
    z = z_ref[0]
    gg = gg_ref[...]
    parts = []
    for h in range(A_HEADS):
        sl = slice(h * A_DV, (h + 1) * A_DV)
        parts.append((_rms(o[:, sl]) * gg * _silu(z[:, sl])).astype(BF16))
    ya = jnp.concatenate(parts, axis=1)
    y = _dot(ya, woa_ref[...]) + _dot(yb_ref[0].astype(BF16), wob_ref[...])
    m = mod_ref[0]
    x1 = _joint_tile(c_ref, x_ref, nct) + m[2:3] * y
    f = (_rms(x1) * ng_ref[...] * (1.0 + m[4:5]) + m[3:4]).astype(BF16)
    hid = _silu(_dot(f, wg_ref[...])) * _dot(f, wu_ref[...])
    o_ref[0] = x1 + m[5:6] * _dot(hid.astype(BF16), wd_ref[...])


def _mix0(o_f, o_b, z, yb, ctx, x, mod, gdn_g, w_out, ng, wg, wu, wd):
    bsz, l, d = x.shape
    lc = ctx.shape[1]
    tt = lc + l
    nct = lc // TM
    na = A_HEADS * A_DV
    woa, wob = w_out[:na].astype(BF16), w_out[na:].astype(BF16)
    wg, wu, wd = wg.astype(BF16), wu.astype(BF16), wd.astype(BF16)
    tok = lambda w: pl.BlockSpec((1, TM, w), lambda b, t: (b, t, 0))
    return pl.pallas_call(
        functools.partial(_mix0_kernel, nct=nct),
        out_shape=jax.ShapeDtypeStruct((bsz, tt, d), F32),
        grid=(bsz, tt // TM),
        in_specs=[tok(na), tok(na), tok(na), tok(B_HEADS * B_DV)] + _joint_specs(d, nct) + [
            pl.BlockSpec((1, 6, d), lambda b, t: (_mod_row(b, t, nct, bsz), 0, 0)),
            _const_spec((1, A_DV)), _const_spec(woa.shape), _const_spec(wob.shape), _const_spec((1, d)),
            _const_spec(wg.shape), _const_spec(wu.shape), _const_spec(wd.shape),
        ],
        out_specs=tok(d),
        compiler_params=_params(("parallel", "parallel")),
        name="mix0_ffn",
    )(o_f, o_b, z, yb, ctx, x, mod, gdn_g.reshape(1, -1), woa, wob, ng.reshape(1, d), wg, wu, wd)


def _proj1_kernel(x_ref, mod_ref, ng_ref, w_ref, q_ref, k_ref, v_ref):
    m = mod_ref[0]
    h = (_rms(x_ref[0]) * ng_ref[...] * (1.0 + m[1:2]) + m[0:1]).astype(BF16)
    p = _dot(h, w_ref[...])
    d = q_ref.shape[-1]
    q_ref[0] = (p[:, :d] * (C_DH ** -0.5 * LOG2E)).astype(BF16)
    k_ref[0] = p[:, d:2 * d].astype(BF16)
    v_ref[0] = p[:, 2 * d:].astype(BF16)


def _proj1(xa, mod, ng, w_qkv, lc):
    bsz, tt, d = xa.shape
    nct = lc // TM
    w = w_qkv.astype(BF16)
    tok = pl.BlockSpec((1, TM, d), lambda b, t: (b, t, 0))
    shp = jax.ShapeDtypeStruct((bsz, tt, d), BF16)
    return pl.pallas_call(
        _proj1_kernel,
        out_shape=(shp, shp, shp),
        grid=(bsz, tt // TM),
        in_specs=[tok, pl.BlockSpec((1, 6, d), lambda b, t: (_mod_row(b, t, nct, bsz), 0, 0)),
                  _const_spec((1, d)), _const_spec(w.shape)],
        out_specs=(tok, tok, tok),
        compiler_params=_params(("parallel", "parallel")),
        name="proj1",
    )(xa, mod, ng.reshape(1, d), w)


NAT_HG = 4
NAT_KEYS = WIN_R * GRID_W


def _nat_bias(rpb):
    qc = np.arange(GRID_W)[:, None]
    kc = np.arange(GRID_W)[None, :]
    c0 = np.clip(qc - WIN_C // 2, 0, GRID_W - WIN_C)
    inside = (kc >= c0) & (kc < c0 + WIN_C)
    dc = kc - qc + (WIN_C - 1)
    onehot = ((np.arange(2 * WIN_C - 1)[:, None, None] == dc[None]) & inside[None]).astype(np.float32)
    tab = jnp.einsum('hrd,dqk->hqrk', rpb, jnp.asarray(onehot), precision=HIGHEST)
    tab = tab + jnp.asarray(np.where(inside, 0.0, NEG).astype(np.float32))[None, :, None, :]
    tab = jnp.stack([tab[:, :, WIN_R - 1 - v:2 * WIN_R - 1 - v, :] for v in range(WIN_R)], axis=0)
    return tab.reshape(WIN_R, rpb.shape[0], GRID_W, NAT_KEYS)


NAT_RB = 4


def _nat_kernel(q_ref, k_ref, v_ref, bias_ref, o_ref, *, lc, rows):
    i = pl.program_id(2)
    kc = k_ref[0, :lc, :]
    vc = v_ref[0, :lc, :]
    lane = lax.broadcasted_iota(jnp.int32, (GRID_W, NAT_HG * C_DH), 1)
    masks = [(lane >= h * C_DH) & (lane < (h + 1) * C_DH) for h in range(NAT_HG)]
    for j in range(NAT_RB):
        r = i * NAT_RB + j
        r0 = jnp.clip(r - WIN_R // 2, 0, rows - WIN_R)
        start = pl.multiple_of(lc + r0 * GRID_W, GRID_W)
        kw = k_ref[0, pl.ds(start, NAT_KEYS), :]
        vw = v_ref[0, pl.ds(start, NAT_KEYS), :]
        q = q_ref[0, j * GRID_W:(j + 1) * GRID_W, :]
        qs = jnp.concatenate([jnp.where(hm, q, jnp.zeros_like(q)) for hm in masks], axis=0)
        s1 = _dot_nt(qs, kw) + bias_ref[r - r0, 0].reshape(NAT_HG * GRID_W, NAT_KEYS)
        s2 = _dot_nt(qs, kc)
        m = jnp.maximum(jnp.max(s1, axis=-1, keepdims=True), jnp.max(s2, axis=-1, keepdims=True))
        p1 = jnp.exp2(s1 - m)
        p2 = jnp.exp2(s2 - m)
        l = jnp.sum(p1, axis=-1, keepdims=True) + jnp.sum(p2, axis=-1, keepdims=True)
        o = (_dot(p1.astype(BF16), vw) + _dot(p2.astype(BF16), vc)) / l
        acc = jnp.zeros((GRID_W, NAT_HG * C_DH), F32)
        for h, hm in enumerate(masks):
            acc = jnp.where(hm, o[h * GRID_W:(h + 1) * GRID_W], acc)
        o_ref[0, j * GRID_W:(j + 1) * GRID_W, :] = acc.astype(BF16)


def _nat(q, k, v, rpb, lc):
    bsz, tt, d = q.shape
    l = tt - lc
    rows = l // GRID_W
    assert rows >= WIN_R and rows % NAT_RB == 0 and lc % (NAT_RB * GRID_W) == 0
    hw = NAT_HG * C_DH
    ng = d // hw
    bias = (_nat_bias(rpb) * LOG2E).reshape(WIN_R, ng, NAT_HG, GRID_W, NAT_KEYS)
    kern = functools.partial(_nat_kernel, lc=lc, rows=rows)
    qb = NAT_RB * GRID_W
    kv = pl.BlockSpec((1, tt, hw), lambda b, g, r: (b, 0, g))
    return pl.pallas_call(
        kern,
        out_shape=jax.ShapeDtypeStruct((bsz, l, d), BF16),
        grid=(bsz, ng, rows // NAT_RB),
        in_specs=[
            pl.BlockSpec((1, qb, hw), lambda b, g, r: (b, lc // qb + r, g)),
            kv, kv,
            pl.BlockSpec((WIN_R, 1, NAT_HG, GRID_W, NAT_KEYS), lambda b, g, r: (0, g, 0, 0, 0)),
        ],
        out_specs=pl.BlockSpec((1, qb, hw), lambda b, g, r: (b, r, g)),
        compiler_params=_params(("parallel", "parallel", "arbitrary")),
        name="nat",
    )(q, k, v, bias)


def _mix1_kernel(o_ref, x_ref, mod_ref, wo_ref, ng_ref, wr_ref, x_out, f_out, w_out, rank_out, cnt_out,
                 carry_ref, *, group):
    t = pl.program_id(1)
    m = mod_ref[0]
    x1 = x_ref[0] + m[2:3] * _dot(o_ref[0], wo_ref[...])
    x_out[0] = x1
    f = _rms(x1) * ng_ref[...] * (1.0 + m[4:5]) + m[3:4]
    f_out[0] = f.astype(BF16)
    logits = _dot3(f, wr_ref[...])
    lane = lax.broadcasted_iota(jnp.int32, logits.shape, 1)
    logits = jnp.where(lane < N_EXPERTS, logits, -jnp.inf)
    m1 = jnp.max(logits, axis=-1, keepdims=True)
    i1 = jnp.min(jnp.where(logits == m1, lane, LANES), axis=-1, keepdims=True)
    rest = jnp.where(lane == i1, -jnp.inf, logits)
    m2 = jnp.max(rest, axis=-1, keepdims=True)
    i2 = jnp.min(jnp.where(rest == m2, lane, LANES), axis=-1, keepdims=True)
    e2 = jnp.exp(m2 - m1)
    g1 = 1.0 / (1.0 + e2)
    w_out[0] = jnp.where(lane == i1, g1, 0.0) + jnp.where(lane == i2, e2 * g1, 0.0)

    @pl.when(t % group == 0)
    def _():
        carry_ref[...] = jnp.zeros_like(carry_ref)

    sel = (lane == i1) | (lane == i2)
    tm = logits.shape[0]
    tri = lax.broadcasted_iota(jnp.int32, (tm, tm), 0) >= lax.broadcasted_iota(jnp.int32, (tm, tm), 1)
    csum = _dot(_onehot(tri), _onehot(sel))
    carry = carry_ref[...]
    rank_out[0] = jnp.where(sel, carry + csum - 1.0, -1.0)
    carry = carry + csum[tm - 1:tm, :]
    carry_ref[...] = carry
    cnt_out[0, 0] = carry


def _mix1(o, xa, mod, w_out, ng, w_router, lc, tmx):
    bsz, tt, d = xa.shape
    l = tt - lc
    nct = lc // TM
    group = tmx // TM
    wr = jnp.concatenate([w_router, jnp.zeros((d, LANES - N_EXPERTS), F32)], axis=1)
    tok = lambda w: pl.BlockSpec((1, TM, w), lambda b, t: (b, t, 0))
    return pl.pallas_call(
        functools.partial(_mix1_kernel, group=group),
        out_shape=(jax.ShapeDtypeStruct((bsz, l, d), F32), jax.ShapeDtypeStruct((bsz, l, d), BF16),
                   jax.ShapeDtypeStruct((bsz, l, LANES), F32), jax.ShapeDtypeStruct((bsz, l, LANES), F32),
                   jax.ShapeDtypeStruct((bsz, l // tmx, 1, LANES), F32)),
        grid=(bsz, l // TM),
        in_specs=[tok(d), pl.BlockSpec((1, TM, d), lambda b, t: (b, t + nct, 0)),
                  pl.BlockSpec((1, 6, d), lambda b, t: (b, 0, 0)),
                  _const_spec((d, d)), _const_spec((1, d)), _const_spec(wr.shape)],
        out_specs=(tok(d), tok(d), tok(LANES), tok(LANES),
                   pl.BlockSpec((1, 1, 1, LANES), lambda b, t: (b, t // group, 0, 0))),
        scratch_shapes=[pltpu.VMEM((1, LANES), F32)],
        compiler_params=_params(("parallel", "arbitrary")),
        name="mix1_router",
    )(o, xa, mod, w_out.astype(BF16), ng.reshape(1, d), wr)


MOE_TOKENS = (2048, 1024)
MOE_GATHER = 256
MOE_ROWS = 128
MOE_BIG = 4
MOE_FF = 512
MOE_COMBINE_ROWS = 512
MOE_VMEM_LIMIT = 60 * 1024 * 1024


def _moe_kernel(cnt_ref, f_ref, rank_t_ref, rank_c_ref, gate_ref, wg_ref, wu_ref, wd_ref, o_ref, xs_ref, y_ref):
    i, e, c = pl.program_id(0), pl.program_id(1), pl.program_id(2)
    n = cnt_ref[i * pl.num_programs(1) + e]
    n_rows = (n + MOE_ROWS - 1) // MOE_ROWS
    n_gather = (n + MOE_GATHER - 1) // MOE_GATHER
    tmx = f_ref.shape[0]

    @pl.when((e == 0) & (c == 0))
    def _():
        o_ref[...] = jnp.zeros_like(o_ref)

    @pl.when(c == 0)
    def _():
        rank_t = rank_t_ref[0]

        def gather(s, carry):
            base = pl.multiple_of(s * MOE_GATHER, MOE_GATHER)
            slot = lax.broadcasted_iota(jnp.int32, (MOE_GATHER, tmx), 0) + base
            sel = _onehot(rank_t == slot)
            xs_ref[pl.ds(base, MOE_GATHER), :] = _dot(sel, f_ref[...]).astype(BF16)
            y_ref[pl.ds(base, MOE_GATHER), :] = jnp.zeros((MOE_GATHER, y_ref.shape[1]), F32)
            return carry

        lax.fori_loop(0, n_gather, gather, 0)

    def expert(base, rows):
        x = xs_ref[pl.ds(base, rows), :]
        hid = _silu(_dot(x, wg_ref[0])) * _dot(x, wu_ref[0])
        y_ref[pl.ds(base, rows), :] += _dot(hid.astype(BF16), wd_ref[0])

    big = MOE_ROWS * MOE_BIG

    def big_block(s, carry):
        expert(pl.multiple_of(s * big, big), big)
        return carry

    lax.fori_loop(0, n_rows // MOE_BIG, big_block, 0)
    done = (n_rows // MOE_BIG) * MOE_BIG
    part = MOE_BIG // 2
    while part >= 1:
        take = ((n_rows - done) // part) > 0

        @pl.when(take)
        def _(done=done, part=part):
            expert(pl.multiple_of(done * MOE_ROWS, MOE_ROWS), part * MOE_ROWS)

        done = done + jnp.where(take, part, 0)
        part //= 2

    @pl.when(c == pl.num_programs(2) - 1)
    def _():
        def combine(s, carry):
            base = pl.multiple_of(s * MOE_GATHER, MOE_GATHER)
            ys = y_ref[pl.ds(base, MOE_GATHER), :].astype(BF16)
            slot = lax.broadcasted_iota(jnp.int32, (MOE_COMBINE_ROWS, MOE_GATHER), 1) + base
            for q in range(tmx // MOE_COMBINE_ROWS):
                rows = slice(q * MOE_COMBINE_ROWS, (q + 1) * MOE_COMBINE_ROWS)
                sel = _onehot(rank_c_ref[0, rows, :] == slot)
                o_ref[rows, :] += gate_ref[0, rows, :] * _dot(sel, ys)
            return carry

        lax.fori_loop(0, n_gather, combine, 0)


def _moe(f, gates, rank, cnt, wg, wu, wd, tmx):
    bsz, l, d = f.shape
    ne, _, dff = wg.shape
    n = bsz * l
    nt = n // tmx
    per_expert = lambda a: jnp.transpose(a.reshape(n, LANES)[:, :ne])
    rank_e = per_expert(rank).astype(jnp.int32)
    counts = cnt.reshape(nt, LANES)[:, :ne].astype(jnp.int32).reshape(nt * ne)
    grid_spec = pltpu.PrefetchScalarGridSpec(
        num_scalar_prefetch=1,
        grid=(nt, ne, dff // MOE_FF),
        in_specs=[
            pl.BlockSpec((tmx, d), lambda i, e, c, cnt: (i, 0)),
            pl.BlockSpec((1, 1, tmx), lambda i, e, c, cnt: (e, 0, i)),
            pl.BlockSpec((1, tmx, 1), lambda i, e, c, cnt: (e, i, 0)),
            pl.BlockSpec((1, tmx, 1), lambda i, e, c, cnt: (e, i, 0)),
            pl.BlockSpec((1, d, MOE_FF), lambda i, e, c, cnt: (e, 0, c)),
            pl.BlockSpec((1, d, MOE_FF), lambda i, e, c, cnt: (e, 0, c)),
            pl.BlockSpec((1, MOE_FF, d), lambda i, e, c, cnt: (e, c, 0)),
        ],
        out_specs=pl.BlockSpec((tmx, d), lambda i, e, c, cnt: (i, 0)),
        scratch_shapes=[pltpu.VMEM((tmx, d), BF16), pltpu.VMEM((tmx, d), F32)],
    )
    return pl.pallas_call(
        _moe_kernel,
        out_shape=jax.ShapeDtypeStruct((n, d), F32),
        grid_spec=grid_spec,
        compiler_params=pltpu.CompilerParams(dimension_semantics=("parallel", "arbitrary", "arbitrary"),
                                             vmem_limit_bytes=MOE_VMEM_LIMIT),
        name="moe_ffn",
    )(counts, f.reshape(n, d), rank_e.reshape(ne, 1, n), rank_e.reshape(ne, n, 1),
      per_expert(gates).reshape(ne, n, 1), wg.astype(BF16), wu.astype(BF16), wd.astype(BF16)).reshape(bsz, l, d)


def _final_kernel(x_ref, y_ref, mod_ref, fg_ref, o_ref):
    x2 = x_ref[0] + mod_ref[0][5:6] * y_ref[0]
    o_ref[0] = _rms(x2) * fg_ref[...]


def _final(x, y, mod, fg):
    bsz, l, d = x.shape
    tm = _pick_chunk(l, (1024, 512, 256))
    tok = pl.BlockSpec((1, tm, d), lambda b, t: (b, t, 0))
    return pl.pallas_call(
        _final_kernel,
        out_shape=jax.ShapeDtypeStruct((bsz, l, d), F32),
        grid=(bsz, l // tm),
        in_specs=[tok, tok, pl.BlockSpec((1, 6, d), lambda b, t: (b, 0, 0)), _const_spec((1, d))],
        out_specs=tok,
        compiler_params=_params(("parallel", "parallel")),
        name="final_norm",
    )(x, y, mod, fg.reshape(1, d))


def kernel(x, c, ctx, c_ctx, ada_w, ada_b, norm_g, ev_w_in, ev_conv_w, ev_a_log, ev_dt_bias, ev_gdn_norm_g, ev_q_norm_g, ev_w_uq, ev_kv_norm_g, ev_w_ukv, ev_w_out, ev_ffn_wg, ev_ffn_wu, ev_ffn_wd, od_w_qkv, od_rpb, od_w_out, od_router, od_exp_wg, od_exp_wu, od_exp_wd, final_norm_g):
    bsz, l, d = x.shape
    lc = ctx.shape[1]
    assert ada_w.shape[0] == 2 and bsz < MOD_ROWS and lc % TM == 0 and l % TM == 0
    cs = jnp.concatenate([c, c_ctx[None], jnp.zeros((MOD_ROWS - bsz - 1, d), F32)], axis=0)
    mods = _ada_mod(cs, ada_w, ada_b).reshape(2, MOD_ROWS, 6, d)

    win, wq, wqr, wk, wv = _proj0_weights(ev_w_in[0], ev_w_uq[0], ev_w_ukv[0])
    qkv_pre, z, ab, q, k, v = _proj0(ctx, x, mods[0], norm_g[0, 0],
                                     (win, ev_q_norm_g[0], wq, wqr, ev_kv_norm_g[0], wk, wv),
                                     _rope_tables(lc, l))
    o_f, o_b = _gdn(qkv_pre, ab, ev_conv_w[0], ev_a_log[0], ev_dt_bias[0], lc)
    yb = _mla_attention(q, k, v, lc)
    xa = _mix0(o_f, o_b, z, yb, ctx, x, mods[0], ev_gdn_norm_g[0], ev_w_out[0], norm_g[0, 1],
               ev_ffn_wg[0], ev_ffn_wu[0], ev_ffn_wd[0])

    q1, k1, v1 = _proj1(xa, mods[1], norm_g[1, 0], od_w_qkv[0], lc)
    o1 = _nat(q1, k1, v1, od_rpb[0], lc)
    tmx = _pick_chunk(l, MOE_TOKENS)
    x1, f1, gates, rank, cnt = _mix1(o1, xa, mods[1], od_w_out[0], norm_g[1, 1], od_router[0], lc, tmx)
    y = _moe(f1, gates, rank, cnt, od_exp_wg[0], od_exp_wu[0], od_exp_wd[0], tmx)
    return _final(x1, y, mods[1], final_norm_g)
```

```python
import functools

import numpy as np
import jax
import jax.numpy as jnp
from jax import lax
from jax.experimental import pallas as pl
from jax.experimental.pallas import tpu as pltpu

F32 = jnp.float32
BF16 = jnp.bfloat16
HIGHEST = lax.Precision.HIGHEST

GRID_W = 64
NORM_EPS = 1e-6
A_HEADS, A_DK, A_DV = 4, 128, 128
GDN_CHUNK_LOG2 = 7
GDN_CHUNK = 1 << GDN_CHUNK_LOG2
GDN_BASE_LOG2 = 3
B_HEADS, B_Q_RANK, B_KV_RANK, B_NOPE, B_ROPE, B_DV = 8, 384, 256, 64, 32, 64
MLA_VROWS = B_DV + 16
ROPE_THETA = 10000.0
C_HEADS, C_DH = 16, 64
WIN_R, WIN_C = 8, 16
N_EXPERTS = 8
LANES = 128
MOD_ROWS = 8
NEG = -1e30
LOG2E = 1.4426950408889634
VMEM_LIMIT = 56 * 1024 * 1024

TM = 256


def _params(sem):
    return pltpu.CompilerParams(dimension_semantics=sem, vmem_limit_bytes=VMEM_LIMIT)


def _const_spec(shape):
    nd = len(shape)
    return pl.BlockSpec(shape, lambda *_: (0,) * nd, pipeline_mode=pl.Buffered(1))


def _rms(x):
    return x * lax.rsqrt(jnp.mean(x * x, axis=-1, keepdims=True) + NORM_EPS)


def _silu(x):
    return x * jax.nn.sigmoid(x)


def _onehot(mask):
    return jnp.where(mask, 1.0, 0.0).astype(BF16)


def _dot(a, b, **kw):
    return jnp.dot(a, b, preferred_element_type=F32, **kw)


def _dot_nt(a, b, **kw):
    return lax.dot_general(a, b, (((1,), (1,)), ((), ())), preferred_element_type=F32, **kw)


def _ada_kernel(s_ref, w_ref, b_ref, o_ref):
    s = _silu(s_ref[...])
    o_ref[0] = _dot(s, w_ref[0], precision=HIGHEST) + b_ref[0]


def _ada_mod(cs, ada_w, ada_b):
    depth, d, d6 = ada_w.shape
    tn = d6 // 4
    return pl.pallas_call(
        _ada_kernel,
        out_shape=jax.ShapeDtypeStruct((depth, MOD_ROWS, d6), F32),
        grid=(depth, d6 // tn),
        in_specs=[
            pl.BlockSpec((MOD_ROWS, d), lambda i, j: (0, 0)),
            pl.BlockSpec((1, d, tn), lambda i, j: (i, 0, j)),
            pl.BlockSpec((1, 1, tn), lambda i, j: (i, 0, j)),
        ],
        out_specs=pl.BlockSpec((1, MOD_ROWS, tn), lambda i, j: (i, 0, j)),
        compiler_params=_params(("parallel", "parallel")),
        name="ada_mod",
    )(cs, ada_w, ada_b.reshape(depth, 1, d6))


N_QKVZ = 4 * A_HEADS * A_DK
IN_OFF_AB = N_QKVZ
IN_OFF_CQ = IN_OFF_AB + LANES
IN_OFF_CKV = IN_OFF_CQ + B_Q_RANK
IN_OFF_KR = IN_OFF_CKV + B_KV_RANK
IN_COLS = IN_OFF_KR + LANES
HW = B_HEADS * LANES


def _joint_tile(c_ref, x_ref, nct):
    return jnp.where(pl.program_id(1) < nct, c_ref[0], x_ref[0])


def _joint_specs(d, nct):
    return [pl.BlockSpec((1, TM, d), lambda b, t: (b, jnp.minimum(t, nct - 1), 0)),
            pl.BlockSpec((1, TM, d), lambda b, t: (b, jnp.maximum(t - nct, 0), 0))]


def _proj0_kernel(c_ref, x_ref, mod_ref, ng_ref, win_ref, qg_ref, wq_ref, wqr_ref, kvg_ref, wk_ref, wv_ref,
                  vone_ref, cq_ref, sq_ref, ck_ref, sk_ref,
                  qkv_ref, z_ref, ab_ref, q_ref, k_ref, vt_ref, *, nct):
    x = _joint_tile(c_ref, x_ref, nct)
    m = mod_ref[0]
    h = _rms(x) * ng_ref[...] * (1.0 + m[1:2]) + m[0:1]
    p = _dot(h.astype(BF16), win_ref[...])
    qkv_ref[0] = p[:, :3 * A_HEADS * A_DK]
    z_ref[0] = p[:, 3 * A_HEADS * A_DK:N_QKVZ]
    ab_ref[0] = p[:, IN_OFF_AB:IN_OFF_CQ]
    nq = (_rms(p[:, IN_OFF_CQ:IN_OFF_CKV]) * qg_ref[...]).astype(BF16)
    nkv = (_rms(p[:, IN_OFF_CKV:IN_OFF_KR]) * kvg_ref[...]).astype(BF16)
    krp = p[:, IN_OFF_KR:IN_COLS]
    qa = _dot(nq, wq_ref[...])
    qb = _dot(nq, wqr_ref[...])
    kn = _dot(nkv, wk_ref[...])
    vv = _dot(nkv, wv_ref[...]) + vone_ref[...]
    cq, sq, ck, sk = cq_ref[...], sq_ref[...], ck_ref[...], sk_ref[...]
    kr = pltpu.roll(krp, B_NOPE, axis=1) * ck + pltpu.roll(krp, B_NOPE - B_ROPE, axis=1) * sk
    for hh in range(B_HEADS):
        sl = slice(hh * LANES, (hh + 1) * LANES)
        q_ref[0, hh] = (qa[:, sl] * cq + qb[:, sl] * sq).astype(BF16)
        k_ref[0, hh] = (kn[:, sl] + kr).astype(BF16)
        vt_ref[0, hh] = vv[:, sl].T[:MLA_VROWS].astype(BF16)


def _rot_cols(w):
    q = B_ROPE // 4
    return jnp.concatenate([-w[:, q:2 * q], w[:, :q], -w[:, 3 * q:], w[:, 2 * q:3 * q]], axis=1)


def _proj0_weights(w_in, w_uq, w_ukv):
    d = w_in.shape[0]
    offs = np.cumsum([0, 512, 512, 512, 512, 8, 8, B_Q_RANK, B_KV_RANK, B_ROPE])
    zeros = lambda n: jnp.zeros((d, n), F32)
    kr = w_in[:, offs[8]:offs[9]]
    win = jnp.concatenate([
        w_in[:, :offs[4]],
        w_in[:, offs[4]:offs[6]], zeros(LANES - 16),
        w_in[:, offs[6]:offs[7]],
        w_in[:, offs[7]:offs[8]],
        kr, _rot_cols(kr), zeros(LANES - 2 * B_ROPE),
    ], axis=1).astype(BF16)
    dq = B_NOPE + B_ROPE
    wq3 = w_uq.reshape(B_Q_RANK, B_HEADS, dq)
    zq = jnp.zeros((B_Q_RANK, B_HEADS, LANES - dq), F32)
    wq = jnp.concatenate([wq3, zq], axis=2).reshape(B_Q_RANK, HW).astype(BF16)
    rot = jnp.stack([_rot_cols(wq3[:, hh, B_NOPE:]) for hh in range(B_HEADS)], axis=1)
    wqr = jnp.concatenate([jnp.zeros((B_Q_RANK, B_HEADS, B_NOPE), F32), rot, zq], axis=2)
    wqr = wqr.reshape(B_Q_RANK, HW).astype(BF16)
    wkv3 = w_ukv.reshape(B_KV_RANK, B_HEADS, B_NOPE + B_DV)
    zk = jnp.zeros((B_KV_RANK, B_HEADS, LANES - B_NOPE), F32)
    wk = jnp.concatenate([wkv3[:, :, :B_NOPE], zk], axis=2).reshape(B_KV_RANK, HW).astype(BF16)
    vpart = wkv3[:, :, B_NOPE:]
    wv = jnp.concatenate([vpart, jnp.zeros_like(vpart)], axis=2).reshape(B_KV_RANK, HW).astype(BF16)
    return win, wq, wqr, wk, wv


def _value_ones():
    vone = np.zeros((1, HW), np.float32)
    for hh in range(B_HEADS):
        vone[0, hh * LANES + B_DV:hh * LANES + MLA_VROWS] = 1.0
    return jnp.asarray(vone)


def _rope_tables(lc, l):
    t = np.arange(l)
    half = B_ROPE // 2
    inv = ROPE_THETA ** (-np.arange(0, half, 2, dtype=np.float64) / half)
    ar = (t // GRID_W)[:, None] * inv[None, :]
    ac = (t % GRID_W)[:, None] * inv[None, :]
    ang = np.concatenate([ar, ar, ac, ac], axis=-1)
    cos = np.concatenate([np.ones((lc, B_ROPE)), np.cos(ang)], axis=0)
    sin = np.concatenate([np.zeros((lc, B_ROPE)), np.sin(ang)], axis=0)
    tt = lc + l
    scale = (B_NOPE + B_ROPE) ** -0.5 * LOG2E
    pad = np.zeros((tt, LANES - B_NOPE - B_ROPE))
    z64 = np.zeros((tt, B_NOPE))
    cq = np.concatenate([np.full((tt, B_NOPE), scale), scale * cos, pad], axis=1)
    sq = np.concatenate([z64, scale * sin, pad], axis=1)
    ck = np.concatenate([z64, cos, pad], axis=1)
    sk = np.concatenate([z64, sin, pad], axis=1)
    return tuple(jnp.asarray(a, F32) for a in (cq, sq, ck, sk))


def _mod_row(b, t, n_ctx_tiles, bsz):
    return jnp.where(t < n_ctx_tiles, bsz, b)


def _proj0(ctx, x, mod, ng, weights, tables):
    bsz, l, d = x.shape
    lc = ctx.shape[1]
    tt = lc + l
    win, qg, wq, wqr, kvg, wk, wv = weights
    vone = _value_ones()
    nct = lc // TM
    tok = lambda w: pl.BlockSpec((1, TM, w), lambda b, t: (b, t, 0))
    head = pl.BlockSpec((1, B_HEADS, TM, LANES), lambda b, t: (b, 0, t, 0))
    head_t = pl.BlockSpec((1, B_HEADS, MLA_VROWS, TM), lambda b, t: (b, 0, 0, t))
    tab = pl.BlockSpec((TM, LANES), lambda b, t: (t, 0))
    hshape = jax.ShapeDtypeStruct((bsz, B_HEADS, tt, LANES), BF16)
    return pl.pallas_call(
        functools.partial(_proj0_kernel, nct=nct),
        out_shape=(
            jax.ShapeDtypeStruct((bsz, tt, 3 * A_HEADS * A_DK), F32),
            jax.ShapeDtypeStruct((bsz, tt, A_HEADS * A_DV), F32),
            jax.ShapeDtypeStruct((bsz, tt, LANES), F32),
            hshape, hshape, jax.ShapeDtypeStruct((bsz, B_HEADS, MLA_VROWS, tt), BF16),
        ),
        grid=(bsz, tt // TM),
        in_specs=_joint_specs(d, nct) + [
            pl.BlockSpec((1, 6, d), lambda b, t: (_mod_row(b, t, nct, bsz), 0, 0)),
            _const_spec((1, d)), _const_spec(win.shape),
            _const_spec((1, B_Q_RANK)), _const_spec(wq.shape), _const_spec(wqr.shape),
            _const_spec((1, B_KV_RANK)), _const_spec(wk.shape), _const_spec(wv.shape),
            _const_spec(vone.shape),
            tab, tab, tab, tab,
        ],
        out_specs=(tok(3 * A_HEADS * A_DK), tok(A_HEADS * A_DV), tok(LANES), head, head, head_t),
        compiler_params=_params(("parallel", "parallel")),
        name="proj0",
    )(ctx, x, mod, ng.reshape(1, d), win, qg.reshape(1, -1), wq, wqr, kvg.reshape(1, -1), wk, wv, vone,
      *tables)


A_CONV = 5
CONV_HALO = 8


def _split_bf16(a):
    hi = a.astype(BF16)
    return hi, (a - hi.astype(F32)).astype(BF16)


def _split3_bf16(a):
    hi, rest = a.astype(BF16), a
    rest = rest - hi.astype(F32)
    mid = rest.astype(BF16)
    return hi, mid, (rest - mid.astype(F32)).astype(BF16)


def _dot3(a, b):
    ah, al = _split_bf16(a)
    bh, bl = _split_bf16(b)
    return _dot(ah, bh) + (_dot(ah, bl) + _dot(al, bh))


def _bdot(a, b):
    return lax.dot_general(a.astype(BF16), b.astype(BF16), (((2,), (1,)), ((0,), (0,))),
                           preferred_element_type=F32)


def _gdn_pre_kernel(x_ref, prev_ref, next_ref, ab_ref, cw_ref, shift_ref, alog_ref, dtb_ref,
                    u_ref, wq_ref, kgt_ref, qk_ref, gl_ref, *, ncc, nc):
    c = GDN_CHUNK
    i = pl.program_id(0)
    bsz = x_ref.shape[0]
    prev_ok = ((i != 0) & (i != ncc)).astype(F32)
    next_ok = ((i != ncc - 1) & (i != nc - 1)).astype(F32)
    cw = cw_ref[...]
    nq = A_HEADS * A_DK
    row = lax.broadcasted_iota(jnp.int32, (c, c), 0)
    col = lax.broadcasted_iota(jnp.int32, (c, c), 1)
    eye = (row == col).astype(F32)
    tri = [((row >= col), (row > col)), ((row <= col), (row < col))]
    ns, rhs, qgs = [], [], []

    def l2n(t):
        return t * lax.rsqrt(jnp.sum(t * t, axis=-1, keepdims=True) + 1e-6)

    for b in range(bsz):
        xe = jnp.concatenate([prev_ref[b] * prev_ok, x_ref[b], next_ref[b] * next_ok], axis=0)
        xw = jnp.concatenate([(xe * cw[j:j + 1]).astype(BF16) for j in range(A_CONV)], axis=0)
        y = _silu(_dot(shift_ref[...], xw))
        ab = ab_ref[b]
        g_all = -jnp.exp(alog_ref[...]) * (jnp.maximum(ab + dtb_ref[...], 0.0)
                                           + jnp.log1p(jnp.exp(-jnp.abs(ab + dtb_ref[...]))))
        beta_all = jax.nn.sigmoid(ab)
        g_parts = _split3_bf16(g_all)
        gt_parts = _split3_bf16(g_all.T)
        g_tot = jnp.sum(g_all, axis=0, keepdims=True)
        heads = []
        for h in range(A_HEADS):
            sl = slice(h * A_DK, (h + 1) * A_DK)
            kh = l2n(y[:, nq + h * A_DK:nq + (h + 1) * A_DK])
            heads.append((l2n(y[:, sl]) * (A_DK ** -0.5), kh, y[:, 2 * nq + h * A_DV:2 * nq + (h + 1) * A_DV],
                          kh.astype(BF16)))
        for d in range(2):
            incl, strict = tri[d]
            m_incl = _onehot(incl)
            gc_col = sum(_dot(m_incl, p) for p in g_parts)
            gc_row = sum(_dot_nt(p, m_incl) for p in gt_parts)
            gl_rows = []
            for h in range(A_HEADS):
                qh, kh, vh, khb = heads[h]
                ln = d * A_HEADS + h
                beta = beta_all[:, 2 * A_HEADS + ln:2 * A_HEADS + ln + 1]
                gcc = gc_col[:, ln:ln + 1]
                gcr = gc_row[ln:ln + 1, :]
                gtot = g_tot[:, ln:ln + 1]
                decay = jnp.where(incl, jnp.exp(jnp.where(incl, gcc - gcr, 0.0)), 0.0)
                n = -jnp.where(strict, beta * _dot_nt(khb, khb) * decay, 0.0)
                eg = jnp.exp(gcc)
                ns.append(n)
                rhs.append(jnp.concatenate([vh * beta, kh * (beta * eg)], axis=1))
                qgs.append(qh * eg)
                kgt_ref[b, d, 0, h] = (kh * jnp.exp(gtot - gcc)).T.astype(BF16)
                qk_ref[b, d, 0, h] = jnp.where(incl, _dot_nt(qh.astype(BF16), khb) * decay, 0.0).astype(BF16)
                gl_rows.append(jnp.broadcast_to(jnp.exp(gtot), (1, LANES)))
            gl_ref[b, d, 0] = jnp.concatenate(gl_rows + gl_rows, axis=0)
    n = jnp.stack(ns, axis=0)
    same = lambda s: lax.shift_right_logical(row, s) == lax.shift_right_logical(col, s)
    nd = jnp.where(same(GDN_BASE_LOG2), n, 0.0)
    x = eye + nd
    p = _bdot(nd, nd)
    z = _bdot(p, jnp.concatenate([p, x], axis=2))
    x = x + z[:, :, c:]
    x = x + _bdot(z[:, :, :c], x)
    for s in range(GDN_BASE_LOG2, GDN_CHUNK_LOG2):
        nl = jnp.where(same(s + 1) & jnp.logical_not(same(s)), n, 0.0)
        x = x + _bdot(_bdot(x, nl), x)
    y = _bdot(x, jnp.stack(rhs, axis=0))
    for b in range(bsz):
        for d in range(2):
            for h in range(A_HEADS):
                k = (b * 2 + d) * A_HEADS + h
                u_ref[b, d, 0, h] = y[k, :, :A_DV]
                wq_ref[b, d, 0, h] = jnp.concatenate([y[k, :, A_DV:], qgs[k]], axis=0).astype(BF16)


def _gdn_scan_kernel(*refs, bsz):
    ins, (of_ref, ob_ref, s_ref) = refs[:10], refs[10:]

    @pl.when(pl.program_id(0) == 0)
    def _():
        s_ref[...] = jnp.zeros_like(s_ref)

    c = GDN_CHUNK
    n = bsz * A_HEADS
    bdot = lambda a, b: lax.dot_general(a, b, (((2,), (1,)), ((0,), (0,))), preferred_element_type=F32)
    flat = lambda ref: ref[:, 0, 0].reshape(n, *ref.shape[-2:])
    both = lambda k: jnp.concatenate([flat(ins[k]), flat(ins[5 + k])], axis=0)
    u, wq, kgt, qk = both(0), both(1), both(2), both(3)
    gl = jnp.concatenate([ins[4][:, 0, 0, :A_HEADS, :].reshape(n, 1, LANES),
                          ins[9][:, 0, 0, :A_HEADS, :].reshape(n, 1, LANES)], axis=0)
    s = s_ref[...]
    ws = bdot(wq, s.astype(BF16))
    v_new = (u - ws[:, :c]).astype(BF16)
    o = ws[:, c:] + bdot(qk, v_new)
    s_ref[...] = s * gl + bdot(kgt, v_new)
    for d, o_ref in ((0, of_ref), (1, ob_ref)):
        for b in range(bsz):
            for h in range(A_HEADS):
                o_ref[b, :, h * A_DV:(h + 1) * A_DV] = o[(d * bsz + b) * A_HEADS + h]


def _gdn(qkv_pre, ab, conv_w, a_log, dt_bias, lc):
    bsz, tt, wqkv = qkv_pre.shape
    c = GDN_CHUNK
    nc, ncc = tt // c, lc // c
    hb = c // CONV_HALO
    per = lambda r, w: pl.BlockSpec((bsz, 2, 1, A_HEADS, r, w), lambda i: (0, 0, i, 0, 0, 0))
    shp = lambda r, w, dt: jax.ShapeDtypeStruct((bsz, 2, nc, A_HEADS, r, w), dt)
    lane_row = lambda t: jnp.concatenate([t.reshape(1, -1), jnp.zeros((1, LANES - t.size), F32)], axis=1)
    cw = jnp.concatenate([conv_w, jnp.zeros((CONV_HALO - A_CONV, wqkv), F32)], axis=0)
    rows = c + 2 * CONV_HALO
    shift = np.zeros((c, A_CONV * rows), np.float32)
    for j in range(A_CONV):
        shift[np.arange(c), j * rows + np.arange(c) + CONV_HALO - A_CONV // 2 + j] = 1.0
    shift = jnp.asarray(shift, BF16)
    pre = pl.pallas_call(
        functools.partial(_gdn_pre_kernel, ncc=ncc, nc=nc),
        out_shape=(shp(c, A_DV, F32), shp(2 * c, A_DK, BF16), shp(A_DK, c, BF16), shp(c, c, BF16),
                   jax.ShapeDtypeStruct((bsz, 2, nc, 2 * A_HEADS, LANES), F32)),
        grid=(nc,),
        in_specs=[pl.BlockSpec((bsz, c, wqkv), lambda i: (0, i, 0)),
                  pl.BlockSpec((bsz, CONV_HALO, wqkv), lambda i: (0, jnp.maximum(i * hb - 1, 0), 0)),
                  pl.BlockSpec((bsz, CONV_HALO, wqkv), lambda i: (0, jnp.minimum((i + 1) * hb, nc * hb - 1), 0)),
                  pl.BlockSpec((bsz, c, LANES), lambda i: (0, i, 0)),
                  _const_spec(cw.shape), _const_spec(shift.shape), _const_spec((1, LANES)),
                  _const_spec((1, LANES))],
        out_specs=(per(c, A_DV), per(2 * c, A_DK), per(A_DK, c), per(c, c),
                   pl.BlockSpec((bsz, 2, 1, 2 * A_HEADS, LANES), lambda i: (0, 0, i, 0, 0))),
        compiler_params=_params(("parallel",)),
        name="gdn_pre",
    )(qkv_pre, qkv_pre, qkv_pre, ab, cw, shift, lane_row(a_log), lane_row(dt_bias))

    def chunk(d, s):
        return s if d == 0 else jnp.where(s < ncc, ncc - 1 - s, nc - 1 - (s - ncc))

    in_specs = []
    for d in range(2):
        for arr in pre[:4]:
            r, w = arr.shape[-2:]
            in_specs.append(pl.BlockSpec((bsz, 1, 1, A_HEADS, r, w),
                                         lambda s, d=d: (0, d, chunk(d, s), 0, 0, 0)))
        in_specs.append(pl.BlockSpec((bsz, 1, 1, 2 * A_HEADS, LANES), lambda s, d=d: (0, d, chunk(d, s), 0, 0)))
    o_shape = jax.ShapeDtypeStruct((bsz, tt, A_HEADS * A_DV), F32)
    return pl.pallas_call(
        functools.partial(_gdn_scan_kernel, bsz=bsz),
        out_shape=(o_shape, o_shape),
        grid=(nc,),
        in_specs=in_specs,
        out_specs=tuple(pl.BlockSpec((bsz, c, A_HEADS * A_DV), lambda s, d=d: (0, chunk(d, s), 0))
                        for d in range(2)),
        scratch_shapes=[pltpu.VMEM((2 * bsz * A_HEADS, A_DK, A_DV), F32)],
        compiler_params=_params(("arbitrary",)),
        name="gdn_scan",
    )(*pre, *pre)


def _pick_chunk(n, options):
    for o in options:
        if n % o == 0:
            return o
    raise ValueError(f"no chunk size in {options} divides {n}")


def _mla_kernel(q_ref, k_ref, vt_ref, o_ref, sa_ref, sb_ref, pa_ref, pb_ref, *, lc, tq, tk, nk):
    t = pl.program_id(2)
    g = q_ref.shape[1]
    bnt = lambda a, b: lax.dot_general(a, b, (((2,), (2,)), ((0,), (0,))), preferred_element_type=F32)
    bnn = lambda a, b: lax.dot_general(a, b, (((2,), (1,)), ((0,), (0,))), preferred_element_type=F32)

    def absorb(s, m, acc, vt_chunk):
        m_new = jnp.maximum(m, jnp.max(s, axis=1, keepdims=True))
        p = jnp.exp2(s - m_new).astype(BF16)
        return m_new, jnp.exp2(m - m_new) * acc + bnn(vt_chunk, p)

    def update(q, m, acc, k_chunk, vt_chunk):
        return absorb(bnt(k_chunk, q), m, acc, vt_chunk)

    def finish(acc):
        parts = [acc[h, :B_DV] / acc[h, B_DV:B_DV + 1] for h in range(g)]
        o_ref[0] = jnp.concatenate(parts, axis=0).T

    init = (jnp.full((g, 1, tq), -jnp.inf, F32), jnp.zeros((g, MLA_VROWS, tq), F32))

    @pl.when(t < lc // tq)
    def _():
        finish(update(q_ref[0], *init, k_ref[0, :, :lc, :], vt_ref[0, :, :, :lc])[1])

    @pl.when(t >= lc // tq)
    def _():
        q = q_ref[0]

        s_bufs, p_bufs = (sa_ref, sb_ref), (pa_ref, pb_ref)
        chunk = lambda c: pl.ds(pl.multiple_of(c * tk, tk), tk)

        def stage_s(c, par):
            s_bufs[par][...] = bnt(k_ref[0, :, chunk(c), :], q)

        def stage_x(par, m):
            s = s_bufs[par][...]
            m_new = jnp.maximum(m, jnp.max(s, axis=1, keepdims=True))
            p_bufs[par][...] = jnp.exp2(s - m_new).astype(BF16)
            return m_new, jnp.exp2(m - m_new)

        def stage_v(c, par, alpha, acc):
            return alpha * acc + bnn(vt_ref[0, :, :, chunk(c)], p_bufs[par][...])

        def step(c, par, state, do_s=True, do_x=True):
            m, alpha, acc = state
            if do_s:
                stage_s(c + 2, par)
            if do_x:
                m_next, alpha_next = stage_x(1 - par, m)
            else:
                m_next, alpha_next = m, alpha
            return m_next, alpha_next, stage_v(c, par, alpha, acc)

        m0, acc0 = init
        stage_s(0, 0)
        if nk > 1:
            stage_s(1, 1)
        m1, alpha1 = stage_x(0, m0)
        state = (m1, alpha1, acc0)
        pairs = max(nk - 2, 0) // 2

        def body(i, state):
            return step(2 * i + 1, 1, step(2 * i, 0, state))

        state = lax.fori_loop(0, pairs, body, state)
        for c in range(2 * pairs, nk):
            state = step(c, c % 2, state, do_s=c + 2 < nk, do_x=c + 1 < nk)
        finish(state[2])


MLA_G = 8
MLA_VMEM_LIMIT = 60 * 1024 * 1024


def _mla_attention(q, k, vt, lc):
    bsz, nh, tt, _ = q.shape
    tq = TM
    tk = _pick_chunk(tt, (768, 512, 256))
    g = MLA_G
    kern = functools.partial(_mla_kernel, lc=lc, tq=tq, tk=tk, nk=tt // tk)
    return pl.pallas_call(
        kern,
        out_shape=jax.ShapeDtypeStruct((bsz, tt, nh * B_DV), F32),
        grid=(bsz, nh // g, tt // tq),
        in_specs=[pl.BlockSpec((1, g, tq, LANES), lambda b, hp, t: (b, hp, t, 0)),
                  pl.BlockSpec((1, g, tt, LANES), lambda b, hp, t: (b, hp, 0, 0), pipeline_mode=pl.Buffered(1)),
                  pl.BlockSpec((1, g, MLA_VROWS, tt), lambda b, hp, t: (b, hp, 0, 0),
                               pipeline_mode=pl.Buffered(1))],
        out_specs=pl.BlockSpec((1, tq, g * B_DV), lambda b, hp, t: (b, t, hp)),
        scratch_shapes=[pltpu.VMEM((g, tk, tq), F32), pltpu.VMEM((g, tk, tq), F32),
                        pltpu.VMEM((g, tk, tq), BF16), pltpu.VMEM((g, tk, tq), BF16)],
        compiler_params=pltpu.CompilerParams(dimension_semantics=("parallel", "parallel", "parallel"),
                                             vmem_limit_bytes=MLA_VMEM_LIMIT),
        name="mla_attention",
    )(q, k, vt)


def _mix0_kernel(of_ref, ob_ref, z_ref, yb_ref, c_ref, x_ref, mod_ref, gg_ref, woa_ref, wob_ref, ng_ref,
                 wg_ref, wu_ref, wd_ref, o_ref, *, nct):
    o = of_ref[0] + ob_ref[0]
    z = z_ref[0]
    gg = gg_ref[...]
    parts = []
    for h in range(A_HEADS):
        sl = slice(h * A_DV, (h + 1) * A_DV)
        parts.append((_rms(o[:, sl]) * gg * _silu(z[:, sl])).astype(BF16))
    ya = jnp.concatenate(parts, axis=1)
    y = _dot(ya, woa_ref[...]) + _dot(yb_ref[0].astype(BF16), wob_ref[...])
    m = mod_ref[0]
    x1 = _joint_tile(c_ref, x_ref, nct) + m[2:3] * y
    f = (_rms(x1) * ng_ref[...] * (1.0 + m[4:5]) + m[3:4]).astype(BF16)
    hid = _silu(_dot(f, wg_ref[...])) * _dot(f, wu_ref[...])
    o_ref[0] = x1 + m[5:6] * _dot(hid.astype(BF16), wd_ref[...])


def _mix0(o_f, o_b, z, yb, ctx, x, mod, gdn_g, w_out, ng, wg, wu, wd):
    bsz, l, d = x.shape
    lc = ctx.shape[1]
    tt = lc + l
    nct = lc // TM
    na = A_HEADS * A_DV
    woa, wob = w_out[:na].astype(BF16), w_out[na:].astype(BF16)
    wg, wu, wd = wg.astype(BF16), wu.astype(BF16), wd.astype(BF16)
    tok = lambda w: pl.BlockSpec((1, TM, w), lambda b, t: (b, t, 0))
    return pl.pallas_call(
        functools.partial(_mix0_kernel, nct=nct),
        out_shape=jax.ShapeDtypeStruct((bsz, tt, d), F32),
        grid=(bsz, tt // TM),
        in_specs=[tok(na), tok(na), tok(na), tok(B_HEADS * B_DV)] + _joint_specs(d, nct) + [
            pl.BlockSpec((1, 6, d), lambda b, t: (_mod_row(b, t, nct, bsz), 0, 0)),
            _const_spec((1, A_DV)), _const_spec(woa.shape), _const_spec(wob.shape), _const_spec((1, d)),
            _const_spec(wg.shape), _const_spec(wu.shape), _const_spec(wd.shape),
        ],
        out_specs=tok(d),
        compiler_params=_params(("parallel", "parallel")),
        name="mix0_ffn",
    )(o_f, o_b, z, yb, ctx, x, mod, gdn_g.reshape(1, -1), woa, wob, ng.reshape(1, d), wg, wu, wd)


def _proj1_kernel(x_ref, mod_ref, ng_ref, w_ref, q_ref, k_ref, v_ref):
    m = mod_ref[0]
    h = (_rms(x_ref[0]) * ng_ref[...] * (1.0 + m[1:2]) + m[0:1]).astype(BF16)
    p = _dot(h, w_ref[...])
    d = q_ref.shape[-1]
    q_ref[0] = (p[:, :d] * (C_DH ** -0.5 * LOG2E)).astype(BF16)
    k_ref[0] = p[:, d:2 * d].astype(BF16)
    v_ref[0] = p[:, 2 * d:].astype(BF16)


def _proj1(xa, mod, ng, w_qkv, lc):
    bsz, tt, d = xa.shape
    nct = lc // TM
    w = w_qkv.astype(BF16)
    tok = pl.BlockSpec((1, TM, d), lambda b, t: (b, t, 0))
    lat = pl.BlockSpec((1, TM, d), lambda b, t: (b, jnp.maximum(t - nct, 0), 0))
    shp = jax.ShapeDtypeStruct((bsz, tt, d), BF16)
    return pl.pallas_call(
        _proj1_kernel,
        out_shape=(jax.ShapeDtypeStruct((bsz, tt - lc, d), BF16), shp, shp),
        grid=(bsz, tt // TM),
        in_specs=[tok, pl.BlockSpec((1, 6, d), lambda b, t: (_mod_row(b, t, nct, bsz), 0, 0)),
                  _const_spec((1, d)), _const_spec(w.shape)],
        out_specs=(lat, tok, tok),
        compiler_params=_params(("parallel", "arbitrary")),
        name="proj1",
    )(xa, mod, ng.reshape(1, d), w)


NAT_HG = 4
NAT_KEYS = WIN_R * GRID_W


def _nat_bias(rpb):
    qc = np.arange(GRID_W)[:, None]
    kc = np.arange(GRID_W)[None, :]
    c0 = np.clip(qc - WIN_C // 2, 0, GRID_W - WIN_C)
    inside = (kc >= c0) & (kc < c0 + WIN_C)
    dc = kc - qc + (WIN_C - 1)
    onehot = ((np.arange(2 * WIN_C - 1)[:, None, None] == dc[None]) & inside[None]).astype(np.float32)
    tab = jnp.einsum('hrd,dqk->hqrk', rpb, jnp.asarray(onehot), precision=HIGHEST)
    tab = tab + jnp.asarray(np.where(inside, 0.0, NEG).astype(np.float32))[None, :, None, :]
    tab = jnp.stack([tab[:, :, WIN_R - 1 - v:2 * WIN_R - 1 - v, :] for v in range(WIN_R)], axis=0)
    return tab.reshape(WIN_R, rpb.shape[0], GRID_W, NAT_KEYS)


NAT_RB = 16


def _nat_kernel(q_ref, k_ref, v_ref, bias_ref, o_ref, s1_ref, s2_ref, p1_ref, p2_ref, l_ref, *, lc, rows):
    i = pl.program_id(2)
    lane = lax.broadcasted_iota(jnp.int32, (GRID_W, NAT_HG * C_DH), 1)
    masks = [(lane >= h * C_DH) & (lane < (h + 1) * C_DH) for h in range(NAT_HG)]

    def window(j):
        r = i * NAT_RB + j
        r0 = jnp.clip(r - WIN_R // 2, 0, rows - WIN_R)
        return r - r0, pl.ds(pl.multiple_of(lc + r0 * GRID_W, GRID_W), NAT_KEYS)

    def qrows(j):
        return pl.ds(pl.multiple_of(j * GRID_W, GRID_W), GRID_W)

    def stage_s(j, par):
        variant, keys = window(j)
        q = q_ref[0, qrows(j), :]
        qs = jnp.concatenate([jnp.where(hm, q, jnp.zeros_like(q)) for hm in masks], axis=0)
        s1_ref[par] = _dot_nt(qs, k_ref[0, keys, :]) + bias_ref[variant, 0].reshape(NAT_HG * GRID_W, NAT_KEYS)
        s2_ref[par] = _dot_nt(qs, k_ref[0, :lc, :])

    def stage_x(par):
        s1, s2 = s1_ref[par], s2_ref[par]
        m = jnp.maximum(jnp.max(s1, axis=-1, keepdims=True), jnp.max(s2, axis=-1, keepdims=True))
        e1 = jnp.exp2(s1 - m)
        e2 = jnp.exp2(s2 - m)
        l_ref[par] = jnp.sum(e1, axis=-1, keepdims=True) + jnp.sum(e2, axis=-1, keepdims=True)
        p1_ref[par] = e1.astype(BF16)
        p2_ref[par] = e2.astype(BF16)

    def stage_v(j, par):
        _, keys = window(j)
        o = (_dot(p1_ref[par], v_ref[0, keys, :]) + _dot(p2_ref[par], v_ref[0, :lc, :])) / l_ref[par]
        acc = jnp.zeros((GRID_W, NAT_HG * C_DH), F32)
        for h, hm in enumerate(masks):
            acc = jnp.where(hm, o[h * GRID_W:(h + 1) * GRID_W], acc)
        o_ref[0, qrows(j), :] = acc.astype(BF16)

    def step(j, par, do_s=True, do_x=True):
        if do_s:
            stage_s(j + 2, par)
        if do_x:
            stage_x(1 - par)
        stage_v(j, par)

    stage_s(0, 0)
    stage_s(1, 1)
    stage_x(0)

    def body(t, carry):
        step(2 * t, 0)
        step(2 * t + 1, 1)
        return carry

    pairs = (NAT_RB - 2) // 2
    lax.fori_loop(0, pairs, body, 0)
    for j in range(2 * pairs, NAT_RB):
        step(j, j % 2, do_s=j + 2 < NAT_RB, do_x=j + 1 < NAT_RB)


def _nat(q, k, v, rpb, lc):
    bsz, tt, d = k.shape
    l = tt - lc
    rows = l // GRID_W
    assert rows >= WIN_R and rows % NAT_RB == 0 and q.shape[1] == l
    hw = NAT_HG * C_DH
    ng = d // hw
    bias = (_nat_bias(rpb) * LOG2E).reshape(WIN_R, ng, NAT_HG, GRID_W, NAT_KEYS)
    kern = functools.partial(_nat_kernel, lc=lc, rows=rows)
    qb = NAT_RB * GRID_W
    kv = pl.BlockSpec((1, tt, hw), lambda b, g, r: (b, 0, g))
    return pl.pallas_call(
        kern,
        out_shape=jax.ShapeDtypeStruct((bsz, l, d), BF16),
        grid=(bsz, ng, rows // NAT_RB),
        in_specs=[
            pl.BlockSpec((1, qb, hw), lambda b, g, r: (b, r, g)),
            kv, kv,
            pl.BlockSpec((WIN_R, 1, NAT_HG, GRID_W, NAT_KEYS), lambda b, g, r: (0, g, 0, 0, 0)),
        ],
        out_specs=pl.BlockSpec((1, qb, hw), lambda b, g, r: (b, r, g)),
        scratch_shapes=[pltpu.VMEM((2, NAT_HG * GRID_W, NAT_KEYS), F32), pltpu.VMEM((2, NAT_HG * GRID_W, lc), F32),
                        pltpu.VMEM((2, NAT_HG * GRID_W, NAT_KEYS), BF16), pltpu.VMEM((2, NAT_HG * GRID_W, lc), BF16),
                        pltpu.VMEM((2, NAT_HG * GRID_W, 1), F32)],
        compiler_params=_params(("parallel", "parallel", "arbitrary")),
        name="nat",
    )(q, k, v, bias)


def _mix1_kernel(o_ref, x_ref, mod_ref, wo_ref, ng_ref, wr_ref, x_out, f_out, w_out, rank_out, cnt_out,
                 carry_ref, *, group):
    t = pl.program_id(1)
    m = mod_ref[0]
    x1 = x_ref[0] + m[2:3] * _dot(o_ref[0], wo_ref[...])
    x_out[0] = x1
    f = _rms(x1) * ng_ref[...] * (1.0 + m[4:5]) + m[3:4]
    f_out[0] = f.astype(BF16)
    logits = _dot3(f, wr_ref[...])
    lane = lax.broadcasted_iota(jnp.int32, logits.shape, 1)
    logits = jnp.where(lane < N_EXPERTS, logits, -jnp.inf)
    m1 = jnp.max(logits, axis=-1, keepdims=True)
    i1 = jnp.min(jnp.where(logits == m1, lane, LANES), axis=-1, keepdims=True)
    rest = jnp.where(lane == i1, -jnp.inf, logits)
    m2 = jnp.max(rest, axis=-1, keepdims=True)
    i2 = jnp.min(jnp.where(rest == m2, lane, LANES), axis=-1, keepdims=True)
    e2 = jnp.exp(m2 - m1)
    g1 = 1.0 / (1.0 + e2)
    w_out[0] = jnp.where(lane == i1, g1, 0.0) + jnp.where(lane == i2, e2 * g1, 0.0)

    @pl.when(t % group == 0)
    def _():
        carry_ref[...] = jnp.zeros_like(carry_ref)

    sel = (lane == i1) | (lane == i2)
    tm = logits.shape[0]
    tri = lax.broadcasted_iota(jnp.int32, (tm, tm), 0) >= lax.broadcasted_iota(jnp.int32, (tm, tm), 1)
    csum = _dot(_onehot(tri), _onehot(sel))
    carry = carry_ref[...]
    rank_out[0] = jnp.where(sel, carry + csum - 1.0, -1.0)
    carry = carry + csum[tm - 1:tm, :]
    carry_ref[...] = carry
    cnt_out[0, 0] = carry


def _mix1(o, xa, mod, w_out, ng, w_router, lc, tmx):
    bsz, tt, d = xa.shape
    l = tt - lc
    nct = lc // TM
    group = tmx // TM
    wr = jnp.concatenate([w_router, jnp.zeros((d, LANES - N_EXPERTS), F32)], axis=1)
    tok = lambda w: pl.BlockSpec((1, TM, w), lambda b, t: (b, t, 0))
    return pl.pallas_call(
        functools.partial(_mix1_kernel, group=group),
        out_shape=(jax.ShapeDtypeStruct((bsz, l, d), F32), jax.ShapeDtypeStruct((bsz, l, d), BF16),
                   jax.ShapeDtypeStruct((bsz, l, LANES), F32), jax.ShapeDtypeStruct((bsz, l, LANES), F32),
                   jax.ShapeDtypeStruct((bsz, l // tmx, 1, LANES), F32)),
        grid=(bsz, l // TM),
        in_specs=[tok(d), pl.BlockSpec((1, TM, d), lambda b, t: (b, t + nct, 0)),
                  pl.BlockSpec((1, 6, d), lambda b, t: (b, 0, 0)),
                  _const_spec((d, d)), _const_spec((1, d)), _const_spec(wr.shape)],
        out_specs=(tok(d), tok(d), tok(LANES), tok(LANES),
                   pl.BlockSpec((1, 1, 1, LANES), lambda b, t: (b, t // group, 0, 0))),
        scratch_shapes=[pltpu.VMEM((1, LANES), F32)],
        compiler_params=_params(("parallel", "arbitrary")),
        name="mix1_router",
    )(o, xa, mod, w_out.astype(BF16), ng.reshape(1, d), wr)


MOE_TOKENS = (2048, 1024)
MOE_GATHER = 256
MOE_ROWS = 128
MOE_BIG = 4
MOE_FF = 512
MOE_COMBINE_ROWS = 512
MOE_VMEM_LIMIT = 60 * 1024 * 1024


def _moe_kernel(cnt_ref, f_ref, rank_t_ref, rank_c_ref, gate_ref, wg_ref, wu_ref, wd_ref, o_ref, xs_ref, y_ref):
    i, e, c = pl.program_id(0), pl.program_id(1), pl.program_id(2)
    n = cnt_ref[i * pl.num_programs(1) + e]
    n_rows = (n + MOE_ROWS - 1) // MOE_ROWS
    n_gather = (n + MOE_GATHER - 1) // MOE_GATHER
    tmx = f_ref.shape[0]

    @pl.when((e == 0) & (c == 0))
    def _():
        o_ref[...] = jnp.zeros_like(o_ref)

    @pl.when(c == 0)
    def _():
        rank_t = rank_t_ref[0]

        def gather(s, carry):
            base = pl.multiple_of(s * MOE_GATHER, MOE_GATHER)
            slot = lax.broadcasted_iota(jnp.int32, (MOE_GATHER, tmx), 0) + base
            sel = _onehot(rank_t == slot)
            xs_ref[pl.ds(base, MOE_GATHER), :] = _dot(sel, f_ref[...]).astype(BF16)
            y_ref[pl.ds(base, MOE_GATHER), :] = jnp.zeros((MOE_GATHER, y_ref.shape[1]), F32)
            return carry

        lax.fori_loop(0, n_gather, gather, 0)

    def expert(base, rows):
        x = xs_ref[pl.ds(base, rows), :]
        hid = _silu(_dot(x, wg_ref[0])) * _dot(x, wu_ref[0])
        y_ref[pl.ds(base, rows), :] += _dot(hid.astype(BF16), wd_ref[0])

    big = MOE_ROWS * MOE_BIG

    def big_block(s, carry):
        expert(pl.multiple_of(s * big, big), big)
        return carry

    lax.fori_loop(0, n_rows // MOE_BIG, big_block, 0)
    done = (n_rows // MOE_BIG) * MOE_BIG
    part = MOE_BIG // 2
    while part >= 1:
        take = ((n_rows - done) // part) > 0

        @pl.when(take)
        def _(done=done, part=part):
            expert(pl.multiple_of(done * MOE_ROWS, MOE_ROWS), part * MOE_ROWS)

        done = done + jnp.where(take, part, 0)
        part //= 2

    @pl.when(c == pl.num_programs(2) - 1)
    def _():
        pick = lax.broadcasted_iota(jnp.int32, rank_c_ref.shape, 1) == e
        rank_col = jnp.sum(jnp.where(pick, rank_c_ref[...], 0.0), axis=-1, keepdims=True)
        gate_col = jnp.sum(jnp.where(pick, gate_ref[...], 0.0), axis=-1, keepdims=True)

        def combine(s, carry):
            base = pl.multiple_of(s * MOE_GATHER, MOE_GATHER)
            ys = y_ref[pl.ds(base, MOE_GATHER), :].astype(BF16)
            slot = (lax.broadcasted_iota(jnp.int32, (MOE_COMBINE_ROWS, MOE_GATHER), 1) + base).astype(F32)
            for q in range(tmx // MOE_COMBINE_ROWS):
                rows = slice(q * MOE_COMBINE_ROWS, (q + 1) * MOE_COMBINE_ROWS)
                sel = _onehot(rank_col[rows] == slot)
                o_ref[rows, :] += gate_col[rows] * _dot(sel, ys)
            return carry

        lax.fori_loop(0, n_gather, combine, 0)


def _moe(f, gates, rank, cnt, wg, wu, wd, tmx):
    bsz, l, d = f.shape
    ne, _, dff = wg.shape
    n = bsz * l
    nt = n // tmx
    per_expert = lambda a: jnp.transpose(a.reshape(n, LANES)[:, :ne])
    rank_e = per_expert(rank).astype(jnp.int32)
    counts = cnt.reshape(nt, LANES)[:, :ne].astype(jnp.int32).reshape(nt * ne)
    grid_spec = pltpu.PrefetchScalarGridSpec(
        num_scalar_prefetch=1,
        grid=(nt, ne, dff // MOE_FF),
        in_specs=[
            pl.BlockSpec((tmx, d), lambda i, e, c, cnt: (i, 0)),
            pl.BlockSpec((1, 1, tmx), lambda i, e, c, cnt: (e, 0, i)),
            pl.BlockSpec((tmx, LANES), lambda i, e, c, cnt: (i, 0)),
            pl.BlockSpec((tmx, LANES), lambda i, e, c, cnt: (i, 0)),
            pl.BlockSpec((1, d, MOE_FF), lambda i, e, c, cnt: (e, 0, c)),
            pl.BlockSpec((1, d, MOE_FF), lambda i, e, c, cnt: (e, 0, c)),
            pl.BlockSpec((1, MOE_FF, d), lambda i, e, c, cnt: (e, c, 0)),
        ],
        out_specs=pl.BlockSpec((tmx, d), lambda i, e, c, cnt: (i, 0)),
        scratch_shapes=[pltpu.VMEM((tmx, d), BF16), pltpu.VMEM((tmx, d), F32)],
    )
    return pl.pallas_call(
        _moe_kernel,
        out_shape=jax.ShapeDtypeStruct((n, d), F32),
        grid_spec=grid_spec,
        compiler_params=pltpu.CompilerParams(dimension_semantics=("parallel", "arbitrary", "arbitrary"),
                                             vmem_limit_bytes=MOE_VMEM_LIMIT),
        name="moe_ffn",
    )(counts, f.reshape(n, d), rank_e.reshape(ne, 1, n), rank.reshape(n, LANES), gates.reshape(n, LANES),
      wg.astype(BF16), wu.astype(BF16), wd.astype(BF16)).reshape(bsz, l, d)


def _final_kernel(x_ref, y_ref, mod_ref, fg_ref, o_ref):
    x2 = x_ref[0] + mod_ref[0][5:6] * y_ref[0]
    o_ref[0] = _rms(x2) * fg_ref[...]


def _final(x, y, mod, fg):
    bsz, l, d = x.shape
    tm = _pick_chunk(l, (1024, 512, 256))
    tok = pl.BlockSpec((1, tm, d), lambda b, t: (b, t, 0))
    return pl.pallas_call(
        _final_kernel,
        out_shape=jax.ShapeDtypeStruct((bsz, l, d), F32),
        grid=(bsz, l // tm),
        in_specs=[tok, tok, pl.BlockSpec((1, 6, d), lambda b, t: (b, 0, 0)), _const_spec((1, d))],
        out_specs=tok,
        compiler_params=_params(("parallel", "parallel")),
        name="final_norm",
    )(x, y, mod, fg.reshape(1, d))


def kernel(x, c, ctx, c_ctx, ada_w, ada_b, norm_g, ev_w_in, ev_conv_w, ev_a_log, ev_dt_bias, ev_gdn_norm_g, ev_q_norm_g, ev_w_uq, ev_kv_norm_g, ev_w_ukv, ev_w_out, ev_ffn_wg, ev_ffn_wu, ev_ffn_wd, od_w_qkv, od_rpb, od_w_out, od_router, od_exp_wg, od_exp_wu, od_exp_wd, final_norm_g):
    bsz, l, d = x.shape
    lc = ctx.shape[1]
    assert ada_w.shape[0] == 2 and bsz < MOD_ROWS and lc % TM == 0 and l % TM == 0
    cs = jnp.concatenate([c, c_ctx[None], jnp.zeros((MOD_ROWS - bsz - 1, d), F32)], axis=0)
    mods = _ada_mod(cs, ada_w, ada_b).reshape(2, MOD_ROWS, 6, d)

    win, wq, wqr, wk, wv = _proj0_weights(ev_w_in[0], ev_w_uq[0], ev_w_ukv[0])
    qkv_pre, z, ab, q, k, v = _proj0(ctx, x, mods[0], norm_g[0, 0],
                                     (win, ev_q_norm_g[0], wq, wqr, ev_kv_norm_g[0], wk, wv),
                                     _rope_tables(lc, l))
    o_f, o_b = _gdn(qkv_pre, ab, ev_conv_w[0], ev_a_log[0], ev_dt_bias[0], lc)
    yb = _mla_attention(q, k, v, lc)
    xa = _mix0(o_f, o_b, z, yb, ctx, x, mods[0], ev_gdn_norm_g[0], ev_w_out[0], norm_g[0, 1],
               ev_ffn_wg[0], ev_ffn_wu[0], ev_ffn_wd[0])

    q1, k1, v1 = _proj1(xa, mods[1], norm_g[1, 0], od_w_qkv[0], lc)
    o1 = _nat(q1, k1, v1, od_rpb[0], lc)
    tmx = _pick_chunk(l, MOE_TOKENS)
    x1, f1, gates, rank, cnt = _mix1(o1, xa, mods[1], od_w_out[0], norm_g[1, 1], od_router[0], lc, tmx)
    y = _moe(f1, gates, rank, cnt, od_exp_wg[0], od_exp_wu[0], od_exp_wd[0], tmx)
    return _final(x1, y, mods[1], final_norm_g)
```

```python
import functools

import numpy as np
import jax
import jax.numpy as jnp
from jax import lax
from jax.experimental import pallas as pl
from jax.experimental.pallas import tpu as pltpu

F32 = jnp.float32
BF16 = jnp.bfloat16
HIGHEST = lax.Precision.HIGHEST

GRID_W = 64
NORM_EPS = 1e-6
A_HEADS, A_DK, A_DV = 4, 128, 128
GDN_CHUNK_LOG2 = 7
GDN_CHUNK = 1 << GDN_CHUNK_LOG2
GDN_BASE_LOG2 = 3
B_HEADS, B_Q_RANK, B_KV_RANK, B_NOPE, B_ROPE, B_DV = 8, 384, 256, 64, 32, 64
MLA_VROWS = B_DV + 16
ROPE_THETA = 10000.0
C_HEADS, C_DH = 16, 64
WIN_R, WIN_C = 8, 16
N_EXPERTS = 8
LANES = 128
MOD_ROWS = 8
NEG = -1e30
LOG2E = 1.4426950408889634
VMEM_LIMIT = 56 * 1024 * 1024

TM = 256


def _params(sem):
    return pltpu.CompilerParams(dimension_semantics=sem, vmem_limit_bytes=VMEM_LIMIT)


def _const_spec(shape):
    nd = len(shape)
    return pl.BlockSpec(shape, lambda *_: (0,) * nd, pipeline_mode=pl.Buffered(1))


def _rms(x):
    return x * lax.rsqrt(jnp.mean(x * x, axis=-1, keepdims=True) + NORM_EPS)


def _silu(x):
    return x * jax.nn.sigmoid(x)


def _onehot(mask):
    return jnp.where(mask, 1.0, 0.0).astype(BF16)


def _dot(a, b, **kw):
    return jnp.dot(a, b, preferred_element_type=F32, **kw)


def _dot_nt(a, b, **kw):
    return lax.dot_general(a, b, (((1,), (1,)), ((), ())), preferred_element_type=F32, **kw)


def _ada_kernel(s_ref, w_ref, b_ref, o_ref):
    s = _silu(s_ref[...])
    o_ref[0] = _dot(s, w_ref[0], precision=HIGHEST) + b_ref[0]


def _ada_mod(cs, ada_w, ada_b):
    depth, d, d6 = ada_w.shape
    tn = d6 // 4
    return pl.pallas_call(
        _ada_kernel,
        out_shape=jax.ShapeDtypeStruct((depth, MOD_ROWS, d6), F32),
        grid=(depth, d6 // tn),
        in_specs=[
            pl.BlockSpec((MOD_ROWS, d), lambda i, j: (0, 0)),
            pl.BlockSpec((1, d, tn), lambda i, j: (i, 0, j)),
            pl.BlockSpec((1, 1, tn), lambda i, j: (i, 0, j)),
        ],
        out_specs=pl.BlockSpec((1, MOD_ROWS, tn), lambda i, j: (i, 0, j)),
        compiler_params=_params(("parallel", "parallel")),
        name="ada_mod",
    )(cs, ada_w, ada_b.reshape(depth, 1, d6))


N_QKVZ = 4 * A_HEADS * A_DK
IN_OFF_AB = N_QKVZ
IN_OFF_CQ = IN_OFF_AB + LANES
IN_OFF_CKV = IN_OFF_CQ + B_Q_RANK
IN_OFF_KR = IN_OFF_CKV + B_KV_RANK
IN_COLS = IN_OFF_KR + LANES
HW = B_HEADS * LANES


def _joint_tile(c_ref, x_ref, nct):
    return jnp.where(pl.program_id(1) < nct, c_ref[0], x_ref[0])


def _joint_specs(d, nct):
    return [pl.BlockSpec((1, TM, d), lambda b, t: (b, jnp.minimum(t, nct - 1), 0)),
            pl.BlockSpec((1, TM, d), lambda b, t: (b, jnp.maximum(t - nct, 0), 0))]


def _proj0_kernel(c_ref, x_ref, mod_ref, ng_ref, win_ref, qg_ref, wq_ref, wqr_ref, kvg_ref, wk_ref, wv_ref,
                  vone_ref, cq_ref, sq_ref, ck_ref, sk_ref,
                  qkv_ref, z_ref, ab_ref, q_ref, k_ref, vt_ref, *, nct):
    x = _joint_tile(c_ref, x_ref, nct)
    m = mod_ref[0]
    h = _rms(x) * ng_ref[...] * (1.0 + m[1:2]) + m[0:1]
    p = _dot(h.astype(BF16), win_ref[...])
    qkv_ref[0] = p[:, :3 * A_HEADS * A_DK]
    z_ref[0] = p[:, 3 * A_HEADS * A_DK:N_QKVZ]
    ab_ref[0] = p[:, IN_OFF_AB:IN_OFF_CQ]
    nq = (_rms(p[:, IN_OFF_CQ:IN_OFF_CKV]) * qg_ref[...]).astype(BF16)
    nkv = (_rms(p[:, IN_OFF_CKV:IN_OFF_KR]) * kvg_ref[...]).astype(BF16)
    krp = p[:, IN_OFF_KR:IN_COLS]
    qa = _dot(nq, wq_ref[...])
    qb = _dot(nq, wqr_ref[...])
    kn = _dot(nkv, wk_ref[...])
    vv = _dot(nkv, wv_ref[...]) + vone_ref[...]
    cq, sq, ck, sk = cq_ref[...], sq_ref[...], ck_ref[...], sk_ref[...]
    kr = pltpu.roll(krp, B_NOPE, axis=1) * ck + pltpu.roll(krp, B_NOPE - B_ROPE, axis=1) * sk
    for hh in range(B_HEADS):
        sl = slice(hh * LANES, (hh + 1) * LANES)
        q_ref[0, hh] = (qa[:, sl] * cq + qb[:, sl] * sq).astype(BF16)
        k_ref[0, hh] = (kn[:, sl] + kr).astype(BF16)
        vt_ref[0, hh] = vv[:, sl].T[:MLA_VROWS].astype(BF16)


def _rot_cols(w):
    q = B_ROPE // 4
    return jnp.concatenate([-w[:, q:2 * q], w[:, :q], -w[:, 3 * q:], w[:, 2 * q:3 * q]], axis=1)


def _proj0_weights(w_in, w_uq, w_ukv):
    d = w_in.shape[0]
    offs = np.cumsum([0, 512, 512, 512, 512, 8, 8, B_Q_RANK, B_KV_RANK, B_ROPE])
    zeros = lambda n: jnp.zeros((d, n), F32)
    kr = w_in[:, offs[8]:offs[9]]
    win = jnp.concatenate([
        w_in[:, :offs[4]],
        w_in[:, offs[4]:offs[6]], zeros(LANES - 16),
        w_in[:, offs[6]:offs[7]],
        w_in[:, offs[7]:offs[8]],
        kr, _rot_cols(kr), zeros(LANES - 2 * B_ROPE),
    ], axis=1).astype(BF16)
    dq = B_NOPE + B_ROPE
    wq3 = w_uq.reshape(B_Q_RANK, B_HEADS, dq)
    zq = jnp.zeros((B_Q_RANK, B_HEADS, LANES - dq), F32)
    wq = jnp.concatenate([wq3, zq], axis=2).reshape(B_Q_RANK, HW).astype(BF16)
    rot = jnp.stack([_rot_cols(wq3[:, hh, B_NOPE:]) for hh in range(B_HEADS)], axis=1)
    wqr = jnp.concatenate([jnp.zeros((B_Q_RANK, B_HEADS, B_NOPE), F32), rot, zq], axis=2)
    wqr = wqr.reshape(B_Q_RANK, HW).astype(BF16)
    wkv3 = w_ukv.reshape(B_KV_RANK, B_HEADS, B_NOPE + B_DV)
    zk = jnp.zeros((B_KV_RANK, B_HEADS, LANES - B_NOPE), F32)
    wk = jnp.concatenate([wkv3[:, :, :B_NOPE], zk], axis=2).reshape(B_KV_RANK, HW).astype(BF16)
    vpart = wkv3[:, :, B_NOPE:]
    wv = jnp.concatenate([vpart, jnp.zeros_like(vpart)], axis=2).reshape(B_KV_RANK, HW).astype(BF16)
    return win, wq, wqr, wk, wv


def _value_ones():
    vone = np.zeros((1, HW), np.float32)
    for hh in range(B_HEADS):
        vone[0, hh * LANES + B_DV:hh * LANES + MLA_VROWS] = 1.0
    return jnp.asarray(vone)


def _rope_tables(lc, l):
    t = np.arange(l)
    half = B_ROPE // 2
    inv = ROPE_THETA ** (-np.arange(0, half, 2, dtype=np.float64) / half)
    ar = (t // GRID_W)[:, None] * inv[None, :]
    ac = (t % GRID_W)[:, None] * inv[None, :]
    ang = np.concatenate([ar, ar, ac, ac], axis=-1)
    cos = np.concatenate([np.ones((lc, B_ROPE)), np.cos(ang)], axis=0)
    sin = np.concatenate([np.zeros((lc, B_ROPE)), np.sin(ang)], axis=0)
    tt = lc + l
    scale = (B_NOPE + B_ROPE) ** -0.5 * LOG2E
    pad = np.zeros((tt, LANES - B_NOPE - B_ROPE))
    z64 = np.zeros((tt, B_NOPE))
    cq = np.concatenate([np.full((tt, B_NOPE), scale), scale * cos, pad], axis=1)
    sq = np.concatenate([z64, scale * sin, pad], axis=1)
    ck = np.concatenate([z64, cos, pad], axis=1)
    sk = np.concatenate([z64, sin, pad], axis=1)
    return tuple(jnp.asarray(a, F32) for a in (cq, sq, ck, sk))


def _mod_row(b, t, n_ctx_tiles, bsz):
    return jnp.where(t < n_ctx_tiles, bsz, b)


def _proj0(ctx, x, mod, ng, weights, tables):
    bsz, l, d = x.shape
    lc = ctx.shape[1]
    tt = lc + l
    win, qg, wq, wqr, kvg, wk, wv = weights
    vone = _value_ones()
    nct = lc // TM
    tok = lambda w: pl.BlockSpec((1, TM, w), lambda b, t: (b, t, 0))
    head = pl.BlockSpec((1, B_HEADS, TM, LANES), lambda b, t: (b, 0, t, 0))
    head_t = pl.BlockSpec((1, B_HEADS, MLA_VROWS, TM), lambda b, t: (b, 0, 0, t))
    tab = pl.BlockSpec((TM, LANES), lambda b, t: (t, 0))
    hshape = jax.ShapeDtypeStruct((bsz, B_HEADS, tt, LANES), BF16)
    return pl.pallas_call(
        functools.partial(_proj0_kernel, nct=nct),
        out_shape=(
            jax.ShapeDtypeStruct((bsz, tt, 3 * A_HEADS * A_DK), F32),
            jax.ShapeDtypeStruct((bsz, tt, A_HEADS * A_DV), F32),
            jax.ShapeDtypeStruct((bsz, tt, LANES), F32),
            hshape, hshape, jax.ShapeDtypeStruct((bsz, B_HEADS, MLA_VROWS, tt), BF16),
        ),
        grid=(bsz, tt // TM),
        in_specs=_joint_specs(d, nct) + [
            pl.BlockSpec((1, 6, d), lambda b, t: (_mod_row(b, t, nct, bsz), 0, 0)),
            _const_spec((1, d)), _const_spec(win.shape),
            _const_spec((1, B_Q_RANK)), _const_spec(wq.shape), _const_spec(wqr.shape),
            _const_spec((1, B_KV_RANK)), _const_spec(wk.shape), _const_spec(wv.shape),
            _const_spec(vone.shape),
            tab, tab, tab, tab,
        ],
        out_specs=(tok(3 * A_HEADS * A_DK), tok(A_HEADS * A_DV), tok(LANES), head, head, head_t),
        compiler_params=_params(("parallel", "parallel")),
        name="proj0",
    )(ctx, x, mod, ng.reshape(1, d), win, qg.reshape(1, -1), wq, wqr, kvg.reshape(1, -1), wk, wv, vone,
      *tables)


A_CONV = 5
CONV_HALO = 8


def _split_bf16(a):
    hi = a.astype(BF16)
    return hi, (a - hi.astype(F32)).astype(BF16)


def _split3_bf16(a):
    hi, rest = a.astype(BF16), a
    rest = rest - hi.astype(F32)
    mid = rest.astype(BF16)
    return hi, mid, (rest - mid.astype(F32)).astype(BF16)


def _dot3(a, b):
    ah, al = _split_bf16(a)
    bh, bl = _split_bf16(b)
    return _dot(ah, bh) + (_dot(ah, bl) + _dot(al, bh))


def _bdot(a, b):
    return lax.dot_general(a.astype(BF16), b.astype(BF16), (((2,), (1,)), ((0,), (0,))),
                           preferred_element_type=F32)


def _gdn_pre_kernel(x_ref, prev_ref, next_ref, ab_ref, cw_ref, shift_ref, alog_ref, dtb_ref,
                    u_ref, wq_ref, kgt_ref, qk_ref, gl_ref, *, ncc, nc):
    c = GDN_CHUNK
    i = pl.program_id(0)
    bsz = x_ref.shape[0]
    prev_ok = ((i != 0) & (i != ncc)).astype(F32)
    next_ok = ((i != ncc - 1) & (i != nc - 1)).astype(F32)
    cw = cw_ref[...]
    nq = A_HEADS * A_DK
    row = lax.broadcasted_iota(jnp.int32, (c, c), 0)
    col = lax.broadcasted_iota(jnp.int32, (c, c), 1)
    eye = (row == col).astype(F32)
    tri = [((row >= col), (row > col)), ((row <= col), (row < col))]
    ns, rhs, qgs = [], [], []

    def l2n(t):
        return t * lax.rsqrt(jnp.sum(t * t, axis=-1, keepdims=True) + 1e-6)

    for b in range(bsz):
        xe = jnp.concatenate([prev_ref[b] * prev_ok, x_ref[b], next_ref[b] * next_ok], axis=0)
        xw = jnp.concatenate([(xe * cw[j:j + 1]).astype(BF16) for j in range(A_CONV)], axis=0)
        y = _silu(_dot(shift_ref[...], xw))
        ab = ab_ref[b]
        g_all = -jnp.exp(alog_ref[...]) * (jnp.maximum(ab + dtb_ref[...], 0.0)
                                           + jnp.log1p(jnp.exp(-jnp.abs(ab + dtb_ref[...]))))
        beta_all = jax.nn.sigmoid(ab)
        g_parts = _split3_bf16(g_all)
        gt_parts = _split3_bf16(g_all.T)
        g_tot = jnp.sum(g_all, axis=0, keepdims=True)
        heads = []
        for h in range(A_HEADS):
            sl = slice(h * A_DK, (h + 1) * A_DK)
            kh = l2n(y[:, nq + h * A_DK:nq + (h + 1) * A_DK])
            heads.append((l2n(y[:, sl]) * (A_DK ** -0.5), kh, y[:, 2 * nq + h * A_DV:2 * nq + (h + 1) * A_DV],
                          kh.astype(BF16)))
        for d in range(2):
            incl, strict = tri[d]
            m_incl = _onehot(incl)
            gc_col = sum(_dot(m_incl, p) for p in g_parts)
            gc_row = sum(_dot_nt(p, m_incl) for p in gt_parts)
            gl_rows = []
            for h in range(A_HEADS):
                qh, kh, vh, khb = heads[h]
                ln = d * A_HEADS + h
                beta = beta_all[:, 2 * A_HEADS + ln:2 * A_HEADS + ln + 1]
                gcc = gc_col[:, ln:ln + 1]
                gcr = gc_row[ln:ln + 1, :]
                gtot = g_tot[:, ln:ln + 1]
                decay = jnp.where(incl, jnp.exp(jnp.where(incl, gcc - gcr, 0.0)), 0.0)
                n = -jnp.where(strict, beta * _dot_nt(khb, khb) * decay, 0.0)
                eg = jnp.exp(gcc)
                ns.append(n)
                rhs.append(jnp.concatenate([vh * beta, kh * (beta * eg)], axis=1))
                qgs.append(qh * eg)
                kgt_ref[b, d, 0, h] = (kh * jnp.exp(gtot - gcc)).T.astype(BF16)
                qk_ref[b, d, 0, h] = jnp.where(incl, _dot_nt(qh.astype(BF16), khb) * decay, 0.0).astype(BF16)
                gl_rows.append(jnp.broadcast_to(jnp.exp(gtot), (1, LANES)))
            gl_ref[b, d, 0] = jnp.concatenate(gl_rows + gl_rows, axis=0)
    n = jnp.stack(ns, axis=0)
    same = lambda s: lax.shift_right_logical(row, s) == lax.shift_right_logical(col, s)
    nd = jnp.where(same(GDN_BASE_LOG2), n, 0.0)
    x = eye + nd
    p = _bdot(nd, nd)
    z = _bdot(p, jnp.concatenate([p, x], axis=2))
    x = x + z[:, :, c:]
    x = x + _bdot(z[:, :, :c], x)
    for s in range(GDN_BASE_LOG2, GDN_CHUNK_LOG2):
        nl = jnp.where(same(s + 1) & jnp.logical_not(same(s)), n, 0.0)
        x = x + _bdot(_bdot(x, nl), x)
    y = _bdot(x, jnp.stack(rhs, axis=0))
    for b in range(bsz):
        for d in range(2):
            for h in range(A_HEADS):
                k = (b * 2 + d) * A_HEADS + h
                u_ref[b, d, 0, h] = y[k, :, :A_DV]
                wq_ref[b, d, 0, h] = jnp.concatenate([y[k, :, A_DV:], qgs[k]], axis=0).astype(BF16)


def _gdn_scan_kernel(*refs, bsz):
    ins, (of_ref, ob_ref, s_ref) = refs[:10], refs[10:]

    @pl.when(pl.program_id(0) == 0)
    def _():
        s_ref[...] = jnp.zeros_like(s_ref)

    c = GDN_CHUNK
    n = bsz * A_HEADS
    bdot = lambda a, b: lax.dot_general(a, b, (((2,), (1,)), ((0,), (0,))), preferred_element_type=F32)
    flat = lambda ref: ref[:, 0, 0].reshape(n, *ref.shape[-2:])
    both = lambda k: jnp.concatenate([flat(ins[k]), flat(ins[5 + k])], axis=0)
    u, wq, kgt, qk = both(0), both(1), both(2), both(3)
    gl = jnp.concatenate([ins[4][:, 0, 0, :A_HEADS, :].reshape(n, 1, LANES),
                          ins[9][:, 0, 0, :A_HEADS, :].reshape(n, 1, LANES)], axis=0)
    s = s_ref[...]
    ws = bdot(wq, s.astype(BF16))
    v_new = (u - ws[:, :c]).astype(BF16)
    o = ws[:, c:] + bdot(qk, v_new)
    s_ref[...] = s * gl + bdot(kgt, v_new)
    for d, o_ref in ((0, of_ref), (1, ob_ref)):
        for b in range(bsz):
            for h in range(A_HEADS):
                o_ref[b, :, h * A_DV:(h + 1) * A_DV] = o[(d * bsz + b) * A_HEADS + h]


def _gdn(qkv_pre, ab, conv_w, a_log, dt_bias, lc):
    bsz, tt, wqkv = qkv_pre.shape
    c = GDN_CHUNK
    nc, ncc = tt // c, lc // c
    hb = c // CONV_HALO
    per = lambda r, w: pl.BlockSpec((bsz, 2, 1, A_HEADS, r, w), lambda i: (0, 0, i, 0, 0, 0))
    shp = lambda r, w, dt: jax.ShapeDtypeStruct((bsz, 2, nc, A_HEADS, r, w), dt)
    lane_row = lambda t: jnp.concatenate([t.reshape(1, -1), jnp.zeros((1, LANES - t.size), F32)], axis=1)
    cw = jnp.concatenate([conv_w, jnp.zeros((CONV_HALO - A_CONV, wqkv), F32)], axis=0)
    rows = c + 2 * CONV_HALO
    shift = np.zeros((c, A_CONV * rows), np.float32)
    for j in range(A_CONV):
        shift[np.arange(c), j * rows + np.arange(c) + CONV_HALO - A_CONV // 2 + j] = 1.0
    shift = jnp.asarray(shift, BF16)
    pre = pl.pallas_call(
        functools.partial(_gdn_pre_kernel, ncc=ncc, nc=nc),
        out_shape=(shp(c, A_DV, F32), shp(2 * c, A_DK, BF16), shp(A_DK, c, BF16), shp(c, c, BF16),
                   jax.ShapeDtypeStruct((bsz, 2, nc, 2 * A_HEADS, LANES), F32)),
        grid=(nc,),
        in_specs=[pl.BlockSpec((bsz, c, wqkv), lambda i: (0, i, 0)),
                  pl.BlockSpec((bsz, CONV_HALO, wqkv), lambda i: (0, jnp.maximum(i * hb - 1, 0), 0)),
                  pl.BlockSpec((bsz, CONV_HALO, wqkv), lambda i: (0, jnp.minimum((i + 1) * hb, nc * hb - 1), 0)),
                  pl.BlockSpec((bsz, c, LANES), lambda i: (0, i, 0)),
                  _const_spec(cw.shape), _const_spec(shift.shape), _const_spec((1, LANES)),
                  _const_spec((1, LANES))],
        out_specs=(per(c, A_DV), per(2 * c, A_DK), per(A_DK, c), per(c, c),
                   pl.BlockSpec((bsz, 2, 1, 2 * A_HEADS, LANES), lambda i: (0, 0, i, 0, 0))),
        compiler_params=_params(("parallel",)),
        name="gdn_pre",
    )(qkv_pre, qkv_pre, qkv_pre, ab, cw, shift, lane_row(a_log), lane_row(dt_bias))

    def chunk(d, s):
        return s if d == 0 else jnp.where(s < ncc, ncc - 1 - s, nc - 1 - (s - ncc))

    in_specs = []
    for d in range(2):
        for arr in pre[:4]:
            r, w = arr.shape[-2:]
            in_specs.append(pl.BlockSpec((bsz, 1, 1, A_HEADS, r, w),
                                         lambda s, d=d: (0, d, chunk(d, s), 0, 0, 0)))
        in_specs.append(pl.BlockSpec((bsz, 1, 1, 2 * A_HEADS, LANES), lambda s, d=d: (0, d, chunk(d, s), 0, 0)))
    o_shape = jax.ShapeDtypeStruct((bsz, tt, A_HEADS * A_DV), F32)
    return pl.pallas_call(
        functools.partial(_gdn_scan_kernel, bsz=bsz),
        out_shape=(o_shape, o_shape),
        grid=(nc,),
        in_specs=in_specs,
        out_specs=tuple(pl.BlockSpec((bsz, c, A_HEADS * A_DV), lambda s, d=d: (0, chunk(d, s), 0))
                        for d in range(2)),
        scratch_shapes=[pltpu.VMEM((2 * bsz * A_HEADS, A_DK, A_DV), F32)],
        compiler_params=_params(("arbitrary",)),
        name="gdn_scan",
    )(*pre, *pre)


def _pick_chunk(n, options):
    for o in options:
        if n % o == 0:
            return o
    raise ValueError(f"no chunk size in {options} divides {n}")


def _mla_kernel(q_ref, k_ref, vt_ref, o_ref, sa_ref, sb_ref, pa_ref, pb_ref, *, lc, tq, tk, nk):
    t = pl.program_id(2)
    g = q_ref.shape[1]
    bnt = lambda a, b: lax.dot_general(a, b, (((2,), (2,)), ((0,), (0,))), preferred_element_type=F32)
    bnn = lambda a, b: lax.dot_general(a, b, (((2,), (1,)), ((0,), (0,))), preferred_element_type=F32)

    def absorb(s, m, acc, vt_chunk):
        m_new = jnp.maximum(m, jnp.max(s, axis=1, keepdims=True))
        p = jnp.exp2(s - m_new).astype(BF16)
        return m_new, jnp.exp2(m - m_new) * acc + bnn(vt_chunk, p)

    def update(q, m, acc, k_chunk, vt_chunk):
        return absorb(bnt(k_chunk, q), m, acc, vt_chunk)

    def finish(acc):
        parts = [acc[h, :B_DV] / acc[h, B_DV:B_DV + 1] for h in range(g)]
        o_ref[0] = jnp.concatenate(parts, axis=0).T

    init = (jnp.full((g, 1, tq), -jnp.inf, F32), jnp.zeros((g, MLA_VROWS, tq), F32))

    @pl.when(t < lc // tq)
    def _():
        finish(update(q_ref[0], *init, k_ref[0, :, :lc, :], vt_ref[0, :, :, :lc])[1])

    @pl.when(t >= lc // tq)
    def _():
        q = q_ref[0]

        s_bufs, p_bufs = (sa_ref, sb_ref), (pa_ref, pb_ref)
        chunk = lambda c: pl.ds(pl.multiple_of(c * tk, tk), tk)

        def stage_s(c, par):
            s_bufs[par][...] = bnt(k_ref[0, :, chunk(c), :], q)

        def stage_x(par, m):
            s = s_bufs[par][...]
            m_new = jnp.maximum(m, jnp.max(s, axis=1, keepdims=True))
            p_bufs[par][...] = jnp.exp2(s - m_new).astype(BF16)
            return m_new, jnp.exp2(m - m_new)

        def stage_v(c, par, alpha, acc):
            return alpha * acc + bnn(vt_ref[0, :, :, chunk(c)], p_bufs[par][...])

        def step(c, par, state, do_s=True, do_x=True):
            m, alpha, acc = state
            if do_s:
                stage_s(c + 2, par)
            if do_x:
                m_next, alpha_next = stage_x(1 - par, m)
            else:
                m_next, alpha_next = m, alpha
            return m_next, alpha_next, stage_v(c, par, alpha, acc)

        m0, acc0 = init
        stage_s(0, 0)
        if nk > 1:
            stage_s(1, 1)
        m1, alpha1 = stage_x(0, m0)
        state = (m1, alpha1, acc0)
        pairs = max(nk - 2, 0) // 2

        def body(i, state):
            return step(2 * i + 1, 1, step(2 * i, 0, state))

        state = lax.fori_loop(0, pairs, body, state)
        for c in range(2 * pairs, nk):
            state = step(c, c % 2, state, do_s=c + 2 < nk, do_x=c + 1 < nk)
        finish(state[2])


MLA_G = 8
MLA_VMEM_LIMIT = 60 * 1024 * 1024


def _mla_attention(q, k, vt, lc):
    bsz, nh, tt, _ = q.shape
    tq = TM
    tk = _pick_chunk(tt, (768, 512, 256))
    g = MLA_G
    kern = functools.partial(_mla_kernel, lc=lc, tq=tq, tk=tk, nk=tt // tk)
    return pl.pallas_call(
        kern,
        out_shape=jax.ShapeDtypeStruct((bsz, tt, nh * B_DV), F32),
        grid=(bsz, nh // g, tt // tq),
        in_specs=[pl.BlockSpec((1, g, tq, LANES), lambda b, hp, t: (b, hp, t, 0)),
                  pl.BlockSpec((1, g, tt, LANES), lambda b, hp, t: (b, hp, 0, 0), pipeline_mode=pl.Buffered(1)),
                  pl.BlockSpec((1, g, MLA_VROWS, tt), lambda b, hp, t: (b, hp, 0, 0),
                               pipeline_mode=pl.Buffered(1))],
        out_specs=pl.BlockSpec((1, tq, g * B_DV), lambda b, hp, t: (b, t, hp)),
        scratch_shapes=[pltpu.VMEM((g, tk, tq), F32), pltpu.VMEM((g, tk, tq), F32),
                        pltpu.VMEM((g, tk, tq), BF16), pltpu.VMEM((g, tk, tq), BF16)],
        compiler_params=pltpu.CompilerParams(dimension_semantics=("parallel", "parallel", "parallel"),
                                             vmem_limit_bytes=MLA_VMEM_LIMIT),
        name="mla_attention",
    )(q, k, vt)


def _mix0_kernel(of_ref, ob_ref, z_ref, yb_ref, c_ref, x_ref, mod_ref, gg_ref, woa_ref, wob_ref, ng_ref,
                 wg_ref, wu_ref, wd_ref, o_ref, *, nct):
    o = of_ref[0] + ob_ref[0]
    z = z_ref[0]
    gg = gg_ref[...]
    parts = []
    for h in range(A_HEADS):
        sl = slice(h * A_DV, (h + 1) * A_DV)
        parts.append((_rms(o[:, sl]) * gg * _silu(z[:, sl])).astype(BF16))
    ya = jnp.concatenate(parts, axis=1)
    y = _dot(ya, woa_ref[...]) + _dot(yb_ref[0].astype(BF16), wob_ref[...])
    m = mod_ref[0]
    x1 = _joint_tile(c_ref, x_ref, nct) + m[2:3] * y
    f = (_rms(x1) * ng_ref[...] * (1.0 + m[4:5]) + m[3:4]).astype(BF16)
    hid = _silu(_dot(f, wg_ref[...])) * _dot(f, wu_ref[...])
    o_ref[0] = x1 + m[5:6] * _dot(hid.astype(BF16), wd_ref[...])


def _mix0(o_f, o_b, z, yb, ctx, x, mod, gdn_g, w_out, ng, wg, wu, wd):
    bsz, l, d = x.shape
    lc = ctx.shape[1]
    tt = lc + l
    nct = lc // TM
    na = A_HEADS * A_DV
    woa, wob = w_out[:na].astype(BF16), w_out[na:].astype(BF16)
    wg, wu, wd = wg.astype(BF16), wu.astype(BF16), wd.astype(BF16)
    tok = lambda w: pl.BlockSpec((1, TM, w), lambda b, t: (b, t, 0))
    return pl.pallas_call(
        functools.partial(_mix0_kernel, nct=nct),
        out_shape=jax.ShapeDtypeStruct((bsz, tt, d), F32),
        grid=(bsz, tt // TM),
        in_specs=[tok(na), tok(na), tok(na), tok(B_HEADS * B_DV)] + _joint_specs(d, nct) + [
            pl.BlockSpec((1, 6, d), lambda b, t: (_mod_row(b, t, nct, bsz), 0, 0)),
            _const_spec((1, A_DV)), _const_spec(woa.shape), _const_spec(wob.shape), _const_spec((1, d)),
            _const_spec(wg.shape), _const_spec(wu.shape), _const_spec(wd.shape),
        ],
        out_specs=tok(d),
        compiler_params=_params(("parallel", "parallel")),
        name="mix0_ffn",
    )(o_f, o_b, z, yb, ctx, x, mod, gdn_g.reshape(1, -1), woa, wob, ng.reshape(1, d), wg, wu, wd)


def _proj1_kernel(x_ref, mod_ref, ng_ref, w_ref, q_ref, k_ref, v_ref):
    m = mod_ref[0]
    h = (_rms(x_ref[0]) * ng_ref[...] * (1.0 + m[1:2]) + m[0:1]).astype(BF16)
    p = _dot(h, w_ref[...])
    d = q_ref.shape[-1]
    q_ref[0] = (p[:, :d] * (C_DH ** -0.5 * LOG2E)).astype(BF16)
    k_ref[0] = p[:, d:2 * d].astype(BF16)
    v_ref[0] = p[:, 2 * d:].astype(BF16)


def _proj1(xa, mod, ng, w_qkv, lc):
    bsz, tt, d = xa.shape
    nct = lc // TM
    w = w_qkv.astype(BF16)
    tok = pl.BlockSpec((1, TM, d), lambda b, t: (b, t, 0))
    lat = pl.BlockSpec((1, TM, d), lambda b, t: (b, jnp.maximum(t - nct, 0), 0))
    shp = jax.ShapeDtypeStruct((bsz, tt, d), BF16)
    return pl.pallas_call(
        _proj1_kernel,
        out_shape=(jax.ShapeDtypeStruct((bsz, tt - lc, d), BF16), shp, shp),
        grid=(bsz, tt // TM),
        in_specs=[tok, pl.BlockSpec((1, 6, d), lambda b, t: (_mod_row(b, t, nct, bsz), 0, 0)),
                  _const_spec((1, d)), _const_spec(w.shape)],
        out_specs=(lat, tok, tok),
        compiler_params=_params(("parallel", "arbitrary")),
        name="proj1",
    )(xa, mod, ng.reshape(1, d), w)


NAT_HG = 4
NAT_KEYS = WIN_R * GRID_W


def _nat_bias(rpb):
    qc = np.arange(GRID_W)[:, None]
    kc = np.arange(GRID_W)[None, :]
    c0 = np.clip(qc - WIN_C // 2, 0, GRID_W - WIN_C)
    inside = (kc >= c0) & (kc < c0 + WIN_C)
    dc = kc - qc + (WIN_C - 1)
    onehot = ((np.arange(2 * WIN_C - 1)[:, None, None] == dc[None]) & inside[None]).astype(np.float32)
    tab = jnp.einsum('hrd,dqk->hqrk', rpb, jnp.asarray(onehot), precision=HIGHEST)
    tab = tab + jnp.asarray(np.where(inside, 0.0, NEG).astype(np.float32))[None, :, None, :]
    tab = jnp.stack([tab[:, :, WIN_R - 1 - v:2 * WIN_R - 1 - v, :] for v in range(WIN_R)], axis=0)
    return tab.reshape(WIN_R, rpb.shape[0], GRID_W, NAT_KEYS)


NAT_RB = (32, 16, 8, 4, 2)


def _nat_kernel(q_ref, k_ref, v_ref, bias_ref, o_ref, s1_ref, s2_ref, p1_ref, p2_ref, l_ref, *, lc, rows,
                nrb):
    i = pl.program_id(2)
    lane = lax.broadcasted_iota(jnp.int32, (GRID_W, NAT_HG * C_DH), 1)
    masks = [(lane >= h * C_DH) & (lane < (h + 1) * C_DH) for h in range(NAT_HG)]

    def window(j):
        r = i * nrb + j
        r0 = jnp.clip(r - WIN_R // 2, 0, rows - WIN_R)
        return r - r0, pl.ds(pl.multiple_of(lc + r0 * GRID_W, GRID_W), NAT_KEYS)

    def qrows(j):
        return pl.ds(pl.multiple_of(j * GRID_W, GRID_W), GRID_W)

    def stage_s(j, par):
        variant, keys = window(j)
        q = q_ref[0, qrows(j), :]
        qs = jnp.concatenate([jnp.where(hm, q, jnp.zeros_like(q)) for hm in masks], axis=0)
        s1_ref[par] = _dot_nt(qs, k_ref[0, keys, :]) + bias_ref[variant, 0].reshape(NAT_HG * GRID_W, NAT_KEYS)
        s2_ref[par] = _dot_nt(qs, k_ref[0, :lc, :])

    def stage_x(par):
        s1, s2 = s1_ref[par], s2_ref[par]
        m = jnp.maximum(jnp.max(s1, axis=-1, keepdims=True), jnp.max(s2, axis=-1, keepdims=True))
        e1 = jnp.exp2(s1 - m)
        e2 = jnp.exp2(s2 - m)
        l_ref[par] = jnp.sum(e1, axis=-1, keepdims=True) + jnp.sum(e2, axis=-1, keepdims=True)
        p1_ref[par] = e1.astype(BF16)
        p2_ref[par] = e2.astype(BF16)

    def stage_v(j, par):
        _, keys = window(j)
        o = (_dot(p1_ref[par], v_ref[0, keys, :]) + _dot(p2_ref[par], v_ref[0, :lc, :])) / l_ref[par]
        acc = jnp.zeros((GRID_W, NAT_HG * C_DH), F32)
        for h, hm in enumerate(masks):
            acc = jnp.where(hm, o[h * GRID_W:(h + 1) * GRID_W], acc)
        o_ref[0, qrows(j), :] = acc.astype(BF16)

    def step(j, par, do_s=True, do_x=True):
        if do_s:
            stage_s(j + 2, par)
        if do_x:
            stage_x(1 - par)
        stage_v(j, par)

    stage_s(0, 0)
    stage_s(1, 1)
    stage_x(0)

    def body(t, carry):
        step(2 * t, 0)
        step(2 * t + 1, 1)
        return carry

    pairs = (nrb - 2) // 2
    lax.fori_loop(0, pairs, body, 0)
    for j in range(2 * pairs, nrb):
        step(j, j % 2, do_s=j + 2 < nrb, do_x=j + 1 < nrb)


def _nat(q, k, v, rpb, lc):
    bsz, tt, d = k.shape
    l = tt - lc
    rows = l // GRID_W
    assert rows >= WIN_R and q.shape[1] == l
    nrb = _pick_chunk(rows, NAT_RB)
    hw = NAT_HG * C_DH
    ng = d // hw
    bias = (_nat_bias(rpb) * LOG2E).reshape(WIN_R, ng, NAT_HG, GRID_W, NAT_KEYS)
    kern = functools.partial(_nat_kernel, lc=lc, rows=rows, nrb=nrb)
    qb = nrb * GRID_W
    kv = pl.BlockSpec((1, tt, hw), lambda b, g, r: (b, 0, g))
    return pl.pallas_call(
        kern,
        out_shape=jax.ShapeDtypeStruct((bsz, l, d), BF16),
        grid=(bsz, ng, rows // nrb),
        in_specs=[
            pl.BlockSpec((1, qb, hw), lambda b, g, r: (b, r, g)),
            kv, kv,
            pl.BlockSpec((WIN_R, 1, NAT_HG, GRID_W, NAT_KEYS), lambda b, g, r: (0, g, 0, 0, 0)),
        ],
        out_specs=pl.BlockSpec((1, qb, hw), lambda b, g, r: (b, r, g)),
        scratch_shapes=[pltpu.VMEM((2, NAT_HG * GRID_W, NAT_KEYS), F32), pltpu.VMEM((2, NAT_HG * GRID_W, lc), F32),
                        pltpu.VMEM((2, NAT_HG * GRID_W, NAT_KEYS), BF16), pltpu.VMEM((2, NAT_HG * GRID_W, lc), BF16),
                        pltpu.VMEM((2, NAT_HG * GRID_W, 1), F32)],
        compiler_params=_params(("parallel", "parallel", "arbitrary")),
        name="nat",
    )(q, k, v, bias)


def _mix1_kernel(o_ref, x_ref, mod_ref, wo_ref, ng_ref, wr_ref, x_out, f_out, w_out, rank_out, cnt_out,
                 carry_ref, *, group):
    t = pl.program_id(1)
    m = mod_ref[0]
    x1 = x_ref[0] + m[2:3] * _dot(o_ref[0], wo_ref[...])
    x_out[0] = x1
    f = _rms(x1) * ng_ref[...] * (1.0 + m[4:5]) + m[3:4]
    f_out[0] = f.astype(BF16)
    logits = _dot3(f, wr_ref[...])
    lane = lax.broadcasted_iota(jnp.int32, logits.shape, 1)
    logits = jnp.where(lane < N_EXPERTS, logits, -jnp.inf)
    m1 = jnp.max(logits, axis=-1, keepdims=True)
    i1 = jnp.min(jnp.where(logits == m1, lane, LANES), axis=-1, keepdims=True)
    rest = jnp.where(lane == i1, -jnp.inf, logits)
    m2 = jnp.max(rest, axis=-1, keepdims=True)
    i2 = jnp.min(jnp.where(rest == m2, lane, LANES), axis=-1, keepdims=True)
    e2 = jnp.exp(m2 - m1)
    g1 = 1.0 / (1.0 + e2)
    w_out[0] = jnp.where(lane == i1, g1, 0.0) + jnp.where(lane == i2, e2 * g1, 0.0)

    @pl.when(t % group == 0)
    def _():
        carry_ref[...] = jnp.zeros_like(carry_ref)

    sel = (lane == i1) | (lane == i2)
    tm = logits.shape[0]
    tri = lax.broadcasted_iota(jnp.int32, (tm, tm), 0) >= lax.broadcasted_iota(jnp.int32, (tm, tm), 1)
    csum = _dot(_onehot(tri), _onehot(sel))
    carry = carry_ref[...]
    rank_out[0] = jnp.where(sel, carry + csum - 1.0, -1.0)
    carry = carry + csum[tm - 1:tm, :]
    carry_ref[...] = carry
    cnt_out[0, 0] = carry


def _mix1(o, xa, mod, w_out, ng, w_router, lc, tmx):
    bsz, tt, d = xa.shape
    l = tt - lc
    nct = lc // TM
    group = tmx // TM
    wr = jnp.concatenate([w_router, jnp.zeros((d, LANES - N_EXPERTS), F32)], axis=1)
    tok = lambda w: pl.BlockSpec((1, TM, w), lambda b, t: (b, t, 0))
    return pl.pallas_call(
        functools.partial(_mix1_kernel, group=group),
        out_shape=(jax.ShapeDtypeStruct((bsz, l, d), F32), jax.ShapeDtypeStruct((bsz, l, d), BF16),
                   jax.ShapeDtypeStruct((bsz, l, LANES), F32), jax.ShapeDtypeStruct((bsz, l, LANES), F32),
                   jax.ShapeDtypeStruct((bsz, l // tmx, 1, LANES), F32)),
        grid=(bsz, l // TM),
        in_specs=[tok(d), pl.BlockSpec((1, TM, d), lambda b, t: (b, t + nct, 0)),
                  pl.BlockSpec((1, 6, d), lambda b, t: (b, 0, 0)),
                  _const_spec((d, d)), _const_spec((1, d)), _const_spec(wr.shape)],
        out_specs=(tok(d), tok(d), tok(LANES), tok(LANES),
                   pl.BlockSpec((1, 1, 1, LANES), lambda b, t: (b, t // group, 0, 0))),
        scratch_shapes=[pltpu.VMEM((1, LANES), F32)],
        compiler_params=_params(("parallel", "arbitrary")),
        name="mix1_router",
    )(o, xa, mod, w_out.astype(BF16), ng.reshape(1, d), wr)


MOE_TOKENS = (2048, 1024)
MOE_GATHER = 256
MOE_ROWS = 128
MOE_BIG = 4
MOE_FF = 512
MOE_COMBINE_ROWS = 512
MOE_VMEM_LIMIT = 60 * 1024 * 1024


def _moe_kernel(cnt_ref, f_ref, rank_t_ref, rank_c_ref, gate_ref, wg_ref, wu_ref, wd_ref, o_ref, xs_ref, y_ref):
    i, e, c = pl.program_id(0), pl.program_id(1), pl.program_id(2)
    n = cnt_ref[i * pl.num_programs(1) + e]
    n_rows = (n + MOE_ROWS - 1) // MOE_ROWS
    n_gather = (n + MOE_GATHER - 1) // MOE_GATHER
    tmx = f_ref.shape[0]

    @pl.when((e == 0) & (c == 0))
    def _():
        o_ref[...] = jnp.zeros_like(o_ref)

    @pl.when(c == 0)
    def _():
        rank_t = rank_t_ref[0]

        def gather(s, carry):
            base = pl.multiple_of(s * MOE_GATHER, MOE_GATHER)
            slot = lax.broadcasted_iota(jnp.int32, (MOE_GATHER, tmx), 0) + base
            sel = _onehot(rank_t == slot)
            xs_ref[pl.ds(base, MOE_GATHER), :] = _dot(sel, f_ref[...]).astype(BF16)
            y_ref[pl.ds(base, MOE_GATHER), :] = jnp.zeros((MOE_GATHER, y_ref.shape[1]), F32)
            return carry

        lax.fori_loop(0, n_gather, gather, 0)

    def expert(base, rows):
        x = xs_ref[pl.ds(base, rows), :]
        hid = _silu(_dot(x, wg_ref[0])) * _dot(x, wu_ref[0])
        y_ref[pl.ds(base, rows), :] += _dot(hid.astype(BF16), wd_ref[0])

    big = MOE_ROWS * MOE_BIG

    def big_block(s, carry):
        expert(pl.multiple_of(s * big, big), big)
        return carry

    lax.fori_loop(0, n_rows // MOE_BIG, big_block, 0)
    done = (n_rows // MOE_BIG) * MOE_BIG
    part = MOE_BIG // 2
    while part >= 1:
        take = ((n_rows - done) // part) > 0

        @pl.when(take)
        def _(done=done, part=part):
            expert(pl.multiple_of(done * MOE_ROWS, MOE_ROWS), part * MOE_ROWS)

        done = done + jnp.where(take, part, 0)
        part //= 2

    @pl.when(c == pl.num_programs(2) - 1)
    def _():
        pick = lax.broadcasted_iota(jnp.int32, rank_c_ref.shape, 1) == e
        rank_col = jnp.sum(jnp.where(pick, rank_c_ref[...], 0.0), axis=-1, keepdims=True)
        gate_col = jnp.sum(jnp.where(pick, gate_ref[...], 0.0), axis=-1, keepdims=True)

        def combine(s, carry):
            base = pl.multiple_of(s * MOE_GATHER, MOE_GATHER)
            ys = y_ref[pl.ds(base, MOE_GATHER), :].astype(BF16)
            slot = (lax.broadcasted_iota(jnp.int32, (MOE_COMBINE_ROWS, MOE_GATHER), 1) + base).astype(F32)
            for q in range(tmx // MOE_COMBINE_ROWS):
                rows = slice(q * MOE_COMBINE_ROWS, (q + 1) * MOE_COMBINE_ROWS)
                sel = _onehot(rank_col[rows] == slot)
                o_ref[rows, :] += gate_col[rows] * _dot(sel, ys)
            return carry

        lax.fori_loop(0, n_gather, combine, 0)


def _moe(f, gates, rank, cnt, wg, wu, wd, tmx):
    bsz, l, d = f.shape
    ne, _, dff = wg.shape
    n = bsz * l
    nt = n // tmx
    per_expert = lambda a: jnp.transpose(a.reshape(n, LANES)[:, :ne])
    rank_e = per_expert(rank).astype(jnp.int32)
    counts = cnt.reshape(nt, LANES)[:, :ne].astype(jnp.int32).reshape(nt * ne)
    grid_spec = pltpu.PrefetchScalarGridSpec(
        num_scalar_prefetch=1,
        grid=(nt, ne, dff // MOE_FF),
        in_specs=[
            pl.BlockSpec((tmx, d), lambda i, e, c, cnt: (i, 0)),
            pl.BlockSpec((1, 1, tmx), lambda i, e, c, cnt: (e, 0, i)),
            pl.BlockSpec((tmx, LANES), lambda i, e, c, cnt: (i, 0)),
            pl.BlockSpec((tmx, LANES), lambda i, e, c, cnt: (i, 0)),
            pl.BlockSpec((1, d, MOE_FF), lambda i, e, c, cnt: (e, 0, c)),
            pl.BlockSpec((1, d, MOE_FF), lambda i, e, c, cnt: (e, 0, c)),
            pl.BlockSpec((1, MOE_FF, d), lambda i, e, c, cnt: (e, c, 0)),
        ],
        out_specs=pl.BlockSpec((tmx, d), lambda i, e, c, cnt: (i, 0)),
        scratch_shapes=[pltpu.VMEM((tmx, d), BF16), pltpu.VMEM((tmx, d), F32)],
    )
    return pl.pallas_call(
        _moe_kernel,
        out_shape=jax.ShapeDtypeStruct((n, d), F32),
        grid_spec=grid_spec,
        compiler_params=pltpu.CompilerParams(dimension_semantics=("parallel", "arbitrary", "arbitrary"),
                                             vmem_limit_bytes=MOE_VMEM_LIMIT),
        name="moe_ffn",
    )(counts, f.reshape(n, d), rank_e.reshape(ne, 1, n), rank.reshape(n, LANES), gates.reshape(n, LANES),
      wg.astype(BF16), wu.astype(BF16), wd.astype(BF16)).reshape(bsz, l, d)


def _final_kernel(x_ref, y_ref, mod_ref, fg_ref, o_ref):
    x2 = x_ref[0] + mod_ref[0][5:6] * y_ref[0]
    o_ref[0] = _rms(x2) * fg_ref[...]


def _final(x, y, mod, fg):
    bsz, l, d = x.shape
    tm = _pick_chunk(l, (1024, 512, 256))
    tok = pl.BlockSpec((1, tm, d), lambda b, t: (b, t, 0))
    return pl.pallas_call(
        _final_kernel,
        out_shape=jax.ShapeDtypeStruct((bsz, l, d), F32),
        grid=(bsz, l // tm),
        in_specs=[tok, tok, pl.BlockSpec((1, 6, d), lambda b, t: (b, 0, 0)), _const_spec((1, d))],
        out_specs=tok,
        compiler_params=_params(("parallel", "parallel")),
        name="final_norm",
    )(x, y, mod, fg.reshape(1, d))


def kernel(x, c, ctx, c_ctx, ada_w, ada_b, norm_g, ev_w_in, ev_conv_w, ev_a_log, ev_dt_bias, ev_gdn_norm_g, ev_q_norm_g, ev_w_uq, ev_kv_norm_g, ev_w_ukv, ev_w_out, ev_ffn_wg, ev_ffn_wu, ev_ffn_wd, od_w_qkv, od_rpb, od_w_out, od_router, od_exp_wg, od_exp_wu, od_exp_wd, final_norm_g):
    bsz, l, d = x.shape
    lc = ctx.shape[1]
    assert ada_w.shape[0] == 2 and bsz < MOD_ROWS and lc % TM == 0 and l % TM == 0
    cs = jnp.concatenate([c, c_ctx[None], jnp.zeros((MOD_ROWS - bsz - 1, d), F32)], axis=0)
    mods = _ada_mod(cs, ada_w, ada_b).reshape(2, MOD_ROWS, 6, d)

    win, wq, wqr, wk, wv = _proj0_weights(ev_w_in[0], ev_w_uq[0], ev_w_ukv[0])
    qkv_pre, z, ab, q, k, v = _proj0(ctx, x, mods[0], norm_g[0, 0],
                                     (win, ev_q_norm_g[0], wq, wqr, ev_kv_norm_g[0], wk, wv),
                                     _rope_tables(lc, l))
    o_f, o_b = _gdn(qkv_pre, ab, ev_conv_w[0], ev_a_log[0], ev_dt_bias[0], lc)
    yb = _mla_attention(q, k, v, lc)
    xa = _mix0(o_f, o_b, z, yb, ctx, x, mods[0], ev_gdn_norm_g[0], ev_w_out[0], norm_g[0, 1],
               ev_ffn_wg[0], ev_ffn_wu[0], ev_ffn_wd[0])

    q1, k1, v1 = _proj1(xa, mods[1], norm_g[1, 0], od_w_qkv[0], lc)
    o1 = _nat(q1, k1, v1, od_rpb[0], lc)
    tmx = _pick_chunk(l, MOE_TOKENS)
    x1, f1, gates, rank, cnt = _mix1(o1, xa, mods[1], od_w_out[0], norm_g[1, 1], od_router[0], lc, tmx)
    y = _moe(f1, gates, rank, cnt, od_exp_wg[0], od_exp_wu[0], od_exp_wd[0], tmx)
    return _final(x1, y, mods[1], final_norm_g)
```

```python
import functools

import numpy as np
import jax
import jax.numpy as jnp
from jax import lax
from jax.experimental import pallas as pl
from jax.experimental.pallas import tpu as pltpu

F32 = jnp.float32
BF16 = jnp.bfloat16
HIGHEST = lax.Precision.HIGHEST

GRID_W = 64
NORM_EPS = 1e-6
A_HEADS, A_DK, A_DV = 4, 128, 128
GDN_CHUNK_LOG2 = 7
GDN_CHUNK = 1 << GDN_CHUNK_LOG2
GDN_BASE_LOG2 = 3
B_HEADS, B_Q_RANK, B_KV_RANK, B_NOPE, B_ROPE, B_DV = 8, 384, 256, 64, 32, 64
MLA_VROWS = B_DV + 16
ROPE_THETA = 10000.0
C_HEADS, C_DH = 16, 64
WIN_R, WIN_C = 8, 16
N_EXPERTS = 8
LANES = 128
MOD_ROWS = 8
NEG = -1e30
LOG2E = 1.4426950408889634
VMEM_LIMIT = 56 * 1024 * 1024

TM = 256


def _params(sem):
    return pltpu.CompilerParams(dimension_semantics=sem, vmem_limit_bytes=VMEM_LIMIT)


def _const_spec(shape):
    nd = len(shape)
    return pl.BlockSpec(shape, lambda *_: (0,) * nd, pipeline_mode=pl.Buffered(1))


def _rms(x):
    return x * lax.rsqrt(jnp.mean(x * x, axis=-1, keepdims=True) + NORM_EPS)


def _silu(x):
    return x * jax.nn.sigmoid(x)


def _onehot(mask):
    return jnp.where(mask, 1.0, 0.0).astype(BF16)


def _dot(a, b, **kw):
    return jnp.dot(a, b, preferred_element_type=F32, **kw)


def _dot_nt(a, b, **kw):
    return lax.dot_general(a, b, (((1,), (1,)), ((), ())), preferred_element_type=F32, **kw)


def _ada_kernel(s_ref, w_ref, b_ref, o_ref):
    s = _silu(s_ref[...])
    o_ref[0] = _dot(s, w_ref[0], precision=HIGHEST) + b_ref[0]


def _ada_mod(cs, ada_w, ada_b):
    depth, d, d6 = ada_w.shape
    tn = d6 // 4
    return pl.pallas_call(
        _ada_kernel,
        out_shape=jax.ShapeDtypeStruct((depth, MOD_ROWS, d6), F32),
        grid=(depth, d6 // tn),
        in_specs=[
            pl.BlockSpec((MOD_ROWS, d), lambda i, j: (0, 0)),
            pl.BlockSpec((1, d, tn), lambda i, j: (i, 0, j)),
            pl.BlockSpec((1, 1, tn), lambda i, j: (i, 0, j)),
        ],
        out_specs=pl.BlockSpec((1, MOD_ROWS, tn), lambda i, j: (i, 0, j)),
        compiler_params=_params(("parallel", "parallel")),
        name="ada_mod",
    )(cs, ada_w, ada_b.reshape(depth, 1, d6))


N_QKVZ = 4 * A_HEADS * A_DK
IN_OFF_AB = N_QKVZ
IN_OFF_CQ = IN_OFF_AB + LANES
IN_OFF_CKV = IN_OFF_CQ + B_Q_RANK
IN_OFF_KR = IN_OFF_CKV + B_KV_RANK
IN_COLS = IN_OFF_KR + LANES
HW = B_HEADS * LANES


def _joint_tile(c_ref, x_ref, nct):
    return jnp.where(pl.program_id(1) < nct, c_ref[0], x_ref[0])


def _joint_specs(d, nct):
    return [pl.BlockSpec((1, TM, d), lambda b, t: (b, jnp.minimum(t, nct - 1), 0)),
            pl.BlockSpec((1, TM, d), lambda b, t: (b, jnp.maximum(t - nct, 0), 0))]


def _proj0_kernel(c_ref, x_ref, mod_ref, ng_ref, win_ref, qg_ref, wq_ref, wqr_ref, kvg_ref, wk_ref, wv_ref,
                  vone_ref, cq_ref, sq_ref, ck_ref, sk_ref,
                  qkv_ref, z_ref, ab_ref, q_ref, k_ref, vt_ref, *, nct):
    x = _joint_tile(c_ref, x_ref, nct)
    m = mod_ref[0]
    h = _rms(x) * ng_ref[...] * (1.0 + m[1:2]) + m[0:1]
    p = _dot(h.astype(BF16), win_ref[...])
    qkv_ref[0] = p[:, :3 * A_HEADS * A_DK]
    z_ref[0] = p[:, 3 * A_HEADS * A_DK:N_QKVZ]
    ab_ref[0] = p[:, IN_OFF_AB:IN_OFF_CQ]
    nq = (_rms(p[:, IN_OFF_CQ:IN_OFF_CKV]) * qg_ref[...]).astype(BF16)
    nkv = (_rms(p[:, IN_OFF_CKV:IN_OFF_KR]) * kvg_ref[...]).astype(BF16)
    krp = p[:, IN_OFF_KR:IN_COLS]
    qa = _dot(nq, wq_ref[...])
    qb = _dot(nq, wqr_ref[...])
    kn = _dot(nkv, wk_ref[...])
    vv = _dot(nkv, wv_ref[...]) + vone_ref[...]
    cq, sq, ck, sk = cq_ref[...], sq_ref[...], ck_ref[...], sk_ref[...]
    kr = pltpu.roll(krp, B_NOPE, axis=1) * ck + pltpu.roll(krp, B_NOPE - B_ROPE, axis=1) * sk
    for hh in range(B_HEADS):
        sl = slice(hh * LANES, (hh + 1) * LANES)
        q_ref[0, hh] = (qa[:, sl] * cq + qb[:, sl] * sq).astype(BF16)
        k_ref[0, hh] = (kn[:, sl] + kr).astype(BF16)
        vt_ref[0, hh] = vv[:, sl].T[:MLA_VROWS].astype(BF16)


def _rot_cols(w):
    q = B_ROPE // 4
    return jnp.concatenate([-w[:, q:2 * q], w[:, :q], -w[:, 3 * q:], w[:, 2 * q:3 * q]], axis=1)


def _proj0_weights(w_in, w_uq, w_ukv):
    d = w_in.shape[0]
    offs = np.cumsum([0, 512, 512, 512, 512, 8, 8, B_Q_RANK, B_KV_RANK, B_ROPE])
    zeros = lambda n: jnp.zeros((d, n), F32)
    kr = w_in[:, offs[8]:offs[9]]
    win = jnp.concatenate([
        w_in[:, :offs[4]],
        w_in[:, offs[4]:offs[6]], zeros(LANES - 16),
        w_in[:, offs[6]:offs[7]],
        w_in[:, offs[7]:offs[8]],
        kr, _rot_cols(kr), zeros(LANES - 2 * B_ROPE),
    ], axis=1).astype(BF16)
    dq = B_NOPE + B_ROPE
    wq3 = w_uq.reshape(B_Q_RANK, B_HEADS, dq)
    zq = jnp.zeros((B_Q_RANK, B_HEADS, LANES - dq), F32)
    wq = jnp.concatenate([wq3, zq], axis=2).reshape(B_Q_RANK, HW).astype(BF16)
    rot = jnp.stack([_rot_cols(wq3[:, hh, B_NOPE:]) for hh in range(B_HEADS)], axis=1)
    wqr = jnp.concatenate([jnp.zeros((B_Q_RANK, B_HEADS, B_NOPE), F32), rot, zq], axis=2)
    wqr = wqr.reshape(B_Q_RANK, HW).astype(BF16)
    wkv3 = w_ukv.reshape(B_KV_RANK, B_HEADS, B_NOPE + B_DV)
    zk = jnp.zeros((B_KV_RANK, B_HEADS, LANES - B_NOPE), F32)
    wk = jnp.concatenate([wkv3[:, :, :B_NOPE], zk], axis=2).reshape(B_KV_RANK, HW).astype(BF16)
    vpart = wkv3[:, :, B_NOPE:]
    wv = jnp.concatenate([vpart, jnp.zeros_like(vpart)], axis=2).reshape(B_KV_RANK, HW).astype(BF16)
    return win, wq, wqr, wk, wv


def _value_ones():
    vone = np.zeros((1, HW), np.float32)
    for hh in range(B_HEADS):
        vone[0, hh * LANES + B_DV:hh * LANES + MLA_VROWS] = 1.0
    return jnp.asarray(vone)


def _rope_tables(lc, l):
    t = np.arange(l)
    half = B_ROPE // 2
    inv = ROPE_THETA ** (-np.arange(0, half, 2, dtype=np.float64) / half)
    ar = (t // GRID_W)[:, None] * inv[None, :]
    ac = (t % GRID_W)[:, None] * inv[None, :]
    ang = np.concatenate([ar, ar, ac, ac], axis=-1)
    cos = np.concatenate([np.ones((lc, B_ROPE)), np.cos(ang)], axis=0)
    sin = np.concatenate([np.zeros((lc, B_ROPE)), np.sin(ang)], axis=0)
    tt = lc + l
    scale = (B_NOPE + B_ROPE) ** -0.5 * LOG2E
    pad = np.zeros((tt, LANES - B_NOPE - B_ROPE))
    z64 = np.zeros((tt, B_NOPE))
    cq = np.concatenate([np.full((tt, B_NOPE), scale), scale * cos, pad], axis=1)
    sq = np.concatenate([z64, scale * sin, pad], axis=1)
    ck = np.concatenate([z64, cos, pad], axis=1)
    sk = np.concatenate([z64, sin, pad], axis=1)
    return tuple(jnp.asarray(a, F32) for a in (cq, sq, ck, sk))


def _mod_row(b, t, n_ctx_tiles, bsz):
    return jnp.where(t < n_ctx_tiles, bsz, b)


def _proj0(ctx, x, mod, ng, weights, tables):
    bsz, l, d = x.shape
    lc = ctx.shape[1]
    tt = lc + l
    win, qg, wq, wqr, kvg, wk, wv = weights
    vone = _value_ones()
    nct = lc // TM
    tok = lambda w: pl.BlockSpec((1, TM, w), lambda b, t: (b, t, 0))
    head = pl.BlockSpec((1, B_HEADS, TM, LANES), lambda b, t: (b, 0, t, 0))
    head_t = pl.BlockSpec((1, B_HEADS, MLA_VROWS, TM), lambda b, t: (b, 0, 0, t))
    tab = pl.BlockSpec((TM, LANES), lambda b, t: (t, 0))
    hshape = jax.ShapeDtypeStruct((bsz, B_HEADS, tt, LANES), BF16)
    return pl.pallas_call(
        functools.partial(_proj0_kernel, nct=nct),
        out_shape=(
            jax.ShapeDtypeStruct((bsz, tt, 3 * A_HEADS * A_DK), F32),
            jax.ShapeDtypeStruct((bsz, tt, A_HEADS * A_DV), F32),
            jax.ShapeDtypeStruct((bsz, tt, LANES), F32),
            hshape, hshape, jax.ShapeDtypeStruct((bsz, B_HEADS, MLA_VROWS, tt), BF16),
        ),
        grid=(bsz, tt // TM),
        in_specs=_joint_specs(d, nct) + [
            pl.BlockSpec((1, 6, d), lambda b, t: (_mod_row(b, t, nct, bsz), 0, 0)),
            _const_spec((1, d)), _const_spec(win.shape),
            _const_spec((1, B_Q_RANK)), _const_spec(wq.shape), _const_spec(wqr.shape),
            _const_spec((1, B_KV_RANK)), _const_spec(wk.shape), _const_spec(wv.shape),
            _const_spec(vone.shape),
            tab, tab, tab, tab,
        ],
        out_specs=(tok(3 * A_HEADS * A_DK), tok(A_HEADS * A_DV), tok(LANES), head, head, head_t),
        compiler_params=_params(("parallel", "parallel")),
        name="proj0",
    )(ctx, x, mod, ng.reshape(1, d), win, qg.reshape(1, -1), wq, wqr, kvg.reshape(1, -1), wk, wv, vone,
      *tables)


A_CONV = 5
CONV_HALO = 8


def _split_bf16(a):
    hi = a.astype(BF16)
    return hi, (a - hi.astype(F32)).astype(BF16)


def _split3_bf16(a):
    hi, rest = a.astype(BF16), a
    rest = rest - hi.astype(F32)
    mid = rest.astype(BF16)
    return hi, mid, (rest - mid.astype(F32)).astype(BF16)


def _dot3(a, b):
    ah, al = _split_bf16(a)
    bh, bl = _split_bf16(b)
    return _dot(ah, bh) + (_dot(ah, bl) + _dot(al, bh))


def _bdot(a, b):
    return lax.dot_general(a.astype(BF16), b.astype(BF16), (((2,), (1,)), ((0,), (0,))),
                           preferred_element_type=F32)


def _gdn_pre_kernel(x_ref, prev_ref, next_ref, ab_ref, cw_ref, shift_ref, alog_ref, dtb_ref,
                    u_ref, wq_ref, kgt_ref, qk_ref, gl_ref, *, ncc, nc):
    c = GDN_CHUNK
    i = pl.program_id(0)
    bsz = x_ref.shape[0]
    prev_ok = ((i != 0) & (i != ncc)).astype(F32)
    next_ok = ((i != ncc - 1) & (i != nc - 1)).astype(F32)
    cw = cw_ref[...]
    nq = A_HEADS * A_DK
    row = lax.broadcasted_iota(jnp.int32, (c, c), 0)
    col = lax.broadcasted_iota(jnp.int32, (c, c), 1)
    eye = (row == col).astype(F32)
    tri = [((row >= col), (row > col)), ((row <= col), (row < col))]
    ns, rhs, qgs = [], [], []

    def l2n(t):
        return t * lax.rsqrt(jnp.sum(t * t, axis=-1, keepdims=True) + 1e-6)

    for b in range(bsz):
        xe = jnp.concatenate([prev_ref[b] * prev_ok, x_ref[b], next_ref[b] * next_ok], axis=0)
        xw = jnp.concatenate([(xe * cw[j:j + 1]).astype(BF16) for j in range(A_CONV)], axis=0)
        y = _silu(_dot(shift_ref[...], xw))
        ab = ab_ref[b]
        g_all = -jnp.exp(alog_ref[...]) * (jnp.maximum(ab + dtb_ref[...], 0.0)
                                           + jnp.log1p(jnp.exp(-jnp.abs(ab + dtb_ref[...]))))
        beta_all = jax.nn.sigmoid(ab)
        g_parts = _split3_bf16(g_all)
        gt_parts = _split3_bf16(g_all.T)
        g_tot = jnp.sum(g_all, axis=0, keepdims=True)
        heads = []
        for h in range(A_HEADS):
            sl = slice(h * A_DK, (h + 1) * A_DK)
            kh = l2n(y[:, nq + h * A_DK:nq + (h + 1) * A_DK])
            heads.append((l2n(y[:, sl]) * (A_DK ** -0.5), kh, y[:, 2 * nq + h * A_DV:2 * nq + (h + 1) * A_DV],
                          kh.astype(BF16)))
        for d in range(2):
            incl, strict = tri[d]
            m_incl = _onehot(incl)
            gc_col = sum(_dot(m_incl, p) for p in g_parts)
            gc_row = sum(_dot_nt(p, m_incl) for p in gt_parts)
            gl_rows = []
            for h in range(A_HEADS):
                qh, kh, vh, khb = heads[h]
                ln = d * A_HEADS + h
                beta = beta_all[:, 2 * A_HEADS + ln:2 * A_HEADS + ln + 1]
                gcc = gc_col[:, ln:ln + 1]
                gcr = gc_row[ln:ln + 1, :]
                gtot = g_tot[:, ln:ln + 1]
                decay = jnp.where(incl, jnp.exp(jnp.where(incl, gcc - gcr, 0.0)), 0.0)
                n = -jnp.where(strict, beta * _dot_nt(khb, khb) * decay, 0.0)
                eg = jnp.exp(gcc)
                ns.append(n)
                rhs.append(jnp.concatenate([vh * beta, kh * (beta * eg)], axis=1))
                qgs.append(qh * eg)
                kgt_ref[b, d, 0, h] = (kh * jnp.exp(gtot - gcc)).T.astype(BF16)
                qk_ref[b, d, 0, h] = jnp.where(incl, _dot_nt(qh.astype(BF16), khb) * decay, 0.0).astype(BF16)
                gl_rows.append(jnp.broadcast_to(jnp.exp(gtot), (1, LANES)))
            gl_ref[b, d, 0] = jnp.concatenate(gl_rows + gl_rows, axis=0)
    n = jnp.stack(ns, axis=0)
    same = lambda s: lax.shift_right_logical(row, s) == lax.shift_right_logical(col, s)
    nd = jnp.where(same(GDN_BASE_LOG2), n, 0.0)
    x = eye + nd
    p = _bdot(nd, nd)
    z = _bdot(p, jnp.concatenate([p, x], axis=2))
    x = x + z[:, :, c:]
    x = x + _bdot(z[:, :, :c], x)
    for s in range(GDN_BASE_LOG2, GDN_CHUNK_LOG2):
        nl = jnp.where(same(s + 1) & jnp.logical_not(same(s)), n, 0.0)
        x = x + _bdot(_bdot(x, nl), x)
    y = _bdot(x, jnp.stack(rhs, axis=0))
    for b in range(bsz):
        for d in range(2):
            for h in range(A_HEADS):
                k = (b * 2 + d) * A_HEADS + h
                u_ref[b, d, 0, h] = y[k, :, :A_DV]
                wq_ref[b, d, 0, h] = jnp.concatenate([y[k, :, A_DV:], qgs[k]], axis=0).astype(BF16)


def _gdn_scan_kernel(*refs, bsz):
    ins, (of_ref, ob_ref, s_ref) = refs[:10], refs[10:]

    @pl.when(pl.program_id(0) == 0)
    def _():
        s_ref[...] = jnp.zeros_like(s_ref)

    c = GDN_CHUNK
    n = bsz * A_HEADS
    bdot = lambda a, b: lax.dot_general(a, b, (((2,), (1,)), ((0,), (0,))), preferred_element_type=F32)
    flat = lambda ref: ref[:, 0, 0].reshape(n, *ref.shape[-2:])
    both = lambda k: jnp.concatenate([flat(ins[k]), flat(ins[5 + k])], axis=0)
    u, wq, kgt, qk = both(0), both(1), both(2), both(3)
    gl = jnp.concatenate([ins[4][:, 0, 0, :A_HEADS, :].reshape(n, 1, LANES),
                          ins[9][:, 0, 0, :A_HEADS, :].reshape(n, 1, LANES)], axis=0)
    s = s_ref[...]
    ws = bdot(wq, s.astype(BF16))
    v_new = (u - ws[:, :c]).astype(BF16)
    o = ws[:, c:] + bdot(qk, v_new)
    s_ref[...] = s * gl + bdot(kgt, v_new)
    for d, o_ref in ((0, of_ref), (1, ob_ref)):
        for b in range(bsz):
            for h in range(A_HEADS):
                o_ref[b, :, h * A_DV:(h + 1) * A_DV] = o[(d * bsz + b) * A_HEADS + h]


def _gdn(qkv_pre, ab, conv_w, a_log, dt_bias, lc):
    bsz, tt, wqkv = qkv_pre.shape
    c = GDN_CHUNK
    nc, ncc = tt // c, lc // c
    hb = c // CONV_HALO
    per = lambda r, w: pl.BlockSpec((bsz, 2, 1, A_HEADS, r, w), lambda i: (0, 0, i, 0, 0, 0))
    shp = lambda r, w, dt: jax.ShapeDtypeStruct((bsz, 2, nc, A_HEADS, r, w), dt)
    lane_row = lambda t: jnp.concatenate([t.reshape(1, -1), jnp.zeros((1, LANES - t.size), F32)], axis=1)
    cw = jnp.concatenate([conv_w, jnp.zeros((CONV_HALO - A_CONV, wqkv), F32)], axis=0)
    rows = c + 2 * CONV_HALO
    shift = np.zeros((c, A_CONV * rows), np.float32)
    for j in range(A_CONV):
        shift[np.arange(c), j * rows + np.arange(c) + CONV_HALO - A_CONV // 2 + j] = 1.0
    shift = jnp.asarray(shift, BF16)
    pre = pl.pallas_call(
        functools.partial(_gdn_pre_kernel, ncc=ncc, nc=nc),
        out_shape=(shp(c, A_DV, F32), shp(2 * c, A_DK, BF16), shp(A_DK, c, BF16), shp(c, c, BF16),
                   jax.ShapeDtypeStruct((bsz, 2, nc, 2 * A_HEADS, LANES), F32)),
        grid=(nc,),
        in_specs=[pl.BlockSpec((bsz, c, wqkv), lambda i: (0, i, 0)),
                  pl.BlockSpec((bsz, CONV_HALO, wqkv), lambda i: (0, jnp.maximum(i * hb - 1, 0), 0)),
                  pl.BlockSpec((bsz, CONV_HALO, wqkv), lambda i: (0, jnp.minimum((i + 1) * hb, nc * hb - 1), 0)),
                  pl.BlockSpec((bsz, c, LANES), lambda i: (0, i, 0)),
                  _const_spec(cw.shape), _const_spec(shift.shape), _const_spec((1, LANES)),
                  _const_spec((1, LANES))],
        out_specs=(per(c, A_DV), per(2 * c, A_DK), per(A_DK, c), per(c, c),
                   pl.BlockSpec((bsz, 2, 1, 2 * A_HEADS, LANES), lambda i: (0, 0, i, 0, 0))),
        compiler_params=_params(("parallel",)),
        name="gdn_pre",
    )(qkv_pre, qkv_pre, qkv_pre, ab, cw, shift, lane_row(a_log), lane_row(dt_bias))

    def chunk(d, s):
        return s if d == 0 else jnp.where(s < ncc, ncc - 1 - s, nc - 1 - (s - ncc))

    in_specs = []
    for d in range(2):
        for arr in pre[:4]:
            r, w = arr.shape[-2:]
            in_specs.append(pl.BlockSpec((bsz, 1, 1, A_HEADS, r, w),
                                         lambda s, d=d: (0, d, chunk(d, s), 0, 0, 0)))
        in_specs.append(pl.BlockSpec((bsz, 1, 1, 2 * A_HEADS, LANES), lambda s, d=d: (0, d, chunk(d, s), 0, 0)))
    o_shape = jax.ShapeDtypeStruct((bsz, tt, A_HEADS * A_DV), F32)
    return pl.pallas_call(
        functools.partial(_gdn_scan_kernel, bsz=bsz),
        out_shape=(o_shape, o_shape),
        grid=(nc,),
        in_specs=in_specs,
        out_specs=tuple(pl.BlockSpec((bsz, c, A_HEADS * A_DV), lambda s, d=d: (0, chunk(d, s), 0))
                        for d in range(2)),
        scratch_shapes=[pltpu.VMEM((2 * bsz * A_HEADS, A_DK, A_DV), F32)],
        compiler_params=_params(("arbitrary",)),
        name="gdn_scan",
    )(*pre, *pre)


def _pick_chunk(n, options):
    for o in options:
        if n % o == 0:
            return o
    raise ValueError(f"no chunk size in {options} divides {n}")


def _mla_kernel(q_ref, k_ref, vt_ref, o_ref, sa_ref, sb_ref, pa_ref, pb_ref, *, lc, tq, tk, nk):
    t = pl.program_id(2)
    g = q_ref.shape[1]
    bnt = lambda a, b: lax.dot_general(a, b, (((2,), (2,)), ((0,), (0,))), preferred_element_type=F32)
    bnn = lambda a, b: lax.dot_general(a, b, (((2,), (1,)), ((0,), (0,))), preferred_element_type=F32)

    def absorb(s, m, acc, vt_chunk):
        m_new = jnp.maximum(m, jnp.max(s, axis=1, keepdims=True))
        p = jnp.exp2(s - m_new).astype(BF16)
        return m_new, jnp.exp2(m - m_new) * acc + bnn(vt_chunk, p)

    def update(q, m, acc, k_chunk, vt_chunk):
        return absorb(bnt(k_chunk, q), m, acc, vt_chunk)

    def finish(acc):
        parts = [acc[h, :B_DV] / acc[h, B_DV:B_DV + 1] for h in range(g)]
        o_ref[0] = jnp.concatenate(parts, axis=0).T

    init = (jnp.full((g, 1, tq), -jnp.inf, F32), jnp.zeros((g, MLA_VROWS, tq), F32))

    @pl.when(t < lc // tq)
    def _():
        finish(update(q_ref[0], *init, k_ref[0, :, :lc, :], vt_ref[0, :, :, :lc])[1])

    @pl.when(t >= lc // tq)
    def _():
        q = q_ref[0]

        s_bufs, p_bufs = (sa_ref, sb_ref), (pa_ref, pb_ref)
        chunk = lambda c: pl.ds(pl.multiple_of(c * tk, tk), tk)

        def stage_s(c, par):
            s_bufs[par][...] = bnt(k_ref[0, :, chunk(c), :], q)

        def stage_x(par, m):
            s = s_bufs[par][...]
            m_new = jnp.maximum(m, jnp.max(s, axis=1, keepdims=True))
            p_bufs[par][...] = jnp.exp2(s - m_new).astype(BF16)
            return m_new, jnp.exp2(m - m_new)

        def stage_v(c, par, alpha, acc):
            return alpha * acc + bnn(vt_ref[0, :, :, chunk(c)], p_bufs[par][...])

        def step(c, par, state, do_s=True, do_x=True):
            m, alpha, acc = state
            if do_s:
                stage_s(c + 2, par)
            if do_x:
                m_next, alpha_next = stage_x(1 - par, m)
            else:
                m_next, alpha_next = m, alpha
            return m_next, alpha_next, stage_v(c, par, alpha, acc)

        m0, acc0 = init
        stage_s(0, 0)
        if nk > 1:
            stage_s(1, 1)
        m1, alpha1 = stage_x(0, m0)
        state = (m1, alpha1, acc0)
        pairs = max(nk - 2, 0) // 2

        def body(i, state):
            return step(2 * i + 1, 1, step(2 * i, 0, state))

        state = lax.fori_loop(0, pairs, body, state)
        for c in range(2 * pairs, nk):
            state = step(c, c % 2, state, do_s=c + 2 < nk, do_x=c + 1 < nk)
        finish(state[2])


MLA_G = 8
MLA_VMEM_LIMIT = 60 * 1024 * 1024


def _mla_attention(q, k, vt, lc):
    bsz, nh, tt, _ = q.shape
    tq = TM
    tk = _pick_chunk(tt, (768, 512, 256))
    g = MLA_G
    kern = functools.partial(_mla_kernel, lc=lc, tq=tq, tk=tk, nk=tt // tk)
    return pl.pallas_call(
        kern,
        out_shape=jax.ShapeDtypeStruct((bsz, tt, nh * B_DV), F32),
        grid=(bsz, nh // g, tt // tq),
        in_specs=[pl.BlockSpec((1, g, tq, LANES), lambda b, hp, t: (b, hp, t, 0)),
                  pl.BlockSpec((1, g, tt, LANES), lambda b, hp, t: (b, hp, 0, 0), pipeline_mode=pl.Buffered(1)),
                  pl.BlockSpec((1, g, MLA_VROWS, tt), lambda b, hp, t: (b, hp, 0, 0),
                               pipeline_mode=pl.Buffered(1))],
        out_specs=pl.BlockSpec((1, tq, g * B_DV), lambda b, hp, t: (b, t, hp)),
        scratch_shapes=[pltpu.VMEM((g, tk, tq), F32), pltpu.VMEM((g, tk, tq), F32),
                        pltpu.VMEM((g, tk, tq), BF16), pltpu.VMEM((g, tk, tq), BF16)],
        compiler_params=pltpu.CompilerParams(dimension_semantics=("parallel", "parallel", "parallel"),
                                             vmem_limit_bytes=MLA_VMEM_LIMIT),
        name="mla_attention",
    )(q, k, vt)


def _mix0_kernel(of_ref, ob_ref, z_ref, yb_ref, c_ref, x_ref, mod_ref, gg_ref, woa_ref, wob_ref, ng_ref,
                 wg_ref, wu_ref, wd_ref, o_ref, *, nct):
    o = of_ref[0] + ob_ref[0]
    z = z_ref[0]
    gg = gg_ref[...]
    parts = []
    for h in range(A_HEADS):
        sl = slice(h * A_DV, (h + 1) * A_DV)
        parts.append((_rms(o[:, sl]) * gg * _silu(z[:, sl])).astype(BF16))
    ya = jnp.concatenate(parts, axis=1)
    y = _dot(ya, woa_ref[...]) + _dot(yb_ref[0].astype(BF16), wob_ref[...])
    m = mod_ref[0]
    x1 = _joint_tile(c_ref, x_ref, nct) + m[2:3] * y
    f = (_rms(x1) * ng_ref[...] * (1.0 + m[4:5]) + m[3:4]).astype(BF16)
    hid = _silu(_dot(f, wg_ref[...])) * _dot(f, wu_ref[...])
    o_ref[0] = x1 + m[5:6] * _dot(hid.astype(BF16), wd_ref[...])


def _mix0(o_f, o_b, z, yb, ctx, x, mod, gdn_g, w_out, ng, wg, wu, wd):
    bsz, l, d = x.shape
    lc = ctx.shape[1]
    tt = lc + l
    nct = lc // TM
    na = A_HEADS * A_DV
    woa, wob = w_out[:na].astype(BF16), w_out[na:].astype(BF16)
    wg, wu, wd = wg.astype(BF16), wu.astype(BF16), wd.astype(BF16)
    tok = lambda w: pl.BlockSpec((1, TM, w), lambda b, t: (b, t, 0))
    return pl.pallas_call(
        functools.partial(_mix0_kernel, nct=nct),
        out_shape=jax.ShapeDtypeStruct((bsz, tt, d), F32),
        grid=(bsz, tt // TM),
        in_specs=[tok(na), tok(na), tok(na), tok(B_HEADS * B_DV)] + _joint_specs(d, nct) + [
            pl.BlockSpec((1, 6, d), lambda b, t: (_mod_row(b, t, nct, bsz), 0, 0)),
            _const_spec((1, A_DV)), _const_spec(woa.shape), _const_spec(wob.shape), _const_spec((1, d)),
            _const_spec(wg.shape), _const_spec(wu.shape), _const_spec(wd.shape),
        ],
        out_specs=tok(d),
        compiler_params=_params(("parallel", "parallel")),
        name="mix0_ffn",
    )(o_f, o_b, z, yb, ctx, x, mod, gdn_g.reshape(1, -1), woa, wob, ng.reshape(1, d), wg, wu, wd)


def _proj1_kernel(x_ref, mod_ref, ng_ref, w_ref, q_ref, k_ref, v_ref):
    m = mod_ref[0]
    h = (_rms(x_ref[0]) * ng_ref[...] * (1.0 + m[1:2]) + m[0:1]).astype(BF16)
    p = _dot(h, w_ref[...])
    d = q_ref.shape[-1]
    q_ref[0] = (p[:, :d] * (C_DH ** -0.5 * LOG2E)).astype(BF16)
    k_ref[0] = p[:, d:2 * d].astype(BF16)
    v_ref[0] = p[:, 2 * d:].astype(BF16)


def _proj1(xa, mod, ng, w_qkv, lc):
    bsz, tt, d = xa.shape
    nct = lc // TM
    w = w_qkv.astype(BF16)
    tok = pl.BlockSpec((1, TM, d), lambda b, t: (b, t, 0))
    lat = pl.BlockSpec((1, TM, d), lambda b, t: (b, jnp.maximum(t - nct, 0), 0))
    shp = jax.ShapeDtypeStruct((bsz, tt, d), BF16)
    return pl.pallas_call(
        _proj1_kernel,
        out_shape=(jax.ShapeDtypeStruct((bsz, tt - lc, d), BF16), shp, shp),
        grid=(bsz, tt // TM),
        in_specs=[tok, pl.BlockSpec((1, 6, d), lambda b, t: (_mod_row(b, t, nct, bsz), 0, 0)),
                  _const_spec((1, d)), _const_spec(w.shape)],
        out_specs=(lat, tok, tok),
        compiler_params=_params(("parallel", "arbitrary")),
        name="proj1",
    )(xa, mod, ng.reshape(1, d), w)


NAT_HG = 4
NAT_KEYS = WIN_R * GRID_W


def _nat_bias(rpb):
    qc = np.arange(GRID_W)[:, None]
    kc = np.arange(GRID_W)[None, :]
    c0 = np.clip(qc - WIN_C // 2, 0, GRID_W - WIN_C)
    inside = (kc >= c0) & (kc < c0 + WIN_C)
    dc = kc - qc + (WIN_C - 1)
    onehot = ((np.arange(2 * WIN_C - 1)[:, None, None] == dc[None]) & inside[None]).astype(np.float32)
    tab = jnp.einsum('hrd,dqk->hqrk', rpb, jnp.asarray(onehot), precision=HIGHEST)
    tab = tab + jnp.asarray(np.where(inside, 0.0, NEG).astype(np.float32))[None, :, None, :]
    tab = jnp.stack([tab[:, :, WIN_R - 1 - v:2 * WIN_R - 1 - v, :] for v in range(WIN_R)], axis=0)
    return tab.reshape(WIN_R, rpb.shape[0], GRID_W, NAT_KEYS)


NAT_RB = (32, 16, 8, 4, 2)


def _nat_kernel(q_ref, k_ref, v_ref, bias_ref, o_ref, s1_ref, s2_ref, p1_ref, p2_ref, l_ref, *, lc, rows,
                nrb):
    i = pl.program_id(2)
    lane = lax.broadcasted_iota(jnp.int32, (GRID_W, NAT_HG * C_DH), 1)
    masks = [(lane >= h * C_DH) & (lane < (h + 1) * C_DH) for h in range(NAT_HG)]

    def window(j):
        r = i * nrb + j
        r0 = jnp.clip(r - WIN_R // 2, 0, rows - WIN_R)
        return r - r0, pl.ds(pl.multiple_of(lc + r0 * GRID_W, GRID_W), NAT_KEYS)

    def qrows(j):
        return pl.ds(pl.multiple_of(j * GRID_W, GRID_W), GRID_W)

    def stage_s(j, par):
        variant, keys = window(j)
        q = q_ref[0, qrows(j), :]
        qs = jnp.concatenate([jnp.where(hm, q, jnp.zeros_like(q)) for hm in masks], axis=0)
        s1_ref[par] = _dot_nt(qs, k_ref[0, keys, :]) + bias_ref[variant, 0].reshape(NAT_HG * GRID_W, NAT_KEYS)
        s2_ref[par] = _dot_nt(qs, k_ref[0, :lc, :])

    def stage_x(par):
        s1, s2 = s1_ref[par], s2_ref[par]
        m = jnp.maximum(jnp.max(s1, axis=-1, keepdims=True), jnp.max(s2, axis=-1, keepdims=True))
        e1 = jnp.exp2(s1 - m)
        e2 = jnp.exp2(s2 - m)
        l_ref[par] = jnp.sum(e1, axis=-1, keepdims=True) + jnp.sum(e2, axis=-1, keepdims=True)
        p1_ref[par] = e1.astype(BF16)
        p2_ref[par] = e2.astype(BF16)

    def stage_v(j, par):
        _, keys = window(j)
        o = (_dot(p1_ref[par], v_ref[0, keys, :]) + _dot(p2_ref[par], v_ref[0, :lc, :])) / l_ref[par]
        acc = jnp.zeros((GRID_W, NAT_HG * C_DH), F32)
        for h, hm in enumerate(masks):
            acc = jnp.where(hm, o[h * GRID_W:(h + 1) * GRID_W], acc)
        o_ref[0, qrows(j), :] = acc.astype(BF16)

    def step(j, par, do_s=True, do_x=True):
        if do_s:
            stage_s(j + 2, par)
        if do_x:
            stage_x(1 - par)
        stage_v(j, par)

    stage_s(0, 0)
    stage_s(1, 1)
    stage_x(0)

    def body(t, carry):
        step(2 * t, 0)
        step(2 * t + 1, 1)
        return carry

    pairs = (nrb - 2) // 2
    lax.fori_loop(0, pairs, body, 0)
    for j in range(2 * pairs, nrb):
        step(j, j % 2, do_s=j + 2 < nrb, do_x=j + 1 < nrb)


def _nat(q, k, v, rpb, lc):
    bsz, tt, d = k.shape
    l = tt - lc
    rows = l // GRID_W
    assert rows >= WIN_R and q.shape[1] == l
    nrb = _pick_chunk(rows, NAT_RB)
    hw = NAT_HG * C_DH
    ng = d // hw
    bias = (_nat_bias(rpb) * LOG2E).reshape(WIN_R, ng, NAT_HG, GRID_W, NAT_KEYS)
    kern = functools.partial(_nat_kernel, lc=lc, rows=rows, nrb=nrb)
    qb = nrb * GRID_W
    kv = pl.BlockSpec((1, tt, hw), lambda b, g, r: (b, 0, g))
    return pl.pallas_call(
        kern,
        out_shape=jax.ShapeDtypeStruct((bsz, l, d), BF16),
        grid=(bsz, ng, rows // nrb),
        in_specs=[
            pl.BlockSpec((1, qb, hw), lambda b, g, r: (b, r, g)),
            kv, kv,
            pl.BlockSpec((WIN_R, 1, NAT_HG, GRID_W, NAT_KEYS), lambda b, g, r: (0, g, 0, 0, 0)),
        ],
        out_specs=pl.BlockSpec((1, qb, hw), lambda b, g, r: (b, r, g)),
        scratch_shapes=[pltpu.VMEM((2, NAT_HG * GRID_W, NAT_KEYS), F32), pltpu.VMEM((2, NAT_HG * GRID_W, lc), F32),
                        pltpu.VMEM((2, NAT_HG * GRID_W, NAT_KEYS), BF16), pltpu.VMEM((2, NAT_HG * GRID_W, lc), BF16),
                        pltpu.VMEM((2, NAT_HG * GRID_W, 1), F32)],
        compiler_params=_params(("parallel", "parallel", "arbitrary")),
        name="nat",
    )(q, k, v, bias)


def _mix1_kernel(o_ref, x_ref, mod_ref, wo_ref, ng_ref, wr_ref, x_out, f_out, w_out, rank_out, cnt_out,
                 carry_ref, *, group):
    t = pl.program_id(1)
    m = mod_ref[0]
    x1 = x_ref[0] + m[2:3] * _dot(o_ref[0], wo_ref[...])
    x_out[0] = x1
    f = _rms(x1) * ng_ref[...] * (1.0 + m[4:5]) + m[3:4]
    f_out[0] = f.astype(BF16)
    logits = _dot3(f, wr_ref[...])
    lane = lax.broadcasted_iota(jnp.int32, logits.shape, 1)
    logits = jnp.where(lane < N_EXPERTS, logits, -jnp.inf)
    m1 = jnp.max(logits, axis=-1, keepdims=True)
    i1 = jnp.min(jnp.where(logits == m1, lane, LANES), axis=-1, keepdims=True)
    rest = jnp.where(lane == i1, -jnp.inf, logits)
    m2 = jnp.max(rest, axis=-1, keepdims=True)
    i2 = jnp.min(jnp.where(rest == m2, lane, LANES), axis=-1, keepdims=True)
    e2 = jnp.exp(m2 - m1)
    g1 = 1.0 / (1.0 + e2)
    w_out[0] = jnp.where(lane == i1, g1, 0.0) + jnp.where(lane == i2, e2 * g1, 0.0)

    @pl.when(t % group == 0)
    def _():
        carry_ref[...] = jnp.zeros_like(carry_ref)

    sel = (lane == i1) | (lane == i2)
    tm = logits.shape[0]
    tri = lax.broadcasted_iota(jnp.int32, (tm, tm), 0) >= lax.broadcasted_iota(jnp.int32, (tm, tm), 1)
    csum = _dot(_onehot(tri), _onehot(sel))
    carry = carry_ref[...]
    rank_out[0] = jnp.where(sel, carry + csum - 1.0, -1.0)
    carry = carry + csum[tm - 1:tm, :]
    carry_ref[...] = carry
    cnt_out[0, 0] = carry


def _mix1(o, xa, mod, w_out, ng, w_router, lc, tmx):
    bsz, tt, d = xa.shape
    l = tt - lc
    nct = lc // TM
    group = tmx // TM
    wr = jnp.concatenate([w_router, jnp.zeros((d, LANES - N_EXPERTS), F32)], axis=1)
    tok = lambda w: pl.BlockSpec((1, TM, w), lambda b, t: (b, t, 0))
    return pl.pallas_call(
        functools.partial(_mix1_kernel, group=group),
        out_shape=(jax.ShapeDtypeStruct((bsz, l, d), F32), jax.ShapeDtypeStruct((bsz, l, d), BF16),
                   jax.ShapeDtypeStruct((bsz, l, LANES), F32), jax.ShapeDtypeStruct((bsz, l, LANES), F32),
                   jax.ShapeDtypeStruct((bsz, l // tmx, 1, LANES), F32)),
        grid=(bsz, l // TM),
        in_specs=[tok(d), pl.BlockSpec((1, TM, d), lambda b, t: (b, t + nct, 0)),
                  pl.BlockSpec((1, 6, d), lambda b, t: (b, 0, 0)),
                  _const_spec((d, d)), _const_spec((1, d)), _const_spec(wr.shape)],
        out_specs=(tok(d), tok(d), tok(LANES), tok(LANES),
                   pl.BlockSpec((1, 1, 1, LANES), lambda b, t: (b, t // group, 0, 0))),
        scratch_shapes=[pltpu.VMEM((1, LANES), F32)],
        compiler_params=_params(("parallel", "arbitrary")),
        name="mix1_router",
    )(o, xa, mod, w_out.astype(BF16), ng.reshape(1, d), wr)


MOE_TOKENS = (2048, 1024)
MOE_GATHER = 256
MOE_ROWS = 128
MOE_BIG = 4
MOE_FF = 512
MOE_COMBINE_ROWS = 512
MOE_VMEM_LIMIT = 60 * 1024 * 1024


def _moe_kernel(cnt_ref, f_ref, rank_t_ref, rank_c_ref, gate_ref, wg_ref, wu_ref, wd_ref, o_ref, xs_ref, y_ref):
    i, e, c = pl.program_id(0), pl.program_id(1), pl.program_id(2)
    n = cnt_ref[i * pl.num_programs(1) + e]
    n_rows = (n + MOE_ROWS - 1) // MOE_ROWS
    n_gather = (n + MOE_GATHER - 1) // MOE_GATHER
    tmx = f_ref.shape[0]

    @pl.when((e == 0) & (c == 0))
    def _():
        o_ref[...] = jnp.zeros_like(o_ref)

    @pl.when(c == 0)
    def _():
        rank_t = rank_t_ref[0]

        def gather(s, carry):
            base = pl.multiple_of(s * MOE_GATHER, MOE_GATHER)
            slot = lax.broadcasted_iota(jnp.int32, (MOE_GATHER, tmx), 0) + base
            sel = _onehot(rank_t == slot)
            xs_ref[pl.ds(base, MOE_GATHER), :] = _dot(sel, f_ref[...]).astype(BF16)
            y_ref[pl.ds(base, MOE_GATHER), :] = jnp.zeros((MOE_GATHER, y_ref.shape[1]), F32)
            return carry

        lax.fori_loop(0, n_gather, gather, 0)

    def expert(base, rows):
        x = xs_ref[pl.ds(base, rows), :]
        hid = _silu(_dot(x, wg_ref[0])) * _dot(x, wu_ref[0])
        y_ref[pl.ds(base, rows), :] += _dot(hid.astype(BF16), wd_ref[0])

    big = MOE_ROWS * MOE_BIG
    n_big, rem = n_rows // MOE_BIG, n_rows % MOE_BIG
    n_loop = jnp.maximum(n_big - 1, 0)

    def big_block(s, carry):
        expert(pl.multiple_of(s * big, big), big)
        return carry

    lax.fori_loop(0, n_loop, big_block, 0)
    last_blocks = jnp.where(n_big > 0, MOE_BIG + rem, rem)
    for size in range(1, 2 * MOE_BIG):
        @pl.when(last_blocks == size)
        def _(size=size):
            expert(pl.multiple_of(n_loop * big, big), size * MOE_ROWS)

    @pl.when(c == pl.num_programs(2) - 1)
    def _():
        pick = lax.broadcasted_iota(jnp.int32, rank_c_ref.shape, 1) == e
        rank_col = jnp.sum(jnp.where(pick, rank_c_ref[...], 0.0), axis=-1, keepdims=True)
        gate_col = jnp.sum(jnp.where(pick, gate_ref[...], 0.0), axis=-1, keepdims=True)

        def combine(s, carry):
            base = pl.multiple_of(s * MOE_GATHER, MOE_GATHER)
            ys = y_ref[pl.ds(base, MOE_GATHER), :].astype(BF16)
            slot = (lax.broadcasted_iota(jnp.int32, (MOE_COMBINE_ROWS, MOE_GATHER), 1) + base).astype(F32)
            for q in range(tmx // MOE_COMBINE_ROWS):
                rows = slice(q * MOE_COMBINE_ROWS, (q + 1) * MOE_COMBINE_ROWS)
                sel = _onehot(rank_col[rows] == slot)
                o_ref[rows, :] += gate_col[rows] * _dot(sel, ys)
            return carry

        lax.fori_loop(0, n_gather, combine, 0)


def _moe(f, gates, rank, cnt, wg, wu, wd, tmx):
    bsz, l, d = f.shape
    ne, _, dff = wg.shape
    n = bsz * l
    nt = n // tmx
    per_expert = lambda a: jnp.transpose(a.reshape(n, LANES)[:, :ne])
    rank_e = per_expert(rank).astype(jnp.int32)
    counts = cnt.reshape(nt, LANES)[:, :ne].astype(jnp.int32).reshape(nt * ne)
    grid_spec = pltpu.PrefetchScalarGridSpec(
        num_scalar_prefetch=1,
        grid=(nt, ne, dff // MOE_FF),
        in_specs=[
            pl.BlockSpec((tmx, d), lambda i, e, c, cnt: (i, 0)),
            pl.BlockSpec((1, 1, tmx), lambda i, e, c, cnt: (e, 0, i)),
            pl.BlockSpec((tmx, LANES), lambda i, e, c, cnt: (i, 0)),
            pl.BlockSpec((tmx, LANES), lambda i, e, c, cnt: (i, 0)),
            pl.BlockSpec((1, d, MOE_FF), lambda i, e, c, cnt: (e, 0, c)),
            pl.BlockSpec((1, d, MOE_FF), lambda i, e, c, cnt: (e, 0, c)),
            pl.BlockSpec((1, MOE_FF, d), lambda i, e, c, cnt: (e, c, 0)),
        ],
        out_specs=pl.BlockSpec((tmx, d), lambda i, e, c, cnt: (i, 0)),
        scratch_shapes=[pltpu.VMEM((tmx, d), BF16), pltpu.VMEM((tmx, d), F32)],
    )
    return pl.pallas_call(
        _moe_kernel,
        out_shape=jax.ShapeDtypeStruct((n, d), F32),
        grid_spec=grid_spec,
        compiler_params=pltpu.CompilerParams(dimension_semantics=("parallel", "arbitrary", "arbitrary"),
                                             vmem_limit_bytes=MOE_VMEM_LIMIT),
        name="moe_ffn",
    )(counts, f.reshape(n, d), rank_e.reshape(ne, 1, n), rank.reshape(n, LANES), gates.reshape(n, LANES),
      wg.astype(BF16), wu.astype(BF16), wd.astype(BF16)).reshape(bsz, l, d)


def _final_kernel(x_ref, y_ref, mod_ref, fg_ref, o_ref):
    x2 = x_ref[0] + mod_ref[0][5:6] * y_ref[0]
    o_ref[0] = _rms(x2) * fg_ref[...]


def _final(x, y, mod, fg):
    bsz, l, d = x.shape
    tm = _pick_chunk(l, (1024, 512, 256))
    tok = pl.BlockSpec((1, tm, d), lambda b, t: (b, t, 0))
    return pl.pallas_call(
        _final_kernel,
        out_shape=jax.ShapeDtypeStruct((bsz, l, d), F32),
        grid=(bsz, l // tm),
        in_specs=[tok, tok, pl.BlockSpec((1, 6, d), lambda b, t: (b, 0, 0)), _const_spec((1, d))],
        out_specs=tok,
        compiler_params=_params(("parallel", "parallel")),
        name="final_norm",
    )(x, y, mod, fg.reshape(1, d))


def kernel(x, c, ctx, c_ctx, ada_w, ada_b, norm_g, ev_w_in, ev_conv_w, ev_a_log, ev_dt_bias, ev_gdn_norm_g, ev_q_norm_g, ev_w_uq, ev_kv_norm_g, ev_w_ukv, ev_w_out, ev_ffn_wg, ev_ffn_wu, ev_ffn_wd, od_w_qkv, od_rpb, od_w_out, od_router, od_exp_wg, od_exp_wu, od_exp_wd, final_norm_g):
    bsz, l, d = x.shape
    lc = ctx.shape[1]
    assert ada_w.shape[0] == 2 and bsz < MOD_ROWS and lc % TM == 0 and l % TM == 0
    cs = jnp.concatenate([c, c_ctx[None], jnp.zeros((MOD_ROWS - bsz - 1, d), F32)], axis=0)
    mods = _ada_mod(cs, ada_w, ada_b).reshape(2, MOD_ROWS, 6, d)

    win, wq, wqr, wk, wv = _proj0_weights(ev_w_in[0], ev_w_uq[0], ev_w_ukv[0])
    qkv_pre, z, ab, q, k, v = _proj0(ctx, x, mods[0], norm_g[0, 0],
                                     (win, ev_q_norm_g[0], wq, wqr, ev_kv_norm_g[0], wk, wv),
                                     _rope_tables(lc, l))
    o_f, o_b = _gdn(qkv_pre, ab, ev_conv_w[0], ev_a_log[0], ev_dt_bias[0], lc)
    yb = _mla_attention(q, k, v, lc)
    xa = _mix0(o_f, o_b, z, yb, ctx, x, mods[0], ev_gdn_norm_g[0], ev_w_out[0], norm_g[0, 1],
               ev_ffn_wg[0], ev_ffn_wu[0], ev_ffn_wd[0])

    q1, k1, v1 = _proj1(xa, mods[1], norm_g[1, 0], od_w_qkv[0], lc)
    o1 = _nat(q1, k1, v1, od_rpb[0], lc)
    tmx = _pick_chunk(l, MOE_TOKENS)
    x1, f1, gates, rank, cnt = _mix1(o1, xa, mods[1], od_w_out[0], norm_g[1, 1], od_router[0], lc, tmx)
    y = _moe(f1, gates, rank, cnt, od_exp_wg[0], od_exp_wu[0], od_exp_wd[0], tmx)
    return _final(x1, y, mods[1], final_norm_g)
```
